```python
import math
import jax
import jax.numpy as jnp
from jax import lax
import numpy as np

D_MODEL = 1024
BATCH = 2
SEQ = 16384
DEPTH = 2
DEC_BATCH = 16
DEC_SEQ = 16
PAST_LEN = 4096

CHUNK = 64
N_MIX = 4
D_GROUP = D_MODEL // N_MIX
D_MIX = N_MIX * D_GROUP
CONV_W = 4
H_A = 4
DK_A = D_GROUP // H_A
DV_A = D_GROUP // H_A
S5_CH = 16
G_B = D_GROUP // S5_CH
P_B = 64
H_C = 4
BLK_C = D_GROUP // H_C
RG_C = 8.0
H_D = 4
HD_D = D_GROUP // H_D
N_PREV = 8
BAND = (N_PREV + 1) * CHUNK
REL_CLIP = 128
N_MEM = 256
H_X = 4
HD_X = D_MODEL // H_X
D_FF = 4 * D_MODEL
ALPHA = (2.0 * DEPTH) ** 0.25
BETA_INIT = (8.0 * DEPTH) ** -0.25
LN_EPS = 1e-5
NORM_EPS = 1e-6
SPLIT_SIZES = (3 * D_GROUP, D_GROUP, H_A, H_A,
               D_GROUP,
               D_GROUP, D_GROUP,
               3 * D_GROUP)
N_IN = sum(SPLIT_SIZES)
SPLIT_IDX = tuple(sum(SPLIT_SIZES[:i + 1]) for i in range(len(SPLIT_SIZES) - 1))

kernel_name = 'hybrid_streaming_encoder_step'


def layer_norm(x, g, b):
    xf = x.astype(jnp.float32)
    mu = jnp.mean(xf, -1, keepdims=True)
    xc = xf - mu
    var = jnp.mean(xc * xc, -1, keepdims=True)
    return (xc * lax.rsqrt(var + LN_EPS) * g.astype(jnp.float32) + b.astype(jnp.float32)).astype(x.dtype)


def l2_normalize(t):
    return t * lax.rsqrt(jnp.sum(t * t, -1, keepdims=True) + NORM_EPS)


def causal_conv(x, buf, w, b):
    l = x.shape[1]
    xp = jnp.concatenate([buf.astype(x.dtype), x], axis=1)
    wc = w.astype(x.dtype)
    y = xp[:, 0:l] * wc[0]
    for j in range(1, CONV_W):
        y = y + xp[:, j:j + l] * wc[j]
    return y + b.astype(x.dtype), xp[:, l:]


def _linear_combine(e1, e2):
    a1, x1 = e1
    a2, x2 = e2
    return a2 * a1, a2 * x1 + x2


def _complex_linear_combine(e1, e2):
    a1r, a1i, x1r, x1i = e1
    a2r, a2i, x2r, x2i = e2
    return (a2r * a1r - a2i * a1i, a2r * a1i + a2i * a1r,
            a2r * x1r - a2i * x1i + x2r, a2r * x1i + a2i * x1r + x2i)


def gated_delta_rule(q, k, v, log_a, beta, s0, chunk):
    b, l, h, _ = q.shape
    dv = v.shape[-1]
    n = l // chunk

    def to_chunks(t):
        t = t.reshape((b, n, chunk, h) + t.shape[3:])
        return jnp.moveaxis(t, (1, 3), (0, 2))

    qc, kc, vc, bc = to_chunks(q), to_chunks(k), to_chunks(v), to_chunks(beta)
    gc = jnp.cumsum(to_chunks(log_a), axis=-1)
    pos = jnp.arange(chunk)
    causal = pos[:, None] >= pos[None, :]
    strict = pos[:, None] > pos[None, :]
    decay = jnp.exp(jnp.where(causal, gc[..., :, None] - gc[..., None, :], -jnp.inf))
    kk = jnp.einsum('nbhid,nbhjd->nbhij', kc, kc)
    lower = jnp.where(strict, bc[..., :, None] * kk * decay, 0.0) + jnp.eye(chunk, dtype=kk.dtype)
    rhs = jnp.concatenate([vc * bc[..., None], kc * (bc * jnp.exp(gc))[..., None]], axis=-1)
    sol = lax.linalg.triangular_solve(lower, rhs, left_side=True, lower=True, unit_diagonal=True)
    u, w = sol[..., :dv], sol[..., dv:]
    qk = jnp.einsum('nbhid,nbhjd->nbhij', qc, kc) * decay
    q_dec = qc * jnp.exp(gc)[..., None]
    k_dec = kc * jnp.exp(gc[..., -1:] - gc)[..., None]
    g_last = jnp.exp(gc[..., -1])

    def step(s, inp):
        u_i, w_i, qk_i, qd_i, kd_i, gl_i = inp
        v_new = u_i - jnp.einsum('bhck,bhkv->bhcv', w_i, s)
        o_i = jnp.einsum('bhck,bhkv->bhcv', qd_i, s) + jnp.einsum('bhij,bhjv->bhiv', qk_i, v_new)
        s = s * gl_i[..., None, None] + jnp.einsum('bhck,bhcv->bhkv', kd_i, v_new)
        return s, o_i

    s_final, o = lax.scan(step, s0, (u, w, qk, q_dec, k_dec, g_last))
    o = jnp.moveaxis(o, (0, 2), (1, 3)).reshape(b, l, h, dv)
    return o, s_final


def gdn_mixer(qkv, gate, dec_in, beta_in, conv_buf, s0, p):
    b, l, _ = qkv.shape
    f32 = jnp.float32
    y, conv_new = causal_conv(qkv, conv_buf, p['gdn_conv_w'], p['gdn_conv_b'])
    y = jax.nn.silu(y.astype(f32))
    q, k, v = jnp.split(y, 3, axis=-1)
    q = l2_normalize(q.reshape(b, l, H_A, DK_A)) * (DK_A ** -0.5)
    k = l2_normalize(k.reshape(b, l, H_A, DK_A))
    v = v.reshape(b, l, H_A, DV_A)
    log_a = -jnp.exp(p['gdn_a_log'].astype(f32)) * jax.nn.softplus(dec_in.astype(f32) + p['gdn_dt_bias'].astype(f32))
    beta = jax.nn.sigmoid(beta_in.astype(f32))
    chunk = CHUNK if l % CHUNK == 0 else l
    o, s_new = gated_delta_rule(q, k, v, log_a, beta, s0.astype(f32), chunk)
    o = o * lax.rsqrt(jnp.mean(o * o, -1, keepdims=True) + NORM_EPS) * p['gdn_norm_g'].astype(f32)
    o = o.reshape(b, l, D_GROUP) * jax.nn.silu(gate.astype(f32))
    return o.astype(qkv.dtype), conv_new, s_new


def s5_mixer(u, h0, p):
    b, l, _ = u.shape
    f32 = jnp.float32
    uf = u.astype(f32).reshape(b, l, G_B, S5_CH)
    dt = jnp.exp(p['s5_log_dt'].astype(f32))[:, None]
    lr, li = p['s5_lam_re'].astype(f32), p['s5_lam_im'].astype(f32)
    mag = jnp.exp(lr * dt)
    ar, ai = mag * jnp.cos(li * dt), mag * jnp.sin(li * dt)
    den = lr * lr + li * li
    fr = ((ar - 1.0) * lr + ai * li) / den
    fi = (ai * lr - (ar - 1.0) * li) / den
    br, bi = p['s5_b_re'].astype(f32), p['s5_b_im'].astype(f32)
    bbr = fr[..., None] * br - fi[..., None] * bi
    bbi = fr[..., None] * bi + fi[..., None] * br
    xr = jnp.einsum('blgc,gpc->blgp', uf, bbr)
    xi = jnp.einsum('blgc,gpc->blgp', uf, bbi)
    h0f = h0.astype(f32)
    h0r, h0i = h0f[..., 0], h0f[..., 1]
    xr = xr.at[:, 0].add(ar * h0r - ai * h0i)
    xi = xi.at[:, 0].add(ar * h0i + ai * h0r)
    full = xr.shape
    _, _, hr, hi = lax.associative_scan(
        _complex_linear_combine,
        (jnp.broadcast_to(ar, full), jnp.broadcast_to(ai, full), xr, xi), axis=1)
    cr, ci = p['s5_c_re'].astype(f32), p['s5_c_im'].astype(f32)
    y = (jnp.einsum('blgp,gcp->blgc', hr, cr) - jnp.einsum('blgp,gcp->blgc', hi, ci)
         + p['s5_d'].astype(f32) * uf)
    y = jax.nn.gelu(y.reshape(b, l, D_GROUP))
    y = y * jax.nn.sigmoid(y @ p['s5_w_glu'].astype(f32) + p['s5_b_glu'].astype(f32))
    h_last = jnp.stack([hr[:, -1], hi[:, -1]], axis=-1)
    return y.astype(u.dtype), h_last


def rglru_mixer(xb, gb, conv_buf, h0, p):
    b, l, _ = xb.shape
    f32 = jnp.float32
    y, conv_new = causal_conv(xb, conv_buf, p['rg_conv_w'], p['rg_conv_b'])
    yf = y.astype(f32)
    yb = yf.reshape(b, l, H_C, BLK_C)
    r = jax.nn.sigmoid(jnp.einsum('blhi,hij->blhj', yb, p['rg_w_r'].astype(f32)).reshape(b, l, D_GROUP)
                       + p['rg_b_r'].astype(f32))
    i = jax.nn.sigmoid(jnp.einsum('blhi,hij->blhj', yb, p['rg_w_i'].astype(f32)).reshape(b, l, D_GROUP)
                       + p['rg_b_i'].astype(f32))
    log_a = -RG_C * r * jax.nn.softplus(-p['rg_lam'].astype(f32))
    a = jnp.exp(log_a)
    xin = jnp.sqrt(-jnp.expm1(2.0 * log_a)) * (i * yf)
    xin = xin.at[:, 0].add(a[:, 0] * h0.astype(f32))
    _, h = lax.associative_scan(_linear_combine, (a, xin), axis=1)
    out = h * jax.nn.gelu(gb.astype(f32))
    return out.astype(xb.dtype), conv_new, h[:, -1]


def _rel_bias(table, rel):
    idx = jnp.clip(rel, -REL_CLIP, REL_CLIP) + REL_CLIP
    return jnp.moveaxis(table[idx].astype(jnp.float32), -1, 0)


def band_attention_prompt(q, k, v, table):
    b, l, h, d = q.shape
    nc = l // CHUNK
    qc = q.reshape(b, nc, CHUNK, h, d)
    kc = k.reshape(b, nc, CHUNK, h, d)
    vc = v.reshape(b, nc, CHUNK, h, d)
    pad = jnp.zeros((b, N_PREV, CHUNK, h, d), k.dtype)
    kp = jnp.concatenate([pad, kc], axis=1)
    vp = jnp.concatenate([pad, vc], axis=1)
    kb = jnp.concatenate([kp[:, s:s + nc] for s in range(N_PREV + 1)], axis=2)
    vb = jnp.concatenate([vp[:, s:s + nc] for s in range(N_PREV + 1)], axis=2)
    s = jnp.einsum('bnihd,bnjhd->bnhij', qc, kb).astype(jnp.float32) * (d ** -0.5)
    k_off = jnp.arange(BAND) - N_PREV * CHUNK
    s = s + _rel_bias(table, jnp.arange(CHUNK)[:, None] - k_off[None, :])
    valid = (jnp.arange(nc)[:, None] * CHUNK + k_off[None, :]) >= 0
    s = jnp.where(valid[None, :, None, None, :], s, -jnp.inf)
    pr = jax.nn.softmax(s, axis=-1).astype(v.dtype)
    return jnp.einsum('bnhij,bnjhd->bnihd', pr, vb).reshape(b, l, h * d)


def band_attention_sample(q, k, v, k_cache, v_cache, table):
    b, l, h, d = q.shape
    rows = k_cache.shape[1]
    kk = jnp.concatenate([k_cache.astype(k.dtype), k], axis=1)
    vv = jnp.concatenate([v_cache.astype(v.dtype), v], axis=1)
    s = jnp.einsum('bihd,bjhd->bhij', q, kk).astype(jnp.float32) * (d ** -0.5)
    k_pos = jnp.arange(rows + l) - rows
    s = s + _rel_bias(table, jnp.arange(l)[:, None] - k_pos[None, :])
    pr = jax.nn.softmax(s, axis=-1).astype(v.dtype)
    return jnp.einsum('bhij,bjhd->bihd', pr, vv).reshape(b, l, h * d)


def cross_attention(x, mem_k, mem_v, p):
    b, l, _ = x.shape
    q = (x @ p['xa_w_q']).reshape(b, l, H_X, HD_X)
    s = jnp.einsum('blhd,bmhd->bhlm', q, mem_k.astype(q.dtype)).astype(jnp.float32) * (HD_X ** -0.5)
    pr = jax.nn.softmax(s, axis=-1).astype(x.dtype)
    o = jnp.einsum('bhlm,bmhd->blhd', pr, mem_v.astype(x.dtype)).reshape(b, l, H_X * HD_X)
    return o @ p['xa_w_o']


def trunk_layer(x, mem_k, mem_v, gdn_conv, gdn_s, s5_h, rg_conv, rg_h, band_k, band_v, p):
    b, l, _ = x.shape
    proj = x @ p['w_in']
    a_qkv, a_gate, a_dec, a_beta, b_u, c_x, c_g, d_qkv = jnp.split(proj, SPLIT_IDX, axis=-1)
    o_a, gdn_conv_new, gdn_s_new = gdn_mixer(a_qkv, a_gate, a_dec, a_beta, gdn_conv, gdn_s, p)
    o_b, s5_h_new = s5_mixer(b_u, s5_h, p)
    o_c, rg_conv_new, rg_h_new = rglru_mixer(c_x, c_g, rg_conv, rg_h, p)
    q_d, k_d, v_d = [t.reshape(b, l, H_D, HD_D) for t in jnp.split(d_qkv, 3, axis=-1)]
    if band_k is None:
        o_d = band_attention_prompt(q_d, k_d, v_d, p['band_rel_bias'])
        rows = min(N_PREV * CHUNK, l)
        band_k_new, band_v_new = k_d[:, l - rows:], v_d[:, l - rows:]
    else:
        o_d = band_attention_sample(q_d, k_d, v_d, band_k, band_v, p['band_rel_bias'])
        band_k_new, band_v_new = k_d, v_d
    mix = jnp.concatenate([o_a, o_b, o_c, o_d], axis=-1) @ p['w_out']
    x = layer_norm(ALPHA * x + mix, p['ln_g'][0], p['ln_b'][0])
    x = layer_norm(ALPHA * x + cross_attention(x, mem_k, mem_v, p), p['ln_g'][1], p['ln_b'][1])
    hid = jax.nn.relu(x @ p['mlp_w1'])
    x = layer_norm(ALPHA * x + (hid * hid) @ p['mlp_w2'], p['ln_g'][2], p['ln_b'][2])
    return x, (gdn_conv_new, gdn_s_new, s5_h_new, rg_conv_new, rg_h_new, band_k_new, band_v_new)


def setup_inputs(seed: int = 0) -> dict:
    key = jax.random.key(seed)
    ks = iter(jax.random.split(key, 64))
    f32 = jnp.float32

    def nrm(shape, scale=1.0):
        return jax.random.normal(next(ks), shape, f32) * scale

    def unif(shape, lo, hi):
        return jax.random.uniform(next(ks), shape, f32, lo, hi)

    band_rows = min(N_PREV * CHUNK, PAST_LEN)
    dt = jnp.exp(unif((DEPTH, H_A), math.log(1e-3), math.log(1e-1)))
    gdn_dt_bias = dt + jnp.log(-jnp.expm1(-dt))
    a_c = unif((DEPTH, D_GROUP), 0.9, 0.999) ** (1.0 / RG_C)
    rg_lam = jnp.log(a_c) - jnp.log1p(-a_c)
    return {
        'x_prompt': nrm((BATCH, SEQ, D_MODEL)),
        'x_sample': nrm((DEC_BATCH, DEC_SEQ, D_MODEL)),
        'state_gdn_conv': nrm((DEPTH, DEC_BATCH, CONV_W - 1, 3 * D_GROUP)),
        'state_gdn': nrm((DEPTH, DEC_BATCH, H_A, DK_A, DV_A), 0.1),
        'state_s5': nrm((DEPTH, DEC_BATCH, G_B, P_B, 2), 0.5),
        'state_rglru_conv': nrm((DEPTH, DEC_BATCH, CONV_W - 1, D_GROUP)),
        'state_rglru': nrm((DEPTH, DEC_BATCH, D_GROUP), 0.5),
        'cache_band_k': nrm((DEPTH, DEC_BATCH, band_rows, H_D, HD_D)),
        'cache_band_v': nrm((DEPTH, DEC_BATCH, band_rows, H_D, HD_D)),
        'cache_mem_k': nrm((DEPTH, DEC_BATCH, N_MEM, H_X, HD_X)),
        'cache_mem_v': nrm((DEPTH, DEC_BATCH, N_MEM, H_X, HD_X)),
        'mem_prompt': nrm((BATCH, N_MEM, D_MODEL)),
        'w_in': nrm((DEPTH, D_MODEL, N_IN), D_MODEL ** -0.5),
        'w_out': nrm((DEPTH, D_MIX, D_MODEL), D_MIX ** -0.5 * BETA_INIT),
        'ln_g': 1.0 + nrm((DEPTH, 3, D_MODEL), 0.02),
        'ln_b': nrm((DEPTH, 3, D_MODEL), 0.02),
        'gdn_conv_w': nrm((DEPTH, CONV_W, 3 * D_GROUP), CONV_W ** -0.5),
        'gdn_conv_b': nrm((DEPTH, 3 * D_GROUP), 0.02),
        'gdn_a_log': jnp.log(unif((DEPTH, H_A), 1.0, 16.0)),
        'gdn_dt_bias': gdn_dt_bias,
        'gdn_norm_g': 1.0 + nrm((DEPTH, DV_A), 0.02),
        's5_lam_re': -0.5 + nrm((DEPTH, G_B, P_B), 0.01),
        's5_lam_im': math.pi * jnp.arange(P_B, dtype=f32) + nrm((DEPTH, G_B, P_B), 0.01),
        's5_log_dt': unif((DEPTH, G_B), math.log(1e-3), math.log(1e-1)),
        's5_b_re': nrm((DEPTH, G_B, P_B, S5_CH), (2.0 * S5_CH) ** -0.5),
        's5_b_im': nrm((DEPTH, G_B, P_B, S5_CH), (2.0 * S5_CH) ** -0.5),
        's5_c_re': nrm((DEPTH, G_B, S5_CH, P_B), (2.0 * P_B) ** -0.5),
        's5_c_im': nrm((DEPTH, G_B, S5_CH, P_B), (2.0 * P_B) ** -0.5),
        's5_d': nrm((DEPTH, G_B, S5_CH)),
        's5_w_glu': nrm((DEPTH, D_GROUP, D_GROUP), D_GROUP ** -0.5),
        's5_b_glu': nrm((DEPTH, D_GROUP), 0.02),
        'rg_conv_w': nrm((DEPTH, CONV_W, D_GROUP), CONV_W ** -0.5),
        'rg_conv_b': nrm((DEPTH, D_GROUP), 0.02),
        'rg_w_r': nrm((DEPTH, H_C, BLK_C, BLK_C), BLK_C ** -0.5),
        'rg_b_r': nrm((DEPTH, D_GROUP), 0.02),
        'rg_w_i': nrm((DEPTH, H_C, BLK_C, BLK_C), BLK_C ** -0.5),
        'rg_b_i': nrm((DEPTH, D_GROUP), 0.02),
        'rg_lam': rg_lam,
        'band_rel_bias': nrm((DEPTH, 2 * REL_CLIP + 1, H_D), 0.1),
        'xa_w_q': nrm((DEPTH, D_MODEL, H_X * HD_X), D_MODEL ** -0.5),
        'xa_w_k': nrm((DEPTH, D_MODEL, H_X * HD_X), D_MODEL ** -0.5),
        'xa_w_v': nrm((DEPTH, D_MODEL, H_X * HD_X), D_MODEL ** -0.5),
        'xa_w_o': nrm((DEPTH, H_X * HD_X, D_MODEL), (H_X * HD_X) ** -0.5 * BETA_INIT),
        'mlp_w1': nrm((DEPTH, D_MODEL, D_FF), D_MODEL ** -0.5),
        'mlp_w2': nrm((DEPTH, D_FF, D_MODEL), D_FF ** -0.5 * BETA_INIT),
    }


def reference(x_prompt, x_sample, state_gdn_conv, state_gdn, state_s5, state_rglru_conv, state_rglru,
              cache_band_k, cache_band_v, cache_mem_k, cache_mem_v, mem_prompt,
              w_in, w_out, ln_g, ln_b, gdn_conv_w, gdn_conv_b, gdn_a_log, gdn_dt_bias, gdn_norm_g,
              s5_lam_re, s5_lam_im, s5_log_dt, s5_b_re, s5_b_im, s5_c_re, s5_c_im, s5_d, s5_w_glu, s5_b_glu,
              rg_conv_w, rg_conv_b, rg_w_r, rg_b_r, rg_w_i, rg_b_i, rg_lam,
              band_rel_bias, xa_w_q, xa_w_k, xa_w_v, xa_w_o, mlp_w1, mlp_w2):
    f32 = jnp.float32
    bp, n_mem = x_prompt.shape[0], mem_prompt.shape[1]
    xp, xs = x_prompt, x_sample
    p_states, s_states = [], []
    for l in range(DEPTH):
        p = {
            'w_in': w_in[l], 'w_out': w_out[l], 'ln_g': ln_g[l], 'ln_b': ln_b[l],
            'gdn_conv_w': gdn_conv_w[l], 'gdn_conv_b': gdn_conv_b[l], 'gdn_a_log': gdn_a_log[l],
            'gdn_dt_bias': gdn_dt_bias[l], 'gdn_norm_g': gdn_norm_g[l],
            's5_lam_re': s5_lam_re[l], 's5_lam_im': s5_lam_im[l], 's5_log_dt': s5_log_dt[l],
            's5_b_re': s5_b_re[l], 's5_b_im': s5_b_im[l], 's5_c_re': s5_c_re[l], 's5_c_im': s5_c_im[l],
            's5_d': s5_d[l], 's5_w_glu': s5_w_glu[l], 's5_b_glu': s5_b_glu[l],
            'rg_conv_w': rg_conv_w[l], 'rg_conv_b': rg_conv_b[l], 'rg_w_r': rg_w_r[l], 'rg_b_r': rg_b_r[l],
            'rg_w_i': rg_w_i[l], 'rg_b_i': rg_b_i[l], 'rg_lam': rg_lam[l],
            'band_rel_bias': band_rel_bias[l],
            'xa_w_q': xa_w_q[l], 'xa_w_o': xa_w_o[l],
            'mlp_w1': mlp_w1[l], 'mlp_w2': mlp_w2[l],
        }
        mk = (mem_prompt @ xa_w_k[l]).reshape(bp, n_mem, H_X, HD_X)
        mv = (mem_prompt @ xa_w_v[l]).reshape(bp, n_mem, H_X, HD_X)
        xp, st_p = trunk_layer(
            xp, mk, mv,
            jnp.zeros((bp, CONV_W - 1, 3 * D_GROUP), xp.dtype), jnp.zeros((bp, H_A, DK_A, DV_A), f32),
            jnp.zeros((bp, G_B, P_B, 2), f32), jnp.zeros((bp, CONV_W - 1, D_GROUP), xp.dtype),
            jnp.zeros((bp, D_GROUP), f32), None, None, p)
        p_states.append(st_p + (mk, mv))
        xs, st_s = trunk_layer(
            xs, cache_mem_k[l], cache_mem_v[l], state_gdn_conv[l], state_gdn[l], state_s5[l],
            state_rglru_conv[l], state_rglru[l], cache_band_k[l], cache_band_v[l], p)
        s_states.append(st_s)

    def stk(states, i):
        return jnp.stack([st[i] for st in states], axis=0)

    return (xp, xs,
            stk(p_states, 0), stk(p_states, 1), stk(p_states, 2), stk(p_states, 3), stk(p_states, 4),
            stk(p_states, 5), stk(p_states, 6), stk(p_states, 7), stk(p_states, 8),
            stk(s_states, 0), stk(s_states, 1), stk(s_states, 2), stk(s_states, 3), stk(s_states, 4),
            stk(s_states, 5), stk(s_states, 6))
```

```python
import functools
import math

import jax
import jax.numpy as jnp
from jax import lax
from jax.experimental import pallas as pl
from jax.experimental.pallas import tpu as pltpu

F32 = jnp.float32
BF16 = jnp.bfloat16
HIGHEST = lax.Precision.HIGHEST

N_MIX = 4
CONV_W = 4
CHUNK = 64
H_A = 4
S5_CH = 16
P_B = 64
H_C = 4
RG_C = 8.0
H_D = 4
N_PREV = 8
REL_CLIP = 128
H_X = 4
LN_EPS = 1e-5
NORM_EPS = 1e-6

LANES = 128
SUBLANES = 8
VMEM_LIMIT_BYTES = 56 * 1024 * 1024


def _params(*semantics):
    return pltpu.CompilerParams(dimension_semantics=semantics, vmem_limit_bytes=VMEM_LIMIT_BYTES)


def _tile(n, pref):
    t = min(n, pref)
    while n % t:
        t -= SUBLANES
    return t


def _mm(a, b):
    return jnp.dot(a.astype(BF16), b.astype(BF16), preferred_element_type=F32)


def _mm_nt(a, b):
    return lax.dot_general(a.astype(BF16), b.astype(BF16), (((1,), (1,)), ((), ())),
                           preferred_element_type=F32)


def _mm_tn(a, b):
    return lax.dot_general(a.astype(BF16), b.astype(BF16), (((0,), (0,)), ((), ())),
                           preferred_element_type=F32)


def _mm_f32(a, b):
    return jnp.dot(a, b, precision=HIGHEST, preferred_element_type=F32)


def _sigmoid(x):
    return 1.0 / (1.0 + jnp.exp(-x))


def _softplus(x):
    return jnp.maximum(x, 0.0) + jnp.log1p(jnp.exp(-jnp.abs(x)))


def _gelu_tanh(x):
    c = math.sqrt(2.0 / math.pi)
    return 0.5 * x * (1.0 + jnp.tanh(c * (x + 0.044715 * (x * x * x))))


def _layer_norm(z, g, b):
    mu = jnp.mean(z, axis=-1, keepdims=True)
    zc = z - mu
    var = jnp.mean(zc * zc, axis=-1, keepdims=True)
    return zc * lax.rsqrt(var + LN_EPS) * g + b


def _matmul_kernel(x_ref, w_ref, o_ref):
    o_ref[...] = _mm(x_ref[...], w_ref[...])


def _matmul(x, w_bf16, tm=512):
    m, k = x.shape
    n = w_bf16.shape[1]
    tm = _tile(m, tm)
    return pl.pallas_call(
        _matmul_kernel,
        grid=(m // tm,),
        in_specs=[pl.BlockSpec((tm, k), lambda i: (i, 0)), pl.BlockSpec((k, n), lambda i: (0, 0))],
        out_specs=pl.BlockSpec((tm, n), lambda i: (i, 0)),
        out_shape=jax.ShapeDtypeStruct((m, n), F32),
        compiler_params=_params("parallel"),
        name="matmul",
    )(x, w_bf16)


def _inproj_kernel(x_ref, w_ref, *o_refs, bounds):
    xb = x_ref[...].astype(BF16)
    for o_ref, (s, e) in zip(o_refs, bounds):
        o_ref[...] = jnp.dot(xb, w_ref[:, s:e], preferred_element_type=F32)


def _inproj(x, w_bf16, widths, tm=512):
    m, k = x.shape
    n = w_bf16.shape[1]
    tm = _tile(m, tm)
    bounds, s = [], 0
    for w in widths:
        bounds.append((s, s + w))
        s += w
    return pl.pallas_call(
        functools.partial(_inproj_kernel, bounds=tuple(bounds)),
        grid=(m // tm,),
        in_specs=[pl.BlockSpec((tm, k), lambda i: (i, 0)), pl.BlockSpec((k, n), lambda i: (0, 0))],
        out_specs=[pl.BlockSpec((tm, w), lambda i: (i, 0)) for w in widths],
        out_shape=[jax.ShapeDtypeStruct((m, w), F32) for w in widths],
        compiler_params=_params("parallel"),
        name="inproj",
    )(x, w_bf16)


def _unit_lower_inverse(a, t):
    r = lax.broadcasted_iota(jnp.int32, (t, t), 0)
    c = lax.broadcasted_iota(jnp.int32, (t, t), 1)
    eye = (r == c).astype(F32)
    n1 = jnp.where((r >> 3) == (c >> 3), -a, 0.0)
    n2 = _mm_f32(n1, n1)
    n4 = _mm_f32(n2, n2)
    x = _mm_f32(_mm_f32(eye + n1, eye + n2), eye + n4)
    lb = 3
    while (1 << lb) < t:
        off = ((r >> (lb + 1)) == (c >> (lb + 1))) & (((r >> lb) & 1) == 1) & (((c >> lb) & 1) == 0)
        x = x - _mm_f32(_mm_f32(x, jnp.where(off, a, 0.0)), x)
        lb += 1
    return x


def _gdn_kernel(qkv_ref, gate_ref, db_ref, cbuf_ref, s0_ref, cw_ref, cb_ref, alog_ref, dtb_ref, ng_ref,
                o_ref, sfin_ref, xp_scr, s_scr, o_scr, *, t, cps, dk):
    i = pl.program_id(1)
    tb = t * cps
    nh = H_A
    dq = nh * dk

    @pl.when(i == 0)
    def _():
        xp_scr[0:SUBLANES, :] = cbuf_ref[0]
        s_scr[...] = s0_ref[0]

    x = qkv_ref[0]
    xp_scr[SUBLANES:SUBLANES + tb, :] = x
    base = SUBLANES - (CONV_W - 1)
    y = xp_scr[base:base + tb, :] * cw_ref[0:1, :]
    for j in range(1, CONV_W):
        y = y + xp_scr[base + j:base + j + tb, :] * cw_ref[j:j + 1, :]
    y = y + cb_ref[...]
    xp_scr[0:SUBLANES, :] = x[tb - SUBLANES:tb, :]
    y = y * _sigmoid(y)

    db = db_ref[0]
    log_a = -jnp.exp(alog_ref[...]) * _softplus(db + dtb_ref[...])
    beta_all = _sigmoid(db)

    r = lax.broadcasted_iota(jnp.int32, (t, t), 0)
    c = lax.broadcasted_iota(jnp.int32, (t, t), 1)
    causal = r >= c
    strict = r > c
    tril = causal.astype(F32)
    triu = (r <= c).astype(F32)
    ones = jnp.ones((t, t), F32)

    for ci in range(cps):
        r0 = ci * t
        la = log_a[r0:r0 + t, :]
        gc = _mm_f32(tril, la)
        eg = jnp.exp(gc)
        for h in range(nh):
            q = y[r0:r0 + t, h * dk:(h + 1) * dk]
            k = y[r0:r0 + t, dq + h * dk:dq + (h + 1) * dk]
            v = y[r0:r0 + t, 2 * dq + h * dk:2 * dq + (h + 1) * dk]
            q = q * lax.rsqrt(jnp.sum(q * q, axis=-1, keepdims=True) + NORM_EPS) * (dk ** -0.5)
            k = k * lax.rsqrt(jnp.sum(k * k, axis=-1, keepdims=True) + NORM_EPS)
            g_col = gc[:, h:h + 1]
            g_row = _mm_f32(ones, la[:, h:h + 1] * triu)
            decay = jnp.exp(jnp.where(causal, g_col - g_row, -jnp.inf))
            beta = beta_all[r0:r0 + t, nh + h:nh + h + 1]
            eg_col = eg[:, h:h + 1]
            g_last = gc[t - 1:t, h:h + 1]
            a = jnp.where(strict, beta * _mm_nt(k, k) * decay, 0.0)
            t_inv = _unit_lower_inverse(a, t)
            rhs = jnp.concatenate([v * beta, k * (beta * eg_col)], axis=-1)
            sol = _mm_f32(t_inv, rhs)
            u = sol[:, :dk]
            w = sol[:, dk:]
            qk = _mm_nt(q, k) * decay
            q_dec = q * eg_col
            k_dec = k * jnp.exp(g_last - g_col)
            s = s_scr[h]
            v_new = u - _mm(w, s)
            o = _mm(q_dec, s) + _mm(qk, v_new)
            s_scr[h] = s * jnp.exp(g_last) + _mm_tn(k_dec, v_new)
            o = o * lax.rsqrt(jnp.mean(o * o, axis=-1, keepdims=True) + NORM_EPS) * ng_ref[...]
            o_scr[r0:r0 + t, h * dk:(h + 1) * dk] = o

    g = gate_ref[0]
    o_ref[0] = (o_scr[...] * (g * _sigmoid(g))).astype(o_ref.dtype)

    @pl.when(i == pl.num_programs(1) - 1)
    def _():
        sfin_ref[0] = s_scr[...]


def _gdn(pa, pdb, cbuf8, s0, cw, cb, alog, dtb, ng, *, t, cps):
    b, l, _ = pa.shape
    nh, dk = s0.shape[1], s0.shape[2]
    dq = nh * dk
    tb = t * cps
    kern = functools.partial(_gdn_kernel, t=t, cps=cps, dk=dk)
    full = lambda shape: pl.BlockSpec(shape, lambda bi, i: (0,) * len(shape))
    return pl.pallas_call(
        kern,
        grid=(b, l // tb),
        in_specs=[
            pl.BlockSpec((1, tb, 3 * dq), lambda bi, i: (bi, i, 0)),
            pl.BlockSpec((1, tb, dq), lambda bi, i: (bi, i, 3)),
            pl.BlockSpec((1, tb, LANES), lambda bi, i: (bi, i, 0)),
            pl.BlockSpec((1, SUBLANES, 3 * dq), lambda bi, i: (bi, 0, 0)),
            pl.BlockSpec((1, nh, dk, dk), lambda bi, i: (bi, 0, 0, 0)),
            full((CONV_W, 3 * dq)), full((1, 3 * dq)), full((1, LANES)), full((1, LANES)), full((1, dk)),
        ],
        out_specs=[
            pl.BlockSpec((1, tb, dq), lambda bi, i: (bi, i, 0)),
            pl.BlockSpec((1, nh, dk, dk), lambda bi, i: (bi, 0, 0, 0)),
        ],
        out_shape=[jax.ShapeDtypeStruct((b, l, dq), BF16), jax.ShapeDtypeStruct((b, nh, dk, dk), F32)],
        scratch_shapes=[pltpu.VMEM((SUBLANES + tb, 3 * dq), F32), pltpu.VMEM((nh, dk, dk), F32),
                        pltpu.VMEM((tb, dq), F32)],
        compiler_params=_params("parallel", "arbitrary"),
        name="gdn",
    )(pa, pa, pdb, cbuf8, s0, cw, cb, alog, dtb, ng)


def _s5_kernel(u_ref, h0_ref, wb_ref, ap_ref, p8_ref, wc_ref, d_ref, wg_ref, bg_ref,
               o_ref, hl_ref, carry_scr, h_scr, *, t, ns):
    i = pl.program_id(1)

    @pl.when(i == 0)
    def _():
        carry_scr[...] = h0_ref[0]

    u = u_ref[0]
    x = _mm(u, wb_ref[...])
    xr, xi = x[:, :ns], x[:, ns:]
    row = lax.broadcasted_iota(jnp.int32, (t, ns), 0) & (SUBLANES - 1)
    for lvl in range(3):
        s = 1 << lvl
        pr, pi = ap_ref[lvl:lvl + 1, :ns], ap_ref[lvl:lvl + 1, ns:]
        keep = row >= s
        sr = jnp.where(keep, pltpu.roll(xr, s, 0), 0.0)
        si = jnp.where(keep, pltpu.roll(xi, s, 0), 0.0)
        xr, xi = xr + (pr * sr - pi * si), xi + (pr * si + pi * sr)
    p8r, p8i = p8_ref[:, :ns], p8_ref[:, ns:]
    cr, ci = carry_scr[:, :ns], carry_scr[:, ns:]
    for j in range(t // SUBLANES):
        sl = slice(j * SUBLANES, (j + 1) * SUBLANES)
        br = xr[sl] + (p8r * cr - p8i * ci)
        bi = xi[sl] + (p8r * ci + p8i * cr)
        h_scr[sl, :ns] = br
        h_scr[sl, ns:] = bi
        cr, ci = br[SUBLANES - 1:SUBLANES], bi[SUBLANES - 1:SUBLANES]
    carry_scr[:, :ns] = cr
    carry_scr[:, ns:] = ci
    hl_ref[0] = carry_scr[...]

    y = _mm(h_scr[...], wc_ref[...]) + d_ref[...] * u
    y = _gelu_tanh(y)
    z = _mm(y, wg_ref[...]) + bg_ref[...]
    o_ref[0] = (y * _sigmoid(z)).astype(o_ref.dtype)


def _s5(pb, h0, wb, ap, p8, wc, d, wg, bg, *, t):
    b, l, dg = pb.shape
    ns2 = wb.shape[1]
    ns = ns2 // 2
    full = lambda shape: pl.BlockSpec(shape, lambda bi, i: (0,) * len(shape))
    return pl.pallas_call(
        functools.partial(_s5_kernel, t=t, ns=ns),
        grid=(b, l // t),
        in_specs=[
            pl.BlockSpec((1, t, dg), lambda bi, i: (bi, i, 0)),
            pl.BlockSpec((1, 1, ns2), lambda bi, i: (bi, 0, 0)),
            full((dg, ns2)), full((3, ns2)), full((SUBLANES, ns2)), full((ns2, dg)), full((1, dg)),
            full((dg, dg)), full((1, dg)),
        ],
        out_specs=[
            pl.BlockSpec((1, t, dg), lambda bi, i: (bi, i, 0)),
            pl.BlockSpec((1, 1, ns2), lambda bi, i: (bi, 0, 0)),
        ],
        out_shape=[jax.ShapeDtypeStruct((b, l, dg), BF16), jax.ShapeDtypeStruct((b, 1, ns2), F32)],
        scratch_shapes=[pltpu.VMEM((1, ns2), F32), pltpu.VMEM((t, ns2), F32)],
        compiler_params=_params("parallel", "arbitrary"),
        name="s5",
    )(pb, h0, wb, ap, p8, wc, d, wg, bg)


def _rglru_kernel(xg_ref, cbuf_ref, h0_ref, cw_ref, cb_ref, wri_ref, bri_ref, lam_ref,
                  o_ref, hl_ref, xp_scr, carry_scr, h_scr, *, t, dg):
    i = pl.program_id(1)

    @pl.when(i == 0)
    def _():
        xp_scr[0:SUBLANES, :] = cbuf_ref[0]
        carry_scr[...] = h0_ref[0]

    x = xg_ref[0][:, :dg]
    gb = xg_ref[0][:, dg:]
    xp_scr[SUBLANES:SUBLANES + t, :] = x
    base = SUBLANES - (CONV_W - 1)
    y = xp_scr[base:base + t, :] * cw_ref[0:1, :]
    for j in range(1, CONV_W):
        y = y + xp_scr[base + j:base + j + t, :] * cw_ref[j:j + 1, :]
    y = y + cb_ref[...]
    xp_scr[0:SUBLANES, :] = x[t - SUBLANES:t, :]

    ri = _mm(y, wri_ref[...]) + bri_ref[...]
    rg = _sigmoid(ri[:, :dg])
    ig = _sigmoid(ri[:, dg:])
    log_a = (-RG_C * rg) * _softplus(-lam_ref[...])
    a = jnp.exp(log_a)
    th = jnp.tanh(log_a)
    xin = jnp.sqrt(-2.0 * th / (1.0 - th)) * (ig * y)

    row = lax.broadcasted_iota(jnp.int32, (t, dg), 0) & (SUBLANES - 1)
    for lvl in range(3):
        s = 1 << lvl
        keep = row >= s
        a_s = jnp.where(keep, pltpu.roll(a, s, 0), 1.0)
        x_s = jnp.where(keep, pltpu.roll(xin, s, 0), 0.0)
        xin = a * x_s + xin
        a = a * a_s
    cr = carry_scr[...]
    for j in range(t // SUBLANES):
        sl = slice(j * SUBLANES, (j + 1) * SUBLANES)
        hb = xin[sl] + a[sl] * cr
        h_scr[sl, :] = hb
        cr = hb[SUBLANES - 1:SUBLANES]
    carry_scr[...] = cr
    hl_ref[0] = cr
    o_ref[0] = (h_scr[...] * _gelu_tanh(gb)).astype(o_ref.dtype)


def _rglru(pc, cbuf8, h0, cw, cb, wri, bri, lam, *, t):
    b, l, dg2 = pc.shape
    dg = dg2 // 2
    full = lambda shape: pl.BlockSpec(shape, lambda bi, i: (0,) * len(shape))
    return pl.pallas_call(
        functools.partial(_rglru_kernel, t=t, dg=dg),
        grid=(b, l // t),
        in_specs=[
            pl.BlockSpec((1, t, dg2), lambda bi, i: (bi, i, 0)),
            pl.BlockSpec((1, SUBLANES, dg), lambda bi, i: (bi, 0, 0)),
            pl.BlockSpec((1, 1, dg), lambda bi, i: (bi, 0, 0)),
            full((CONV_W, dg)), full((1, dg)), full((dg, dg2)), full((1, dg2)), full((1, dg)),
        ],
        out_specs=[
            pl.BlockSpec((1, t, dg), lambda bi, i: (bi, i, 0)),
            pl.BlockSpec((1, 1, dg), lambda bi, i: (bi, 0, 0)),
        ],
        out_shape=[jax.ShapeDtypeStruct((b, l, dg), BF16), jax.ShapeDtypeStruct((b, 1, dg), F32)],
        scratch_shapes=[pltpu.VMEM((SUBLANES + t, dg), F32), pltpu.VMEM((1, dg), F32),
                        pltpu.VMEM((t, dg), F32)],
        compiler_params=_params("parallel", "arbitrary"),
        name="rglru",
    )(pc, cbuf8, h0, cw, cb, wri, bri, lam)


def _band_prompt_kernel(q_ref, kc_ref, vc_ref, kp_ref, vp_ref, bias_ref, o_ref, *, qb, hd):
    i = pl.program_id(1)
    q = q_ref[0]
    k = jnp.concatenate([kp_ref[0], kc_ref[0]], axis=0)
    v = jnp.concatenate([vp_ref[0], vc_ref[0]], axis=0)
    col = lax.broadcasted_iota(jnp.int32, (qb, 2 * qb), 1)
    has_prev = jnp.logical_or(i > 0, col >= qb)
    outs = []
    for h in range(H_D):
        sl = slice(h * hd, (h + 1) * hd)
        s = _mm_nt(q[:, sl], k[:, sl]) * (hd ** -0.5) + bias_ref[h]
        s = jnp.where(has_prev, s, -jnp.inf)
        m = jnp.max(s, axis=-1, keepdims=True)
        p = jnp.exp(s - m)
        den = jnp.sum(p, axis=-1, keepdims=True)
        outs.append(_mm(p / den, v[:, sl]))
    o_ref[0] = jnp.concatenate(outs, axis=-1).astype(o_ref.dtype)


def _band_prompt(pd, bias, *, qb):
    b, l, w3 = pd.shape
    w = w3 // 3
    hd = w // H_D
    prev = lambda bi, i: jnp.maximum(i - 1, 0)
    return pl.pallas_call(
        functools.partial(_band_prompt_kernel, qb=qb, hd=hd),
        grid=(b, l // qb),
        in_specs=[
            pl.BlockSpec((1, qb, w), lambda bi, i: (bi, i, 0)),
            pl.BlockSpec((1, qb, w), lambda bi, i: (bi, i, 1)),
            pl.BlockSpec((1, qb, w), lambda bi, i: (bi, i, 2)),
            pl.BlockSpec((1, qb, w), lambda bi, i: (bi, prev(bi, i), 1)),
            pl.BlockSpec((1, qb, w), lambda bi, i: (bi, prev(bi, i), 2)),
            pl.BlockSpec((H_D, qb, 2 * qb), lambda bi, i: (0, 0, 0)),
        ],
        out_specs=pl.BlockSpec((1, qb, w), lambda bi, i: (bi, i, 0)),
        out_shape=jax.ShapeDtypeStruct((b, l, w), BF16),
        compiler_params=_params("parallel", "arbitrary"),
        name="band_prompt",
    )(pd, pd, pd, pd, pd, bias)


def _band_sample_kernel(qkv_ref, kc_ref, vc_ref, bc_ref, bn_ref, o_ref, *, hd):
    w = H_D * hd
    x = qkv_ref[0]
    q, kn, vn = x[:, :w], x[:, w:2 * w], x[:, 2 * w:]
    kc = kc_ref[0]
    vc = vc_ref[0]
    outs = []
    for h in range(H_D):
        sl = slice(h * hd, (h + 1) * hd)
        sc = _mm_nt(q[:, sl], kc[:, sl]) * (hd ** -0.5) + bc_ref[h]
        sn = _mm_nt(q[:, sl], kn[:, sl]) * (hd ** -0.5) + bn_ref[h]
        m = jnp.maximum(jnp.max(sc, axis=-1, keepdims=True), jnp.max(sn, axis=-1, keepdims=True))
        pc = jnp.exp(sc - m)
        pn = jnp.exp(sn - m)
        den = jnp.sum(pc, axis=-1, keepdims=True) + jnp.sum(pn, axis=-1, keepdims=True)
        outs.append(_mm(pc / den, vc[:, sl]) + _mm(pn / den, vn[:, sl]))
    o_ref[0] = jnp.concatenate(outs, axis=-1).astype(o_ref.dtype)


def _band_sample(pd, k_cache, v_cache, bias_c, bias_n):
    b, l, w3 = pd.shape
    w = w3 // 3
    hd = w // H_D
    rows = k_cache.shape[1]
    return pl.pallas_call(
        functools.partial(_band_sample_kernel, hd=hd),
        grid=(b,),
        in_specs=[
            pl.BlockSpec((1, l, w3), lambda bi: (bi, 0, 0)),
            pl.BlockSpec((1, rows, w), lambda bi: (bi, 0, 0)),
            pl.BlockSpec((1, rows, w), lambda bi: (bi, 0, 0)),
            pl.BlockSpec((H_D, l, rows), lambda bi: (0, 0, 0)),
            pl.BlockSpec((H_D, l, l), lambda bi: (0, 0, 0)),
        ],
        out_specs=pl.BlockSpec((1, l, w), lambda bi: (bi, 0, 0)),
        out_shape=jax.ShapeDtypeStruct((b, l, w), BF16),
        compiler_params=_params("parallel"),
        name="band_sample",
    )(pd, k_cache, v_cache, bias_c, bias_n)


def _outproj_kernel(x_ref, oa_ref, ob_ref, oc_ref, od_ref, w_ref, g_ref, b_ref, o_ref, *, alpha, dg):
    mix = jnp.dot(oa_ref[...], w_ref[0:dg, :], preferred_element_type=F32)
    for n, r in enumerate((ob_ref, oc_ref, od_ref), start=1):
        mix = mix + jnp.dot(r[...], w_ref[n * dg:(n + 1) * dg, :], preferred_element_type=F32)
    o_ref[...] = _layer_norm(alpha * x_ref[...] + mix, g_ref[...], b_ref[...])


def _outproj(x, oa, ob, oc, od, w_bf16, g, b, *, alpha, tm=512):
    m, d = x.shape
    dg = oa.shape[1]
    tm = _tile(m, tm)
    row = lambda width: pl.BlockSpec((tm, width), lambda i: (i, 0))
    full = lambda shape: pl.BlockSpec(shape, lambda i: (0,) * len(shape))
    return pl.pallas_call(
        functools.partial(_outproj_kernel, alpha=alpha, dg=dg),
        grid=(m // tm,),
        in_specs=[row(d), row(dg), row(dg), row(dg), row(dg), full(w_bf16.shape), full((1, d)), full((1, d))],
        out_specs=row(d),
        out_shape=jax.ShapeDtypeStruct((m, d), F32),
        compiler_params=_params("parallel"),
        name="outproj_ln",
    )(x, oa, ob, oc, od, w_bf16, g, b)


def _xattn_kernel(x_ref, mk_ref, mv_ref, wq_ref, wo_ref, g_ref, b_ref, o_ref, *, alpha, hd):
    x = x_ref[0]
    q = _mm(x, wq_ref[...])
    mk = mk_ref[0]
    mv = mv_ref[0]
    outs = []
    for h in range(H_X):
        sl = slice(h * hd, (h + 1) * hd)
        s = _mm_nt(q[:, sl], mk[:, sl]) * (hd ** -0.5)
        m = jnp.max(s, axis=-1, keepdims=True)
        p = jnp.exp(s - m)
        den = jnp.sum(p, axis=-1, keepdims=True)
        outs.append(_mm(p / den, mv[:, sl]))
    att = _mm(jnp.concatenate(outs, axis=-1), wo_ref[...])
    o_ref[0] = _layer_norm(alpha * x + att, g_ref[...], b_ref[...])


def _xattn(x, mk, mv, wq, wo, g, b, *, alpha, tm=512):
    bsz, l, d = x.shape
    n_mem = mk.shape[1]
    hd = d // H_X
    tm = _tile(l, tm)
    full = lambda shape: pl.BlockSpec(shape, lambda bi, i: (0,) * len(shape))
    return pl.pallas_call(
        functools.partial(_xattn_kernel, alpha=alpha, hd=hd),
        grid=(bsz, l // tm),
        in_specs=[
            pl.BlockSpec((1, tm, d), lambda bi, i: (bi, i, 0)),
            pl.BlockSpec((1, n_mem, d), lambda bi, i: (bi, 0, 0)),
            pl.BlockSpec((1, n_mem, d), lambda bi, i: (bi, 0, 0)),
            full(wq.shape), full(wo.shape), full((1, d)), full((1, d)),
        ],
        out_specs=pl.BlockSpec((1, tm, d), lambda bi, i: (bi, i, 0)),
        out_shape=jax.ShapeDtypeStruct((bsz, l, d), F32),
        compiler_params=_params("parallel", "parallel"),
        name="xattn_ln",
    )(x, mk, mv, wq, wo, g, b)


def _mlp_kernel(x_ref, w1_ref, w2_ref, g_ref, b_ref, o_ref, acc_ref, *, alpha):
    f = pl.program_id(1)
    hid = jnp.maximum(_mm(x_ref[...], w1_ref[...]), 0.0)
    part = _mm(hid * hid, w2_ref[...])

    @pl.when(f == 0)
    def _():
        acc_ref[...] = part

    @pl.when(f > 0)
    def _():
        acc_ref[...] += part

    @pl.when(f == pl.num_programs(1) - 1)
    def _():
        o_ref[...] = _layer_norm(alpha * x_ref[...] + acc_ref[...], g_ref[...], b_ref[...])


def _mlp(x, w1, w2, g, b, *, alpha, tm=1024, tf=512):
    m, d = x.shape
    dff = w1.shape[1]
    tm = _tile(m, tm)
    tf = _tile(dff, tf)
    return pl.pallas_call(
        functools.partial(_mlp_kernel, alpha=alpha),
        grid=(m // tm, dff // tf),
        in_specs=[
            pl.BlockSpec((tm, d), lambda i, f: (i, 0)),
            pl.BlockSpec((d, tf), lambda i, f: (0, f)),
            pl.BlockSpec((tf, d), lambda i, f: (f, 0)),
            pl.BlockSpec((1, d), lambda i, f: (0, 0)),
            pl.BlockSpec((1, d), lambda i, f: (0, 0)),
        ],
        out_specs=pl.BlockSpec((tm, d), lambda i, f: (i, 0)),
        out_shape=jax.ShapeDtypeStruct((m, d), F32),
        scratch_shapes=[pltpu.VMEM((tm, d), F32)],
        compiler_params=_params("parallel", "arbitrary"),
        name="mlp_ln",
    )(x, w1, w2, g, b)


def _block_diag(blocks):
    g, r, c = blocks.shape
    eye = jnp.eye(g, dtype=blocks.dtype)
    return (eye[:, None, :, None] * blocks[:, :, None, :]).reshape(g * r, g * c)


def _pad_lanes(v):
    return jnp.pad(v, (0, LANES - v.shape[0]))[None, :]


def _s5_params(lam_re, lam_im, log_dt, b_re, b_im, c_re, c_im):
    dt = jnp.exp(log_dt)[:, None]
    mag = jnp.exp(lam_re * dt)
    ar, ai = mag * jnp.cos(lam_im * dt), mag * jnp.sin(lam_im * dt)
    den = lam_re * lam_re + lam_im * lam_im
    fr = ((ar - 1.0) * lam_re + ai * lam_im) / den
    fi = (ai * lam_re - (ar - 1.0) * lam_im) / den
    bbr = fr[..., None] * b_re - fi[..., None] * b_im
    bbi = fr[..., None] * b_im + fi[..., None] * b_re
    wb = jnp.concatenate([_block_diag(jnp.swapaxes(bbr, 1, 2)), _block_diag(jnp.swapaxes(bbi, 1, 2))], axis=1)
    wc = jnp.concatenate([_block_diag(jnp.swapaxes(c_re, 1, 2)), -_block_diag(jnp.swapaxes(c_im, 1, 2))], axis=0)
    ar, ai = ar.reshape(-1), ai.reshape(-1)
    pows = [(ar, ai)]
    for _ in range(SUBLANES - 1):
        pr, pi = pows[-1]
        pows.append((pr * ar - pi * ai, pr * ai + pi * ar))
    cat = lambda idx: jnp.stack([jnp.concatenate(pows[n]) for n in idx], axis=0)
    return wb.astype(BF16), cat((0, 1, 3)), cat(range(SUBLANES)), wc.astype(BF16)


def _rel_bias_table(table, q_len, k_pos):
    rel = jnp.arange(q_len)[:, None] - k_pos[None, :]
    idx = jnp.clip(rel, -REL_CLIP, REL_CLIP) + REL_CLIP
    return jnp.moveaxis(table[idx].astype(F32), -1, 0)


def _band_prompt_bias(table, qb):
    band = (N_PREV + 1) * CHUNK
    small = _rel_bias_table(table, CHUNK, jnp.arange(band) - N_PREV * CHUNK)
    rows = [jnp.pad(small, ((0, 0), (0, 0), (c * CHUNK, 2 * qb - band - c * CHUNK)), constant_values=-jnp.inf)
            for c in range(qb // CHUNK)]
    return jnp.concatenate(rows, axis=1)


def _trunk_layer(x, mem_k, mem_v, gdn_conv, gdn_s, s5_h, rg_conv, rg_h, band_k, band_v, p, *, alpha):
    b, l, d = x.shape
    dg = d // N_MIX
    m = b * l
    pa, pb, pc, pd, pdb = _inproj(x.reshape(m, d), p["w_in"], (4 * dg, dg, 2 * dg, 3 * dg, LANES))
    pa, pb, pc, pd, pdb = [t.reshape(b, l, -1) for t in (pa, pb, pc, pd, pdb)]

    pad8 = lambda buf: jnp.pad(buf, ((0, 0), (SUBLANES - (CONV_W - 1), 0), (0, 0)))
    chunk = CHUNK if l % CHUNK == 0 else l
    cps = max(1, min(4, l // chunk))
    o_a, gdn_s_new = _gdn(pa, pdb, pad8(gdn_conv), gdn_s, p["gdn_conv_w"], p["gdn_conv_b"], p["gdn_a_log"],
                          p["gdn_dt_bias"], p["gdn_norm_g"], t=chunk, cps=cps)
    gdn_conv_new = pa[:, l - (CONV_W - 1):, :3 * dg]

    ns = p["s5_wb"].shape[1] // 2
    h0 = jnp.concatenate([s5_h[..., 0].reshape(b, 1, ns), s5_h[..., 1].reshape(b, 1, ns)], axis=-1)
    o_b, h_last = _s5(pb, h0, p["s5_wb"], p["s5_ap"], p["s5_p8"], p["s5_wc"], p["s5_d"], p["s5_w_glu"],
                      p["s5_b_glu"], t=_tile(l, 256))
    s5_h_new = jnp.stack([h_last[:, 0, :ns].reshape(s5_h.shape[:-1]), h_last[:, 0, ns:].reshape(s5_h.shape[:-1])],
                         axis=-1)

    o_c, rg_last = _rglru(pc, pad8(rg_conv), rg_h[:, None, :], p["rg_conv_w"], p["rg_conv_b"], p["rg_wri"],
                          p["rg_bri"], p["rg_lam"], t=_tile(l, 256))
    rg_conv_new = pc[:, l - (CONV_W - 1):, :dg]
    rg_h_new = rg_last[:, 0, :]

    hd = dg // H_D
    k_d = pd[:, :, dg:2 * dg].reshape(b, l, H_D, hd)
    v_d = pd[:, :, 2 * dg:].reshape(b, l, H_D, hd)
    if band_k is None:
        qb = N_PREV * CHUNK
        o_d = _band_prompt(pd, p["band_bias_prompt"], qb=qb)
        rows = min(N_PREV * CHUNK, l)
        band_k_new, band_v_new = k_d[:, l - rows:], v_d[:, l - rows:]
    else:
        rows = band_k.shape[1]
        o_d = _band_sample(pd, band_k.reshape(b, rows, dg), band_v.reshape(b, rows, dg),
                           p["band_bias_cache"], p["band_bias_new"])
        band_k_new, band_v_new = k_d, v_d

    x2 = _outproj(x.reshape(m, d), o_a.reshape(m, dg), o_b.reshape(m, dg), o_c.reshape(m, dg),
                  o_d.reshape(m, dg), p["w_out"], p["ln_g"][0:1], p["ln_b"][0:1], alpha=alpha)
    x3 = _xattn(x2.reshape(b, l, d), mem_k, mem_v, p["xa_w_q"], p["xa_w_o"], p["ln_g"][1:2], p["ln_b"][1:2],
                alpha=alpha)
    x4 = _mlp(x3.reshape(m, d), p["mlp_w1"], p["mlp_w2"], p["ln_g"][2:3], p["ln_b"][2:3], alpha=alpha)
    return x4.reshape(b, l, d), (gdn_conv_new, gdn_s_new, s5_h_new, rg_conv_new, rg_h_new, band_k_new, band_v_new)


def kernel(x_prompt, x_sample, state_gdn_conv, state_gdn, state_s5, state_rglru_conv, state_rglru, cache_band_k, cache_band_v, cache_mem_k, cache_mem_v, mem_prompt, w_in, w_out, ln_g, ln_b, gdn_conv_w, gdn_conv_b, gdn_a_log, gdn_dt_bias, gdn_norm_g, s5_lam_re, s5_lam_im, s5_log_dt, s5_b_re, s5_b_im, s5_c_re, s5_c_im, s5_d, s5_w_glu, s5_b_glu, rg_conv_w, rg_conv_b, rg_w_r, rg_b_r, rg_w_i, rg_b_i, rg_lam, band_rel_bias, xa_w_q, xa_w_k, xa_w_v, xa_w_o, mlp_w1, mlp_w2):
    depth = w_in.shape[0]
    bp, lp, d = x_prompt.shape
    bs, ls, _ = x_sample.shape
    n_mem = mem_prompt.shape[1]
    dg = d // N_MIX
    hd_x = d // H_X
    alpha = (2.0 * depth) ** 0.25
    band_rows = cache_band_k.shape[2]

    sizes = (3 * dg, dg, H_A, H_A, dg, dg, dg, 3 * dg)
    offs = [0]
    for s in sizes:
        offs.append(offs[-1] + s)

    xp, xs = x_prompt, x_sample
    p_states, s_states = [], []
    for l in range(depth):
        wi = w_in[l]
        w_db = jnp.pad(wi[:, offs[2]:offs[4]], ((0, 0), (0, LANES - 2 * H_A)))
        w_in_l = jnp.concatenate([wi[:, offs[0]:offs[2]], wi[:, offs[4]:offs[5]], wi[:, offs[5]:offs[7]],
                                  wi[:, offs[7]:offs[8]], w_db], axis=1).astype(BF16)
        s5_wb, s5_ap, s5_p8, s5_wc = _s5_params(s5_lam_re[l], s5_lam_im[l], s5_log_dt[l], s5_b_re[l], s5_b_im[l],
                                                s5_c_re[l], s5_c_im[l])
        p = {
            "w_in": w_in_l, "w_out": w_out[l].astype(BF16), "ln_g": ln_g[l], "ln_b": ln_b[l],
            "gdn_conv_w": gdn_conv_w[l], "gdn_conv_b": gdn_conv_b[l][None, :],
            "gdn_a_log": _pad_lanes(gdn_a_log[l]), "gdn_dt_bias": _pad_lanes(gdn_dt_bias[l]),
            "gdn_norm_g": gdn_norm_g[l][None, :],
            "s5_wb": s5_wb, "s5_ap": s5_ap, "s5_p8": s5_p8, "s5_wc": s5_wc,
            "s5_d": s5_d[l].reshape(1, dg), "s5_w_glu": s5_w_glu[l].astype(BF16), "s5_b_glu": s5_b_glu[l][None, :],
            "rg_conv_w": rg_conv_w[l], "rg_conv_b": rg_conv_b[l][None, :],
            "rg_wri": jnp.concatenate([_block_diag(rg_w_r[l]), _block_diag(rg_w_i[l])], axis=1).astype(BF16),
            "rg_bri": jnp.concatenate([rg_b_r[l], rg_b_i[l]])[None, :], "rg_lam": rg_lam[l][None, :],
            "band_bias_prompt": _band_prompt_bias(band_rel_bias[l], N_PREV * CHUNK),
            "band_bias_cache": _rel_bias_table(band_rel_bias[l], ls, jnp.arange(band_rows) - band_rows),
            "band_bias_new": _rel_bias_table(band_rel_bias[l], ls, jnp.arange(ls)),
            "xa_w_q": xa_w_q[l].astype(BF16), "xa_w_o": xa_w_o[l].astype(BF16),
            "mlp_w1": mlp_w1[l].astype(BF16), "mlp_w2": mlp_w2[l].astype(BF16),
        }
        mem2 = mem_prompt.reshape(bp * n_mem, d)
        mk = _matmul(mem2, xa_w_k[l].astype(BF16)).reshape(bp, n_mem, d)
        mv = _matmul(mem2, xa_w_v[l].astype(BF16)).reshape(bp, n_mem, d)
        xp, st_p = _trunk_layer(
            xp, mk, mv,
            jnp.zeros((bp, CONV_W - 1, 3 * dg), F32), jnp.zeros((bp, H_A, dg // H_A, dg // H_A), F32),
            jnp.zeros((bp, dg // S5_CH, P_B, 2), F32), jnp.zeros((bp, CONV_W - 1, dg), F32),
            jnp.zeros((bp, dg), F32), None, None, p, alpha=alpha)
        p_states.append(st_p + (mk.reshape(bp, n_mem, H_X, hd_x), mv.reshape(bp, n_mem, H_X, hd_x)))
        xs, st_s = _trunk_layer(
            xs, cache_mem_k[l].reshape(bs, n_mem, d), cache_mem_v[l].reshape(bs, n_mem, d),
            state_gdn_conv[l], state_gdn[l], state_s5[l], state_rglru_conv[l], state_rglru[l],
            cache_band_k[l], cache_band_v[l], p, alpha=alpha)
        s_states.append(st_s)

    def stk(states, i):
        return jnp.stack([st[i] for st in states], axis=0)

    return (xp, xs,
            stk(p_states, 0), stk(p_states, 1), stk(p_states, 2), stk(p_states, 3), stk(p_states, 4),
            stk(p_states, 5), stk(p_states, 6), stk(p_states, 7), stk(p_states, 8),
            stk(s_states, 0), stk(s_states, 1), stk(s_states, 2), stk(s_states, 3), stk(s_states, 4),
            stk(s_states, 5), stk(s_states, 6))
```

```python
import functools
import math

import jax
import jax.numpy as jnp
from jax import lax
from jax.experimental import pallas as pl
from jax.experimental.pallas import tpu as pltpu

F32 = jnp.float32
BF16 = jnp.bfloat16
HIGHEST = lax.Precision.HIGHEST

N_MIX = 4
CONV_W = 4
CHUNK = 64
H_A = 4
S5_CH = 16
P_B = 64
H_C = 4
RG_C = 8.0
H_D = 4
N_PREV = 8
REL_CLIP = 128
H_X = 4
LN_EPS = 1e-5
NORM_EPS = 1e-6

LANES = 128
SUBLANES = 8
VMEM_LIMIT_BYTES = 56 * 1024 * 1024
GDN_BLOCK = 128
BAND_PIECE = 256


def _params(*semantics):
    return pltpu.CompilerParams(dimension_semantics=semantics, vmem_limit_bytes=VMEM_LIMIT_BYTES)


def _tile(n, pref):
    t = min(n, pref)
    while n % t:
        t -= SUBLANES
    return t


def _mm(a, b):
    return jnp.dot(a.astype(BF16), b.astype(BF16), preferred_element_type=F32)


def _mm_nt(a, b):
    return lax.dot_general(a.astype(BF16), b.astype(BF16), (((1,), (1,)), ((), ())),
                           preferred_element_type=F32)


def _mm_tn(a, b):
    return lax.dot_general(a.astype(BF16), b.astype(BF16), (((0,), (0,)), ((), ())),
                           preferred_element_type=F32)


def _mm_f32(a, b):
    return jnp.dot(a, b, precision=HIGHEST, preferred_element_type=F32)


def _sigmoid(x):
    return 1.0 / (1.0 + jnp.exp(-x))


def _softplus(x):
    return jnp.maximum(x, 0.0) + jnp.log1p(jnp.exp(-jnp.abs(x)))


def _gelu_tanh(x):
    c = math.sqrt(2.0 / math.pi)
    return 0.5 * x * (1.0 + jnp.tanh(c * (x + 0.044715 * (x * x * x))))


def _layer_norm(z, g, b):
    mu = jnp.mean(z, axis=-1, keepdims=True)
    zc = z - mu
    var = jnp.mean(zc * zc, axis=-1, keepdims=True)
    return zc * lax.rsqrt(var + LN_EPS) * g + b


def _matmul_kernel(x_ref, w_ref, o_ref):
    o_ref[...] = _mm(x_ref[...], w_ref[...])


def _matmul(x, w_bf16, tm=512):
    m, k = x.shape
    n = w_bf16.shape[1]
    tm = _tile(m, tm)
    return pl.pallas_call(
        _matmul_kernel,
        grid=(m // tm,),
        in_specs=[pl.BlockSpec((tm, k), lambda i: (i, 0)), pl.BlockSpec((k, n), lambda i: (0, 0))],
        out_specs=pl.BlockSpec((tm, n), lambda i: (i, 0)),
        out_shape=jax.ShapeDtypeStruct((m, n), F32),
        compiler_params=_params("parallel"),
        name="matmul",
    )(x, w_bf16)


def _inproj_kernel(x_ref, w_ref, *o_refs, bounds):
    xb = x_ref[...].astype(BF16)
    for o_ref, (s, e) in zip(o_refs, bounds):
        o_ref[...] = jnp.dot(xb, w_ref[:, s:e], preferred_element_type=F32)


def _inproj(x, w_bf16, widths, tm=512):
    m, k = x.shape
    n = w_bf16.shape[1]
    tm = _tile(m, tm)
    bounds, s = [], 0
    for w in widths:
        bounds.append((s, s + w))
        s += w
    return pl.pallas_call(
        functools.partial(_inproj_kernel, bounds=tuple(bounds)),
        grid=(m // tm,),
        in_specs=[pl.BlockSpec((tm, k), lambda i: (i, 0)), pl.BlockSpec((k, n), lambda i: (0, 0))],
        out_specs=[pl.BlockSpec((tm, w), lambda i: (i, 0)) for w in widths],
        out_shape=[jax.ShapeDtypeStruct((m, w), F32) for w in widths],
        compiler_params=_params("parallel"),
        name="inproj",
    )(x, w_bf16)


def _inverse_masks(r, c, t):
    neg_diag8 = jnp.where((r >> 3) == (c >> 3), -1.0, 0.0)
    offs, lb = [], 3
    while (1 << lb) < t:
        off = ((r >> (lb + 1)) == (c >> (lb + 1))) & (((r >> lb) & 1) == 1) & (((c >> lb) & 1) == 0)
        offs.append(jnp.where(off, 1.0, 0.0))
        lb += 1
    return neg_diag8, offs


def _unit_lower_inverse_offdiag(a_list, masks):
    neg_diag8, offs = masks
    n1 = [a * neg_diag8 for a in a_list]
    n2 = [_mm(x, x) for x in n1]
    n3 = [_mm(x, x2) for x, x2 in zip(n1, n2)]
    n4 = [_mm(x2, x2) for x2 in n2]
    p = [x + x2 + x3 for x, x2, x3 in zip(n1, n2, n3)]
    pn4 = [_mm(pp, x4) for pp, x4 in zip(p, n4)]
    y = [pp + x4 + px for pp, x4, px in zip(p, n4, pn4)]
    for off in offs:
        m = [a * off for a in a_list]
        z = [mm + _mm(yy, mm) for yy, mm in zip(y, m)]
        zy = [_mm(zz, yy) for zz, yy in zip(z, y)]
        y = [yy - (zz + zzy) for yy, zz, zzy in zip(y, z, zy)]
    return y


def _split3(x):
    h1 = x.astype(BF16)
    r1 = x - h1.astype(F32)
    h2 = r1.astype(BF16)
    h3 = (r1 - h2.astype(F32)).astype(BF16)
    return h1, h2, h3


def _gdn_kernel(qkv_ref, gate_ref, db_ref, cbuf_ref, s0_ref, cw_ref, cb_ref, alog_ref, dtb_ref, ng_ref,
                o_ref, sfin_ref, xp_scr, s_scr, o_scr, *, t, cps, dk):
    i = pl.program_id(1)
    tb = t * cps
    nh = H_A
    dq = nh * dk

    @pl.when(i == 0)
    def _():
        xp_scr[0:SUBLANES, :] = cbuf_ref[0]
        s_scr[...] = s0_ref[0]

    x = qkv_ref[0]
    xp_scr[SUBLANES:SUBLANES + tb, :] = x
    base = SUBLANES - (CONV_W - 1)
    y = xp_scr[base:base + tb, :] * cw_ref[0:1, :]
    for j in range(1, CONV_W):
        y = y + xp_scr[base + j:base + j + tb, :] * cw_ref[j:j + 1, :]
    y = y + cb_ref[...]
    xp_scr[0:SUBLANES, :] = x[tb - SUBLANES:tb, :]
    y = y * _sigmoid(y)

    db = db_ref[0]
    log_a = -jnp.exp(alog_ref[...]) * _softplus(db + dtb_ref[...])
    beta_all = _sigmoid(db)

    bs = min(tb, GDN_BLOCK)
    lt = t.bit_length() - 1
    r = lax.broadcasted_iota(jnp.int32, (bs, bs), 0)
    c = lax.broadcasted_iota(jnp.int32, (bs, bs), 1)
    same = (r >> lt) == (c >> lt)
    causal_neg = jnp.where(same & (r >= c), 0.0, -jnp.inf)
    strict_f = jnp.where(same & (r > c), 1.0, 0.0)
    tril = jnp.where(same & (r >= c), 1.0, 0.0).astype(BF16)
    striu = jnp.where(same & (r < c), 1.0, 0.0).astype(BF16)
    inv_masks = _inverse_masks(r, c, t)

    blocks = list(range(0, tb, bs))
    pairs = [(bi, h) for bi in range(len(blocks)) for h in range(nh)]
    la3 = [_split3(log_a[b0:b0 + bs, :]) for b0 in blocks]
    gc = [sum(jnp.dot(tril, part, preferred_element_type=F32) for part in parts) for parts in la3]
    rv = [sum(jnp.dot(striu, part, preferred_element_type=F32) for part in parts) for parts in la3]
    eg = [jnp.exp(g) for g in gc]
    erv = [jnp.exp(g) for g in rv]
    e_tot = [jnp.exp(g + g2) for g, g2 in zip(gc, rv)]
    gc_rows = [g.T for g in gc]

    def head_cols(z, bi, h, off):
        b0 = blocks[bi]
        return z[b0:b0 + bs, off + h * dk:off + (h + 1) * dk]

    q = [head_cols(y, bi, h, 0) for bi, h in pairs]
    k = [head_cols(y, bi, h, dq) for bi, h in pairs]
    v = [head_cols(y, bi, h, 2 * dq) for bi, h in pairs]
    q = [z * lax.rsqrt(jnp.sum(z * z, axis=-1, keepdims=True) + NORM_EPS) * (dk ** -0.5) for z in q]
    k = [z * lax.rsqrt(jnp.sum(z * z, axis=-1, keepdims=True) + NORM_EPS) for z in k]
    decay = [jnp.exp(gc[bi][:, h:h + 1] - gc_rows[bi][h:h + 1, :] + causal_neg) for bi, h in pairs]
    beta = [beta_all[blocks[bi]:blocks[bi] + bs, nh + h:nh + h + 1] for bi, h in pairs]
    eg_col = [eg[bi][:, h:h + 1] for bi, h in pairs]
    kk = [_mm_nt(z, z) for z in k]
    qk = [_mm_nt(zq, zk) for zq, zk in zip(q, k)]
    a = [(b * z * d) * strict_f for b, z, d in zip(beta, kk, decay)]
    qk = [z * d for z, d in zip(qk, decay)]
    y_inv = _unit_lower_inverse_offdiag(a, inv_masks)
    rhs = [jnp.concatenate([zv * b, zk * (b * e)], axis=-1) for zv, zk, b, e in zip(v, k, beta, eg_col)]
    sol = [z + _mm(yi, z) for yi, z in zip(y_inv, rhs)]
    q_dec = [z * e for z, e in zip(q, eg_col)]
    k_dec = [z * erv[bi][:, h:h + 1] for z, (bi, h) in zip(k, pairs)]

    states = [s_scr[h] for h in range(nh)]
    v_news = [[] for _ in pairs]
    o_inter = [[] for _ in pairs]
    for bi in range(len(blocks)):
        for r0 in range(0, bs, t):
            idx = [bi * nh + h for h in range(nh)]
            ws = [_mm(jnp.concatenate([sol[n][r0:r0 + t, dk:], q_dec[n][r0:r0 + t]], axis=0), states[h])
                  for h, n in enumerate(idx)]
            v_new = [sol[n][r0:r0 + t, :dk] - z[:t] for n, z in zip(idx, ws)]
            kv = [_mm_tn(k_dec[n][r0:r0 + t], z) for n, z in zip(idx, v_new)]
            states = [s * e_tot[bi][r0:r0 + 1, h:h + 1] + z for h, (s, z) in enumerate(zip(states, kv))]
            for n, z, z2 in zip(idx, v_new, ws):
                v_news[n].append(z)
                o_inter[n].append(z2[t:])
    o = [jnp.concatenate(oi, axis=0) + _mm(z, jnp.concatenate(vn, axis=0))
         for oi, z, vn in zip(o_inter, qk, v_news)]
    o = [z * lax.rsqrt(jnp.mean(z * z, axis=-1, keepdims=True) + NORM_EPS) * ng_ref[...] for z in o]
    for (bi, h), z in zip(pairs, o):
        o_scr[blocks[bi]:blocks[bi] + bs, h * dk:(h + 1) * dk] = z
    for h in range(nh):
        s_scr[h] = states[h]

    g = gate_ref[0]
    o_ref[0] = (o_scr[...] * (g * _sigmoid(g))).astype(o_ref.dtype)

    @pl.when(i == pl.num_programs(1) - 1)
    def _():
        sfin_ref[0] = s_scr[...]


def _gdn(pa, pdb, cbuf8, s0, cw, cb, alog, dtb, ng, *, t, cps):
    b, l, _ = pa.shape
    nh, dk = s0.shape[1], s0.shape[2]
    dq = nh * dk
    tb = t * cps
    kern = functools.partial(_gdn_kernel, t=t, cps=cps, dk=dk)
    full = lambda shape: pl.BlockSpec(shape, lambda bi, i: (0,) * len(shape))
    return pl.pallas_call(
        kern,
        grid=(b, l // tb),
        in_specs=[
            pl.BlockSpec((1, tb, 3 * dq), lambda bi, i: (bi, i, 0)),
            pl.BlockSpec((1, tb, dq), lambda bi, i: (bi, i, 3)),
            pl.BlockSpec((1, tb, LANES), lambda bi, i: (bi, i, 0)),
            pl.BlockSpec((1, SUBLANES, 3 * dq), lambda bi, i: (bi, 0, 0)),
            pl.BlockSpec((1, nh, dk, dk), lambda bi, i: (bi, 0, 0, 0)),
            full((CONV_W, 3 * dq)), full((1, 3 * dq)), full((1, LANES)), full((1, LANES)), full((1, dk)),
        ],
        out_specs=[
            pl.BlockSpec((1, tb, dq), lambda bi, i: (bi, i, 0)),
            pl.BlockSpec((1, nh, dk, dk), lambda bi, i: (bi, 0, 0, 0)),
        ],
        out_shape=[jax.ShapeDtypeStruct((b, l, dq), BF16), jax.ShapeDtypeStruct((b, nh, dk, dk), F32)],
        scratch_shapes=[pltpu.VMEM((SUBLANES + tb, 3 * dq), F32), pltpu.VMEM((nh, dk, dk), F32),
                        pltpu.VMEM((tb, dq), F32)],
        compiler_params=_params("parallel", "arbitrary"),
        name="gdn",
    )(pa, pa, pdb, cbuf8, s0, cw, cb, alog, dtb, ng)


def _s5_kernel(u_ref, h0_ref, wb_ref, ap_ref, p8_ref, wc_ref, d_ref, wg_ref, bg_ref,
               o_ref, hl_ref, carry_scr, h_scr, *, t, ns):
    i = pl.program_id(1)

    @pl.when(i == 0)
    def _():
        carry_scr[...] = h0_ref[0]

    u = u_ref[0]
    x = _mm(u, wb_ref[...])
    ng = t // SUBLANES
    xr = x[:, :ns].reshape(ng, SUBLANES, ns)
    xi = x[:, ns:].reshape(ng, SUBLANES, ns)
    for lvl in range(3):
        s = 1 << lvl
        pr, pi = ap_ref[lvl, :, :ns], ap_ref[lvl, :, ns:]
        sr = pltpu.roll(xr, s, 1)
        si = pltpu.roll(xi, s, 1)
        xr, xi = xr + (pr * sr - pi * si), xi + (pr * si + pi * sr)
    p8r, p8i = p8_ref[:, :ns], p8_ref[:, ns:]
    cr, ci = carry_scr[:, :ns], carry_scr[:, ns:]
    for j in range(ng):
        sl = slice(j * SUBLANES, (j + 1) * SUBLANES)
        br = xr[j] + (p8r * cr - p8i * ci)
        bi = xi[j] + (p8r * ci + p8i * cr)
        h_scr[sl, :ns] = br
        h_scr[sl, ns:] = bi
        cr, ci = br[SUBLANES - 1:SUBLANES], bi[SUBLANES - 1:SUBLANES]
    carry_scr[:, :ns] = cr
    carry_scr[:, ns:] = ci
    hl_ref[0] = carry_scr[...]

    y = _mm(h_scr[...], wc_ref[...]) + d_ref[...] * u
    y = _gelu_tanh(y)
    z = _mm(y, wg_ref[...]) + bg_ref[...]
    o_ref[0] = (y * _sigmoid(z)).astype(o_ref.dtype)


def _s5(pb, h0, wb, ap, p8, wc, d, wg, bg, *, t):
    b, l, dg = pb.shape
    ns2 = wb.shape[1]
    ns = ns2 // 2
    full = lambda shape: pl.BlockSpec(shape, lambda bi, i: (0,) * len(shape))
    return pl.pallas_call(
        functools.partial(_s5_kernel, t=t, ns=ns),
        grid=(b, l // t),
        in_specs=[
            pl.BlockSpec((1, t, dg), lambda bi, i: (bi, i, 0)),
            pl.BlockSpec((1, 1, ns2), lambda bi, i: (bi, 0, 0)),
            full((dg, ns2)), full((3, SUBLANES, ns2)), full((SUBLANES, ns2)), full((ns2, dg)), full((1, dg)),
            full((dg, dg)), full((1, dg)),
        ],
        out_specs=[
            pl.BlockSpec((1, t, dg), lambda bi, i: (bi, i, 0)),
            pl.BlockSpec((1, 1, ns2), lambda bi, i: (bi, 0, 0)),
        ],
        out_shape=[jax.ShapeDtypeStruct((b, l, dg), BF16), jax.ShapeDtypeStruct((b, 1, ns2), F32)],
        scratch_shapes=[pltpu.VMEM((1, ns2), F32), pltpu.VMEM((t, ns2), F32)],
        compiler_params=_params("parallel", "arbitrary"),
        name="s5",
    )(pb, h0, wb, ap, p8, wc, d, wg, bg)


def _rglru_kernel(xg_ref, cbuf_ref, h0_ref, cw_ref, cb_ref, wri_ref, bri_ref, lam_ref,
                  o_ref, hl_ref, xp_scr, carry_scr, h_scr, *, t, dg):
    i = pl.program_id(1)

    @pl.when(i == 0)
    def _():
        xp_scr[0:SUBLANES, :] = cbuf_ref[0]
        carry_scr[...] = h0_ref[0]

    x = xg_ref[0][:, :dg]
    gb = xg_ref[0][:, dg:]
    xp_scr[SUBLANES:SUBLANES + t, :] = x
    base = SUBLANES - (CONV_W - 1)
    y = xp_scr[base:base + t, :] * cw_ref[0:1, :]
    for j in range(1, CONV_W):
        y = y + xp_scr[base + j:base + j + t, :] * cw_ref[j:j + 1, :]
    y = y + cb_ref[...]
    xp_scr[0:SUBLANES, :] = x[t - SUBLANES:t, :]

    ri = _mm(y, wri_ref[...]) + bri_ref[...]
    rg = _sigmoid(ri[:, :dg])
    ig = _sigmoid(ri[:, dg:])
    log_a = (-RG_C * rg) * _softplus(-lam_ref[...])
    a = jnp.exp(log_a)
    th = jnp.tanh(log_a)
    xin = jnp.sqrt(-2.0 * th / (1.0 - th)) * (ig * y)

    row = lax.broadcasted_iota(jnp.int32, (t, dg), 0) & (SUBLANES - 1)
    for lvl in range(3):
        s = 1 << lvl
        keep = row >= s
        a_s = jnp.where(keep, pltpu.roll(a, s, 0), 1.0)
        x_s = jnp.where(keep, pltpu.roll(xin, s, 0), 0.0)
        xin = a * x_s + xin
        a = a * a_s
    cr = carry_scr[...]
    for j in range(t // SUBLANES):
        sl = slice(j * SUBLANES, (j + 1) * SUBLANES)
        hb = xin[sl] + a[sl] * cr
        h_scr[sl, :] = hb
        cr = hb[SUBLANES - 1:SUBLANES]
    carry_scr[...] = cr
    hl_ref[0] = cr
    o_ref[0] = (h_scr[...] * _gelu_tanh(gb)).astype(o_ref.dtype)


def _rglru(pc, cbuf8, h0, cw, cb, wri, bri, lam, *, t):
    b, l, dg2 = pc.shape
    dg = dg2 // 2
    full = lambda shape: pl.BlockSpec(shape, lambda bi, i: (0,) * len(shape))
    return pl.pallas_call(
        functools.partial(_rglru_kernel, t=t, dg=dg),
        grid=(b, l // t),
        in_specs=[
            pl.BlockSpec((1, t, dg2), lambda bi, i: (bi, i, 0)),
            pl.BlockSpec((1, SUBLANES, dg), lambda bi, i: (bi, 0, 0)),
            pl.BlockSpec((1, 1, dg), lambda bi, i: (bi, 0, 0)),
            full((CONV_W, dg)), full((1, dg)), full((dg, dg2)), full((1, dg2)), full((1, dg)),
        ],
        out_specs=[
            pl.BlockSpec((1, t, dg), lambda bi, i: (bi, i, 0)),
            pl.BlockSpec((1, 1, dg), lambda bi, i: (bi, 0, 0)),
        ],
        out_shape=[jax.ShapeDtypeStruct((b, l, dg), BF16), jax.ShapeDtypeStruct((b, 1, dg), F32)],
        scratch_shapes=[pltpu.VMEM((SUBLANES + t, dg), F32), pltpu.VMEM((1, dg), F32),
                        pltpu.VMEM((t, dg), F32)],
        compiler_params=_params("parallel", "arbitrary"),
        name="rglru",
    )(pc, cbuf8, h0, cw, cb, wri, bri, lam)


def _band_prompt_kernel(q_ref, kc_ref, vc_ref, kp_ref, vp_ref, bias_ref, o_ref, *, qb, qp, hd):
    span = N_PREV * CHUNK
    q = q_ref[0] * (hd ** -0.5)
    k = jnp.concatenate([kp_ref[0], kc_ref[0]], axis=0)
    v = jnp.concatenate([vp_ref[0], vc_ref[0]], axis=0)
    units = [(h, p) for h in range(H_D) for p in range(qb // qp)]
    col = lambda h: slice(h * hd, (h + 1) * hd)
    win = lambda p: slice(qb - span + p * qp, qb + (p + 1) * qp)
    s = [_mm_nt(q[p * qp:(p + 1) * qp, col(h)], k[win(p), col(h)]) + bias_ref[0, p, h] for h, p in units]
    e = [jnp.exp(z - jnp.max(z, axis=-1, keepdims=True)) for z in s]
    den = [jnp.sum(z, axis=-1, keepdims=True) for z in e]
    o = [_mm(z, v[win(p), col(h)]) / d for z, d, (h, p) in zip(e, den, units)]
    npc = qb // qp
    o_ref[0] = jnp.concatenate([jnp.concatenate(o[h * npc:(h + 1) * npc], axis=0) for h in range(H_D)],
                               axis=-1).astype(o_ref.dtype)


def _band_prompt(pd, bias, *, qb, qp):
    b, l, w3 = pd.shape
    w = w3 // 3
    hd = w // H_D
    prev = lambda bi, i: jnp.maximum(i - 1, 0)
    return pl.pallas_call(
        functools.partial(_band_prompt_kernel, qb=qb, qp=qp, hd=hd),
        grid=(b, l // qb),
        in_specs=[
            pl.BlockSpec((1, qb, w), lambda bi, i: (bi, i, 0)),
            pl.BlockSpec((1, qb, w), lambda bi, i: (bi, i, 1)),
            pl.BlockSpec((1, qb, w), lambda bi, i: (bi, i, 2)),
            pl.BlockSpec((1, qb, w), lambda bi, i: (bi, prev(bi, i), 1)),
            pl.BlockSpec((1, qb, w), lambda bi, i: (bi, prev(bi, i), 2)),
            pl.BlockSpec((1,) + bias.shape[1:], lambda bi, i: (jnp.minimum(i, 1), 0, 0, 0, 0)),
        ],
        out_specs=pl.BlockSpec((1, qb, w), lambda bi, i: (bi, i, 0)),
        out_shape=jax.ShapeDtypeStruct((b, l, w), BF16),
        compiler_params=_params("parallel", "arbitrary"),
        name="band_prompt",
    )(pd, pd, pd, pd, pd, bias)


def _band_sample_kernel(qkv_ref, kc_ref, vc_ref, bc_ref, bn_ref, o_ref, *, hd):
    w = H_D * hd
    x = qkv_ref[0]
    q, kn, vn = x[:, :w], x[:, w:2 * w], x[:, 2 * w:]
    kc = kc_ref[0]
    vc = vc_ref[0]
    outs = []
    for h in range(H_D):
        sl = slice(h * hd, (h + 1) * hd)
        sc = _mm_nt(q[:, sl], kc[:, sl]) * (hd ** -0.5) + bc_ref[h]
        sn = _mm_nt(q[:, sl], kn[:, sl]) * (hd ** -0.5) + bn_ref[h]
        m = jnp.maximum(jnp.max(sc, axis=-1, keepdims=True), jnp.max(sn, axis=-1, keepdims=True))
        pc = jnp.exp(sc - m)
        pn = jnp.exp(sn - m)
        den = jnp.sum(pc, axis=-1, keepdims=True) + jnp.sum(pn, axis=-1, keepdims=True)
        outs.append(_mm(pc / den, vc[:, sl]) + _mm(pn / den, vn[:, sl]))
    o_ref[0] = jnp.concatenate(outs, axis=-1).astype(o_ref.dtype)


def _band_sample(pd, k_cache, v_cache, bias_c, bias_n):
    b, l, w3 = pd.shape
    w = w3 // 3
    hd = w // H_D
    rows = k_cache.shape[1]
    return pl.pallas_call(
        functools.partial(_band_sample_kernel, hd=hd),
        grid=(b,),
        in_specs=[
            pl.BlockSpec((1, l, w3), lambda bi: (bi, 0, 0)),
            pl.BlockSpec((1, rows, w), lambda bi: (bi, 0, 0)),
            pl.BlockSpec((1, rows, w), lambda bi: (bi, 0, 0)),
            pl.BlockSpec((H_D, l, rows), lambda bi: (0, 0, 0)),
            pl.BlockSpec((H_D, l, l), lambda bi: (0, 0, 0)),
        ],
        out_specs=pl.BlockSpec((1, l, w), lambda bi: (bi, 0, 0)),
        out_shape=jax.ShapeDtypeStruct((b, l, w), BF16),
        compiler_params=_params("parallel"),
        name="band_sample",
    )(pd, k_cache, v_cache, bias_c, bias_n)


def _outproj_kernel(x_ref, oa_ref, ob_ref, oc_ref, od_ref, w_ref, g_ref, b_ref, o_ref, *, alpha, dg):
    mix = jnp.dot(oa_ref[...], w_ref[0:dg, :], preferred_element_type=F32)
    for n, r in enumerate((ob_ref, oc_ref, od_ref), start=1):
        mix = mix + jnp.dot(r[...], w_ref[n * dg:(n + 1) * dg, :], preferred_element_type=F32)
    o_ref[...] = _layer_norm(alpha * x_ref[...] + mix, g_ref[...], b_ref[...])


def _outproj(x, oa, ob, oc, od, w_bf16, g, b, *, alpha, tm=512):
    m, d = x.shape
    dg = oa.shape[1]
    tm = _tile(m, tm)
    row = lambda width: pl.BlockSpec((tm, width), lambda i: (i, 0))
    full = lambda shape: pl.BlockSpec(shape, lambda i: (0,) * len(shape))
    return pl.pallas_call(
        functools.partial(_outproj_kernel, alpha=alpha, dg=dg),
        grid=(m // tm,),
        in_specs=[row(d), row(dg), row(dg), row(dg), row(dg), full(w_bf16.shape), full((1, d)), full((1, d))],
        out_specs=row(d),
        out_shape=jax.ShapeDtypeStruct((m, d), F32),
        compiler_params=_params("parallel"),
        name="outproj_ln",
    )(x, oa, ob, oc, od, w_bf16, g, b)


def _xattn_kernel(x_ref, mk_ref, mv_ref, wq_ref, wo_ref, g_ref, b_ref, o_ref, *, alpha, hd):
    x = x_ref[0]
    q = _mm(x, wq_ref[...])
    mk = mk_ref[0]
    mv = mv_ref[0]
    outs = []
    for h in range(H_X):
        sl = slice(h * hd, (h + 1) * hd)
        s = _mm_nt(q[:, sl], mk[:, sl]) * (hd ** -0.5)
        m = jnp.max(s, axis=-1, keepdims=True)
        p = jnp.exp(s - m)
        den = jnp.sum(p, axis=-1, keepdims=True)
        outs.append(_mm(p / den, mv[:, sl]))
    att = _mm(jnp.concatenate(outs, axis=-1), wo_ref[...])
    o_ref[0] = _layer_norm(alpha * x + att, g_ref[...], b_ref[...])


def _xattn(x, mk, mv, wq, wo, g, b, *, alpha, tm=512):
    bsz, l, d = x.shape
    n_mem = mk.shape[1]
    hd = d // H_X
    tm = _tile(l, tm)
    full = lambda shape: pl.BlockSpec(shape, lambda bi, i: (0,) * len(shape))
    return pl.pallas_call(
        functools.partial(_xattn_kernel, alpha=alpha, hd=hd),
        grid=(bsz, l // tm),
        in_specs=[
            pl.BlockSpec((1, tm, d), lambda bi, i: (bi, i, 0)),
            pl.BlockSpec((1, n_mem, d), lambda bi, i: (bi, 0, 0)),
            pl.BlockSpec((1, n_mem, d), lambda bi, i: (bi, 0, 0)),
            full(wq.shape), full(wo.shape), full((1, d)), full((1, d)),
        ],
        out_specs=pl.BlockSpec((1, tm, d), lambda bi, i: (bi, i, 0)),
        out_shape=jax.ShapeDtypeStruct((bsz, l, d), F32),
        compiler_params=_params("parallel", "parallel"),
        name="xattn_ln",
    )(x, mk, mv, wq, wo, g, b)


def _mlp_kernel(x_ref, w1_ref, w2_ref, g_ref, b_ref, o_ref, xb_scr, *, alpha, parts):
    f = pl.program_id(1)

    @pl.when(f == 0)
    def _():
        xb_scr[...] = x_ref[...].astype(BF16)
        o_ref[...] = jnp.zeros_like(o_ref)

    rows = x_ref.shape[0] // parts
    sl = [slice(n * rows, (n + 1) * rows) for n in range(parts)]
    hid = [jnp.maximum(jnp.dot(xb_scr[s, :], w1_ref[...], preferred_element_type=F32), 0.0) for s in sl]
    act = [(z * z).astype(BF16) for z in hid]
    for s, z in zip(sl, act):
        o_ref[s, :] += jnp.dot(z, w2_ref[...], preferred_element_type=F32)

    @pl.when(f == pl.num_programs(1) - 1)
    def _():
        o_ref[...] = _layer_norm(alpha * x_ref[...] + o_ref[...], g_ref[...], b_ref[...])


def _mlp(x, w1, w2, g, b, *, alpha, tm=1024, tf=1024):
    m, d = x.shape
    dff = w1.shape[1]
    tm = _tile(m, tm)
    tf = _tile(dff, tf)
    parts = 2 if tm % (2 * SUBLANES * 2) == 0 else 1
    return pl.pallas_call(
        functools.partial(_mlp_kernel, alpha=alpha, parts=parts),
        grid=(m // tm, dff // tf),
        in_specs=[
            pl.BlockSpec((tm, d), lambda i, f: (i, 0)),
            pl.BlockSpec((d, tf), lambda i, f: (0, f)),
            pl.BlockSpec((tf, d), lambda i, f: (f, 0)),
            pl.BlockSpec((1, d), lambda i, f: (0, 0)),
            pl.BlockSpec((1, d), lambda i, f: (0, 0)),
        ],
        out_specs=pl.BlockSpec((tm, d), lambda i, f: (i, 0)),
        out_shape=jax.ShapeDtypeStruct((m, d), F32),
        scratch_shapes=[pltpu.VMEM((tm, d), BF16)],
        compiler_params=_params("parallel", "arbitrary"),
        name="mlp_ln",
    )(x, w1, w2, g, b)


def _block_diag(blocks):
    g, r, c = blocks.shape
    eye = jnp.eye(g, dtype=blocks.dtype)
    return (eye[:, None, :, None] * blocks[:, :, None, :]).reshape(g * r, g * c)


def _pad_lanes(v):
    return jnp.pad(v, (0, LANES - v.shape[0]))[None, :]


def _s5_params(lam_re, lam_im, log_dt, b_re, b_im, c_re, c_im):
    dt = jnp.exp(log_dt)[:, None]
    mag = jnp.exp(lam_re * dt)
    ar, ai = mag * jnp.cos(lam_im * dt), mag * jnp.sin(lam_im * dt)
    den = lam_re * lam_re + lam_im * lam_im
    fr = ((ar - 1.0) * lam_re + ai * lam_im) / den
    fi = (ai * lam_re - (ar - 1.0) * lam_im) / den
    bbr = fr[..., None] * b_re - fi[..., None] * b_im
    bbi = fr[..., None] * b_im + fi[..., None] * b_re
    wb = jnp.concatenate([_block_diag(jnp.swapaxes(bbr, 1, 2)), _block_diag(jnp.swapaxes(bbi, 1, 2))], axis=1)
    wc = jnp.concatenate([_block_diag(jnp.swapaxes(c_re, 1, 2)), -_block_diag(jnp.swapaxes(c_im, 1, 2))], axis=0)
    ar, ai = ar.reshape(-1), ai.reshape(-1)
    pows = [(ar, ai)]
    for _ in range(SUBLANES - 1):
        pr, pi = pows[-1]
        pows.append((pr * ar - pi * ai, pr * ai + pi * ar))
    cat = lambda idx: jnp.stack([jnp.concatenate(pows[n]) for n in idx], axis=0)
    row = jnp.arange(SUBLANES)[:, None]
    shift_pows = jnp.stack([jnp.where(row >= (1 << lvl), cat(((1 << lvl) - 1,)), 0.0) for lvl in range(3)])
    return wb.astype(BF16), shift_pows, cat(range(SUBLANES)), wc.astype(BF16)


def _rel_bias_table(table, n_rows, n_cols, offset):
    tab = table.astype(F32).T
    rel_min, rel_max = offset - (n_cols - 1), offset + n_rows - 1
    lo, hi = max(rel_min, -REL_CLIP), min(rel_max, REL_CLIP)
    parts = [jnp.repeat(tab[:, :1], lo - rel_min, axis=1), tab[:, lo + REL_CLIP:hi + REL_CLIP + 1],
             jnp.repeat(tab[:, -1:], rel_max - hi, axis=1)]
    ext = jnp.concatenate(parts, axis=1)
    length = n_rows + n_cols - 1
    flipped = jnp.pad(ext[:, ::-1], ((0, 0), (0, 1)))
    shifted = jnp.tile(flipped, (1, n_rows))[:, :n_rows * length].reshape(-1, n_rows, length)
    return shifted[:, :, n_rows - 1:n_rows - 1 + n_cols]


def _band_prompt_bias(table, qb, qp):
    span = N_PREV * CHUNK
    width = span + qp
    bias = _rel_bias_table(table, qp, width, span)
    r = jnp.arange(qp)[:, None] // CHUNK
    j = jnp.arange(width)[None, :]
    in_band = (j // CHUNK >= r) & (j // CHUNK <= r + N_PREV)
    regular = jnp.where(in_band, bias, -jnp.inf)
    first = jnp.stack([jnp.where(in_band & (j >= span - p * qp), bias, -jnp.inf) for p in range(qb // qp)])
    return jnp.stack([first, jnp.broadcast_to(regular, first.shape)])


def _trunk_layer(x, mem_k, mem_v, gdn_conv, gdn_s, s5_h, rg_conv, rg_h, band_k, band_v, p, *, alpha):
    b, l, d = x.shape
    dg = d // N_MIX
    m = b * l
    pa, pb, pc, pd, pdb = _inproj(x.reshape(m, d), p["w_in"], (4 * dg, dg, 2 * dg, 3 * dg, LANES))
    pa, pb, pc, pd, pdb = [t.reshape(b, l, -1) for t in (pa, pb, pc, pd, pdb)]

    pad8 = lambda buf: jnp.pad(buf, ((0, 0), (SUBLANES - (CONV_W - 1), 0), (0, 0)))
    chunk = CHUNK if l % CHUNK == 0 else l
    cps = max(1, min(4, l // chunk))
    o_a, gdn_s_new = _gdn(pa, pdb, pad8(gdn_conv), gdn_s, p["gdn_conv_w"], p["gdn_conv_b"], p["gdn_a_log"],
                          p["gdn_dt_bias"], p["gdn_norm_g"], t=chunk, cps=cps)
    gdn_conv_new = pa[:, l - (CONV_W - 1):, :3 * dg]

    ns = p["s5_wb"].shape[1] // 2
    h0 = jnp.concatenate([s5_h[..., 0].reshape(b, 1, ns), s5_h[..., 1].reshape(b, 1, ns)], axis=-1)
    o_b, h_last = _s5(pb, h0, p["s5_wb"], p["s5_ap"], p["s5_p8"], p["s5_wc"], p["s5_d"], p["s5_w_glu"],
                      p["s5_b_glu"], t=_tile(l, 256))
    s5_h_new = jnp.stack([h_last[:, 0, :ns].reshape(s5_h.shape[:-1]), h_last[:, 0, ns:].reshape(s5_h.shape[:-1])],
                         axis=-1)

    o_c, rg_last = _rglru(pc, pad8(rg_conv), rg_h[:, None, :], p["rg_conv_w"], p["rg_conv_b"], p["rg_wri"],
                          p["rg_bri"], p["rg_lam"], t=_tile(l, 256))
    rg_conv_new = pc[:, l - (CONV_W - 1):, :dg]
    rg_h_new = rg_last[:, 0, :]

    hd = dg // H_D
    k_d = pd[:, :, dg:2 * dg].reshape(b, l, H_D, hd)
    v_d = pd[:, :, 2 * dg:].reshape(b, l, H_D, hd)
    if band_k is None:
        qb = N_PREV * CHUNK
        o_d = _band_prompt(pd, p["band_bias_prompt"], qb=qb, qp=BAND_PIECE)
        rows = min(N_PREV * CHUNK, l)
        band_k_new, band_v_new = k_d[:, l - rows:], v_d[:, l - rows:]
    else:
        rows = band_k.shape[1]
        o_d = _band_sample(pd, band_k.reshape(b, rows, dg), band_v.reshape(b, rows, dg),
                           p["band_bias_cache"], p["band_bias_new"])
        band_k_new, band_v_new = k_d, v_d

    x2 = _outproj(x.reshape(m, d), o_a.reshape(m, dg), o_b.reshape(m, dg), o_c.reshape(m, dg),
                  o_d.reshape(m, dg), p["w_out"], p["ln_g"][0:1], p["ln_b"][0:1], alpha=alpha)
    x3 = _xattn(x2.reshape(b, l, d), mem_k, mem_v, p["xa_w_q"], p["xa_w_o"], p["ln_g"][1:2], p["ln_b"][1:2],
                alpha=alpha)
    x4 = _mlp(x3.reshape(m, d), p["mlp_w1"], p["mlp_w2"], p["ln_g"][2:3], p["ln_b"][2:3], alpha=alpha)
    return x4.reshape(b, l, d), (gdn_conv_new, gdn_s_new, s5_h_new, rg_conv_new, rg_h_new, band_k_new, band_v_new)


def kernel(x_prompt, x_sample, state_gdn_conv, state_gdn, state_s5, state_rglru_conv, state_rglru, cache_band_k, cache_band_v, cache_mem_k, cache_mem_v, mem_prompt, w_in, w_out, ln_g, ln_b, gdn_conv_w, gdn_conv_b, gdn_a_log, gdn_dt_bias, gdn_norm_g, s5_lam_re, s5_lam_im, s5_log_dt, s5_b_re, s5_b_im, s5_c_re, s5_c_im, s5_d, s5_w_glu, s5_b_glu, rg_conv_w, rg_conv_b, rg_w_r, rg_b_r, rg_w_i, rg_b_i, rg_lam, band_rel_bias, xa_w_q, xa_w_k, xa_w_v, xa_w_o, mlp_w1, mlp_w2):
    depth = w_in.shape[0]
    bp, lp, d = x_prompt.shape
    bs, ls, _ = x_sample.shape
    n_mem = mem_prompt.shape[1]
    dg = d // N_MIX
    hd_x = d // H_X
    alpha = (2.0 * depth) ** 0.25
    band_rows = cache_band_k.shape[2]

    sizes = (3 * dg, dg, H_A, H_A, dg, dg, dg, 3 * dg)
    offs = [0]
    for s in sizes:
        offs.append(offs[-1] + s)

    xp, xs = x_prompt, x_sample
    p_states, s_states = [], []
    for l in range(depth):
        wi = w_in[l]
        w_db = jnp.pad(wi[:, offs[2]:offs[4]], ((0, 0), (0, LANES - 2 * H_A)))
        w_in_l = jnp.concatenate([wi[:, offs[0]:offs[2]], wi[:, offs[4]:offs[5]], wi[:, offs[5]:offs[7]],
                                  wi[:, offs[7]:offs[8]], w_db], axis=1).astype(BF16)
        s5_wb, s5_ap, s5_p8, s5_wc = _s5_params(s5_lam_re[l], s5_lam_im[l], s5_log_dt[l], s5_b_re[l], s5_b_im[l],
                                                s5_c_re[l], s5_c_im[l])
        p = {
            "w_in": w_in_l, "w_out": w_out[l].astype(BF16), "ln_g": ln_g[l], "ln_b": ln_b[l],
            "gdn_conv_w": gdn_conv_w[l], "gdn_conv_b": gdn_conv_b[l][None, :],
            "gdn_a_log": _pad_lanes(gdn_a_log[l]), "gdn_dt_bias": _pad_lanes(gdn_dt_bias[l]),
            "gdn_norm_g": gdn_norm_g[l][None, :],
            "s5_wb": s5_wb, "s5_ap": s5_ap, "s5_p8": s5_p8, "s5_wc": s5_wc,
            "s5_d": s5_d[l].reshape(1, dg), "s5_w_glu": s5_w_glu[l].astype(BF16), "s5_b_glu": s5_b_glu[l][None, :],
            "rg_conv_w": rg_conv_w[l], "rg_conv_b": rg_conv_b[l][None, :],
            "rg_wri": jnp.concatenate([_block_diag(rg_w_r[l]), _block_diag(rg_w_i[l])], axis=1).astype(BF16),
            "rg_bri": jnp.concatenate([rg_b_r[l], rg_b_i[l]])[None, :], "rg_lam": rg_lam[l][None, :],
            "band_bias_prompt": _band_prompt_bias(band_rel_bias[l], N_PREV * CHUNK, BAND_PIECE),
            "band_bias_cache": _rel_bias_table(band_rel_bias[l], ls, band_rows, band_rows),
            "band_bias_new": _rel_bias_table(band_rel_bias[l], ls, ls, 0),
            "xa_w_q": xa_w_q[l].astype(BF16), "xa_w_o": xa_w_o[l].astype(BF16),
            "mlp_w1": mlp_w1[l].astype(BF16), "mlp_w2": mlp_w2[l].astype(BF16),
        }
        mem2 = mem_prompt.reshape(bp * n_mem, d)
        mk = _matmul(mem2, xa_w_k[l].astype(BF16)).reshape(bp, n_mem, d)
        mv = _matmul(mem2, xa_w_v[l].astype(BF16)).reshape(bp, n_mem, d)
        xp, st_p = _trunk_layer(
            xp, mk, mv,
            jnp.zeros((bp, CONV_W - 1, 3 * dg), F32), jnp.zeros((bp, H_A, dg // H_A, dg // H_A), F32),
            jnp.zeros((bp, dg // S5_CH, P_B, 2), F32), jnp.zeros((bp, CONV_W - 1, dg), F32),
            jnp.zeros((bp, dg), F32), None, None, p, alpha=alpha)
        p_states.append(st_p + (mk.reshape(bp, n_mem, H_X, hd_x), mv.reshape(bp, n_mem, H_X, hd_x)))
        xs, st_s = _trunk_layer(
            xs, cache_mem_k[l].reshape(bs, n_mem, d), cache_mem_v[l].reshape(bs, n_mem, d),
            state_gdn_conv[l], state_gdn[l], state_s5[l], state_rglru_conv[l], state_rglru[l],
            cache_band_k[l], cache_band_v[l], p, alpha=alpha)
        s_states.append(st_s)

    def stk(states, i):
        return jnp.stack([st[i] for st in states], axis=0)

    return (xp, xs,
            stk(p_states, 0), stk(p_states, 1), stk(p_states, 2), stk(p_states, 3), stk(p_states, 4),
            stk(p_states, 5), stk(p_states, 6), stk(p_states, 7), stk(p_states, 8),
            stk(s_states, 0), stk(s_states, 1), stk(s_states, 2), stk(s_states, 3), stk(s_states, 4),
            stk(s_states, 5), stk(s_states, 6))
```

```python
import functools
import math

import jax
import jax.numpy as jnp
from jax import lax
from jax.experimental import pallas as pl
from jax.experimental.pallas import tpu as pltpu

F32 = jnp.float32
BF16 = jnp.bfloat16
HIGHEST = lax.Precision.HIGHEST

N_MIX = 4
CONV_W = 4
CHUNK = 64
H_A = 4
S5_CH = 16
P_B = 64
H_C = 4
RG_C = 8.0
H_D = 4
N_PREV = 8
REL_CLIP = 128
H_X = 4
LN_EPS = 1e-5
NORM_EPS = 1e-6

LANES = 128
SUBLANES = 8
VMEM_LIMIT_BYTES = 56 * 1024 * 1024
GDN_BLOCK = 128
GDN_CHUNKS_PER_STEP = 8
BAND_PIECE = 256


def _params(*semantics):
    return pltpu.CompilerParams(dimension_semantics=semantics, vmem_limit_bytes=VMEM_LIMIT_BYTES)


def _tile(n, pref):
    t = min(n, pref)
    while n % t:
        t -= SUBLANES
    return t


def _mm(a, b):
    return jnp.dot(a.astype(BF16), b.astype(BF16), preferred_element_type=F32)


def _mm_nt(a, b):
    return lax.dot_general(a.astype(BF16), b.astype(BF16), (((1,), (1,)), ((), ())),
                           preferred_element_type=F32)


def _mm_tn(a, b):
    return lax.dot_general(a.astype(BF16), b.astype(BF16), (((0,), (0,)), ((), ())),
                           preferred_element_type=F32)


def _mm_f32(a, b):
    return jnp.dot(a, b, precision=HIGHEST, preferred_element_type=F32)


def _sigmoid(x):
    return 1.0 / (1.0 + jnp.exp(-x))


def _softplus(x):
    return jnp.maximum(x, 0.0) + jnp.log1p(jnp.exp(-jnp.abs(x)))


def _gelu_tanh(x):
    c = math.sqrt(2.0 / math.pi)
    return 0.5 * x * (1.0 + jnp.tanh(c * (x + 0.044715 * (x * x * x))))


def _layer_norm(z, g, b):
    mu = jnp.mean(z, axis=-1, keepdims=True)
    zc = z - mu
    var = jnp.mean(zc * zc, axis=-1, keepdims=True)
    return zc * lax.rsqrt(var + LN_EPS) * g + b


def _matmul_kernel(x_ref, w_ref, o_ref):
    o_ref[...] = _mm(x_ref[...], w_ref[...])


def _layer_spec(w, layer, block=None, index=None):
    block = tuple(w.shape[1:]) if block is None else block
    index = (lambda *_: (0,) * len(block)) if index is None else index
    return pl.BlockSpec((None,) + block, lambda *g: (layer,) + tuple(index(*g)))


def _matmul(x, w_bf16, layer, tm=512):
    m, k = x.shape
    n = w_bf16.shape[2]
    tm = _tile(m, tm)
    return pl.pallas_call(
        _matmul_kernel,
        grid=(m // tm,),
        in_specs=[pl.BlockSpec((tm, k), lambda i: (i, 0)), _layer_spec(w_bf16, layer)],
        out_specs=pl.BlockSpec((tm, n), lambda i: (i, 0)),
        out_shape=jax.ShapeDtypeStruct((m, n), F32),
        compiler_params=_params("parallel"),
        name="matmul",
    )(x, w_bf16)


def _inproj_kernel(x_ref, w_ref, *o_refs, bounds):
    xb = x_ref[...].astype(BF16)
    for o_ref, (s, e) in zip(o_refs, bounds):
        o_ref[...] = jnp.dot(xb, w_ref[:, s:e], preferred_element_type=F32)


def _inproj(x, w_bf16, layer, widths, tm=512):
    m, k = x.shape
    tm = _tile(m, tm)
    bounds, s = [], 0
    for w in widths:
        bounds.append((s, s + w))
        s += w
    return pl.pallas_call(
        functools.partial(_inproj_kernel, bounds=tuple(bounds)),
        grid=(m // tm,),
        in_specs=[pl.BlockSpec((tm, k), lambda i: (i, 0)), _layer_spec(w_bf16, layer)],
        out_specs=[pl.BlockSpec((tm, w), lambda i: (i, 0)) for w in widths],
        out_shape=[jax.ShapeDtypeStruct((m, w), F32) for w in widths],
        compiler_params=_params("parallel"),
        name="inproj",
    )(x, w_bf16)


def _inverse_masks(r, c, t):
    neg_diag8 = jnp.where((r >> 3) == (c >> 3), -1.0, 0.0)
    offs, lb = [], 3
    while (1 << lb) < t:
        off = ((r >> (lb + 1)) == (c >> (lb + 1))) & (((r >> lb) & 1) == 1) & (((c >> lb) & 1) == 0)
        offs.append(jnp.where(off, 1.0, 0.0))
        lb += 1
    return neg_diag8, offs


def _unit_lower_inverse_offdiag(a_list, masks):
    neg_diag8, offs = masks
    n1 = [a * neg_diag8 for a in a_list]
    n2 = [_mm(x, x) for x in n1]
    n3 = [_mm(x, x2) for x, x2 in zip(n1, n2)]
    n4 = [_mm(x2, x2) for x2 in n2]
    p = [x + x2 + x3 for x, x2, x3 in zip(n1, n2, n3)]
    pn4 = [_mm(pp, x4) for pp, x4 in zip(p, n4)]
    y = [pp + x4 + px for pp, x4, px in zip(p, n4, pn4)]
    for off in offs:
        m = [a * off for a in a_list]
        z = [mm + _mm(yy, mm) for yy, mm in zip(y, m)]
        zy = [_mm(zz, yy) for zz, yy in zip(z, y)]
        y = [yy - (zz + zzy) for yy, zz, zzy in zip(y, z, zy)]
    return y


def _split3(x):
    h1 = x.astype(BF16)
    r1 = x - h1.astype(F32)
    h2 = r1.astype(BF16)
    h3 = (r1 - h2.astype(F32)).astype(BF16)
    return h1, h2, h3


def _gdn_kernel(qkv_ref, gate_ref, db_ref, cbuf_ref, s0_ref, cw_ref, cb_ref, alog_ref, dtb_ref, ng_ref,
                o_ref, sfin_ref, xp_scr, s_scr, o_scr, *, t, cps, dk):
    i = pl.program_id(1)
    tb = t * cps
    nh = H_A
    dq = nh * dk

    @pl.when(i == 0)
    def _():
        xp_scr[0:SUBLANES, :] = cbuf_ref[0]
        s_scr[...] = s0_ref[0]

    x = qkv_ref[0]
    xp_scr[SUBLANES:SUBLANES + tb, :] = x
    base = SUBLANES - (CONV_W - 1)
    y = xp_scr[base:base + tb, :] * cw_ref[0:1, :]
    for j in range(1, CONV_W):
        y = y + xp_scr[base + j:base + j + tb, :] * cw_ref[j:j + 1, :]
    y = y + cb_ref[...]
    xp_scr[0:SUBLANES, :] = x[tb - SUBLANES:tb, :]
    y = y * _sigmoid(y)

    db = db_ref[0]
    log_a = -jnp.exp(alog_ref[...]) * _softplus(db + dtb_ref[...])
    beta_all = _sigmoid(db)

    bs = min(tb, GDN_BLOCK)
    lt = t.bit_length() - 1
    r = lax.broadcasted_iota(jnp.int32, (bs, bs), 0)
    c = lax.broadcasted_iota(jnp.int32, (bs, bs), 1)
    same = (r >> lt) == (c >> lt)
    causal_neg = jnp.where(same & (r >= c), 0.0, -jnp.inf)
    strict_f = jnp.where(same & (r > c), 1.0, 0.0)
    tril = jnp.where(same & (r >= c), 1.0, 0.0).astype(BF16)
    striu = jnp.where(same & (r < c), 1.0, 0.0).astype(BF16)
    inv_masks = _inverse_masks(r, c, t)

    blocks = list(range(0, tb, bs))
    pairs = [(bi, h) for bi in range(len(blocks)) for h in range(nh)]
    la3 = [_split3(log_a[b0:b0 + bs, :]) for b0 in blocks]
    gc = [sum(jnp.dot(tril, part, preferred_element_type=F32) for part in parts) for parts in la3]
    rv = [sum(jnp.dot(striu, part, preferred_element_type=F32) for part in parts) for parts in la3]
    eg = [jnp.exp(g) for g in gc]
    erv = [jnp.exp(g) for g in rv]
    e_tot = [jnp.exp(g + g2) for g, g2 in zip(gc, rv)]
    gc_rows = [g.T for g in gc]

    def head_cols(z, bi, h, off):
        b0 = blocks[bi]
        return z[b0:b0 + bs, off + h * dk:off + (h + 1) * dk]

    q = [head_cols(y, bi, h, 0) for bi, h in pairs]
    k = [head_cols(y, bi, h, dq) for bi, h in pairs]
    v = [head_cols(y, bi, h, 2 * dq) for bi, h in pairs]
    q = [z * lax.rsqrt(jnp.sum(z * z, axis=-1, keepdims=True) + NORM_EPS) * (dk ** -0.5) for z in q]
    k = [z * lax.rsqrt(jnp.sum(z * z, axis=-1, keepdims=True) + NORM_EPS) for z in k]
    decay = [jnp.exp(gc[bi][:, h:h + 1] - gc_rows[bi][h:h + 1, :] + causal_neg) for bi, h in pairs]
    beta = [beta_all[blocks[bi]:blocks[bi] + bs, nh + h:nh + h + 1] for bi, h in pairs]
    eg_col = [eg[bi][:, h:h + 1] for bi, h in pairs]
    kk = [_mm_nt(z, z) for z in k]
    qk = [_mm_nt(zq, zk) for zq, zk in zip(q, k)]
    a = [(b * z * d) * strict_f for b, z, d in zip(beta, kk, decay)]
    qk = [z * d for z, d in zip(qk, decay)]
    y_inv = _unit_lower_inverse_offdiag(a, inv_masks)
    rhs = [jnp.concatenate([zv * b, zk * (b * e)], axis=-1) for zv, zk, b, e in zip(v, k, beta, eg_col)]
    sol = [z + _mm(yi, z) for yi, z in zip(y_inv, rhs)]
    q_dec = [z * e for z, e in zip(q, eg_col)]
    k_dec = [z * erv[bi][:, h:h + 1] for z, (bi, h) in zip(k, pairs)]

    states = [s_scr[h] for h in range(nh)]
    v_news = [[] for _ in pairs]
    o_inter = [[] for _ in pairs]
    for bi in range(len(blocks)):
        for r0 in range(0, bs, t):
            idx = [bi * nh + h for h in range(nh)]
            ws = [_mm(jnp.concatenate([sol[n][r0:r0 + t, dk:], q_dec[n][r0:r0 + t]], axis=0), states[h])
                  for h, n in enumerate(idx)]
            v_new = [sol[n][r0:r0 + t, :dk] - z[:t] for n, z in zip(idx, ws)]
            kv = [_mm_tn(k_dec[n][r0:r0 + t], z) for n, z in zip(idx, v_new)]
            states = [s * e_tot[bi][r0:r0 + 1, h:h + 1] + z for h, (s, z) in enumerate(zip(states, kv))]
            for n, z, z2 in zip(idx, v_new, ws):
                v_news[n].append(z)
                o_inter[n].append(z2[t:])
    o = [jnp.concatenate(oi, axis=0) + _mm(z, jnp.concatenate(vn, axis=0))
         for oi, z, vn in zip(o_inter, qk, v_news)]
    o = [z * lax.rsqrt(jnp.mean(z * z, axis=-1, keepdims=True) + NORM_EPS) * ng_ref[...] for z in o]
    for (bi, h), z in zip(pairs, o):
        o_scr[blocks[bi]:blocks[bi] + bs, h * dk:(h + 1) * dk] = z
    for h in range(nh):
        s_scr[h] = states[h]

    g = gate_ref[0]
    o_ref[0] = (o_scr[...] * (g * _sigmoid(g))).astype(o_ref.dtype)

    @pl.when(i == pl.num_programs(1) - 1)
    def _():
        sfin_ref[0] = s_scr[...]


def _gdn(pa, pdb, cbuf8, s0, cw, cb, alog, dtb, ng, *, t, cps):
    b, l, _ = pa.shape
    nh, dk = s0.shape[1], s0.shape[2]
    dq = nh * dk
    tb = t * cps
    kern = functools.partial(_gdn_kernel, t=t, cps=cps, dk=dk)
    full = lambda shape: pl.BlockSpec(shape, lambda bi, i: (0,) * len(shape))
    return pl.pallas_call(
        kern,
        grid=(b, l // tb),
        in_specs=[
            pl.BlockSpec((1, tb, 3 * dq), lambda bi, i: (bi, i, 0)),
            pl.BlockSpec((1, tb, dq), lambda bi, i: (bi, i, 3)),
            pl.BlockSpec((1, tb, LANES), lambda bi, i: (bi, i, 0)),
            pl.BlockSpec((1, SUBLANES, 3 * dq), lambda bi, i: (bi, 0, 0)),
            pl.BlockSpec((1, nh, dk, dk), lambda bi, i: (bi, 0, 0, 0)),
            full((CONV_W, 3 * dq)), full((1, 3 * dq)), full((1, LANES)), full((1, LANES)), full((1, dk)),
        ],
        out_specs=[
            pl.BlockSpec((1, tb, dq), lambda bi, i: (bi, i, 0)),
            pl.BlockSpec((1, nh, dk, dk), lambda bi, i: (bi, 0, 0, 0)),
        ],
        out_shape=[jax.ShapeDtypeStruct((b, l, dq), BF16), jax.ShapeDtypeStruct((b, nh, dk, dk), F32)],
        scratch_shapes=[pltpu.VMEM((SUBLANES + tb, 3 * dq), F32), pltpu.VMEM((nh, dk, dk), F32),
                        pltpu.VMEM((tb, dq), F32)],
        compiler_params=_params("parallel", "arbitrary"),
        name="gdn",
    )(pa, pa, pdb, cbuf8, s0, cw, cb, alog, dtb, ng)


def _s5_kernel(u_ref, h0_ref, wb_ref, ap_ref, p8_ref, wc_ref, d_ref, wg_ref, bg_ref,
               o_ref, hl_ref, carry_scr, h_scr, *, t, ns):
    i = pl.program_id(1)

    @pl.when(i == 0)
    def _():
        carry_scr[...] = h0_ref[0]

    u = u_ref[0]
    ng = t // SUBLANES
    dg = u.shape[1]
    first = lax.broadcasted_iota(jnp.int32, (SUBLANES, dg), 0) == 0
    u_prev = jnp.where(first, 0.0, pltpu.roll(u.reshape(ng, SUBLANES, dg), 1, 1)).reshape(t, dg)
    x = _mm(jnp.concatenate([u, u_prev], axis=-1), wb_ref[...])
    xr = x[:, :ns].reshape(ng, SUBLANES, ns)
    xi = x[:, ns:].reshape(ng, SUBLANES, ns)
    for lvl in range(2):
        s = 2 << lvl
        pr, pi = ap_ref[lvl, :, :ns], ap_ref[lvl, :, ns:]
        sr = pltpu.roll(xr, s, 1)
        si = pltpu.roll(xi, s, 1)
        xr, xi = xr + (pr * sr - pi * si), xi + (pr * si + pi * sr)
    p8r, p8i = p8_ref[:, :ns], p8_ref[:, ns:]
    cr, ci = carry_scr[:, :ns], carry_scr[:, ns:]
    for j in range(ng):
        sl = slice(j * SUBLANES, (j + 1) * SUBLANES)
        br = xr[j] + (p8r * cr - p8i * ci)
        bi = xi[j] + (p8r * ci + p8i * cr)
        h_scr[sl, :ns] = br
        h_scr[sl, ns:] = bi
        cr, ci = br[SUBLANES - 1:SUBLANES], bi[SUBLANES - 1:SUBLANES]
    carry_scr[:, :ns] = cr
    carry_scr[:, ns:] = ci
    hl_ref[0] = carry_scr[...]

    y = _mm(h_scr[...], wc_ref[...]) + d_ref[...] * u
    y = _gelu_tanh(y)
    z = _mm(y, wg_ref[...]) + bg_ref[...]
    o_ref[0] = (y * _sigmoid(z)).astype(o_ref.dtype)


def _s5(pb, h0, wb, ap, p8, wc, d, wg, bg, *, t):
    b, l, dg = pb.shape
    ns2 = wb.shape[1]
    ns = ns2 // 2
    full = lambda shape: pl.BlockSpec(shape, lambda bi, i: (0,) * len(shape))
    return pl.pallas_call(
        functools.partial(_s5_kernel, t=t, ns=ns),
        grid=(b, l // t),
        in_specs=[
            pl.BlockSpec((1, t, dg), lambda bi, i: (bi, i, 0)),
            pl.BlockSpec((1, 1, ns2), lambda bi, i: (bi, 0, 0)),
            full((2 * dg, ns2)), full((2, SUBLANES, ns2)), full((SUBLANES, ns2)), full((ns2, dg)), full((1, dg)),
            full((dg, dg)), full((1, dg)),
        ],
        out_specs=[
            pl.BlockSpec((1, t, dg), lambda bi, i: (bi, i, 0)),
            pl.BlockSpec((1, 1, ns2), lambda bi, i: (bi, 0, 0)),
        ],
        out_shape=[jax.ShapeDtypeStruct((b, l, dg), BF16), jax.ShapeDtypeStruct((b, 1, ns2), F32)],
        scratch_shapes=[pltpu.VMEM((1, ns2), F32), pltpu.VMEM((t, ns2), F32)],
        compiler_params=_params("parallel", "arbitrary"),
        name="s5",
    )(pb, h0, wb, ap, p8, wc, d, wg, bg)


def _rglru_kernel(xg_ref, cbuf_ref, h0_ref, cw_ref, cb_ref, wri_ref, bri_ref, lam_ref,
                  o_ref, hl_ref, xp_scr, carry_scr, h_scr, *, t, dg):
    i = pl.program_id(1)

    @pl.when(i == 0)
    def _():
        xp_scr[0:SUBLANES, :] = cbuf_ref[0]
        carry_scr[...] = h0_ref[0]

    x = xg_ref[0][:, :dg]
    gb = xg_ref[0][:, dg:]
    xp_scr[SUBLANES:SUBLANES + t, :] = x
    base = SUBLANES - (CONV_W - 1)
    y = xp_scr[base:base + t, :] * cw_ref[0:1, :]
    for j in range(1, CONV_W):
        y = y + xp_scr[base + j:base + j + t, :] * cw_ref[j:j + 1, :]
    y = y + cb_ref[...]
    xp_scr[0:SUBLANES, :] = x[t - SUBLANES:t, :]

    ri = _mm(y, wri_ref[...]) + bri_ref[...]
    rg = _sigmoid(ri[:, :dg])
    ig = _sigmoid(ri[:, dg:])
    log_a = (-RG_C * rg) * _softplus(-lam_ref[...])
    a = jnp.exp(log_a)
    th = jnp.tanh(log_a)
    xin = jnp.sqrt(-2.0 * th / (1.0 - th)) * (ig * y)

    ng = t // SUBLANES
    a = a.reshape(ng, SUBLANES, dg)
    xin = xin.reshape(ng, SUBLANES, dg)
    row = lax.broadcasted_iota(jnp.int32, (SUBLANES, dg), 0)
    for lvl in range(3):
        s = 1 << lvl
        keep = row >= s
        a_s = jnp.where(keep, pltpu.roll(a, s, 1), 1.0)
        x_s = jnp.where(keep, pltpu.roll(xin, s, 1), 0.0)
        xin = a * x_s + xin
        a = a * a_s
    cr = carry_scr[...]
    for j in range(ng):
        sl = slice(j * SUBLANES, (j + 1) * SUBLANES)
        hb = xin[j] + a[j] * cr
        h_scr[sl, :] = hb
        cr = hb[SUBLANES - 1:SUBLANES]
    carry_scr[...] = cr
    hl_ref[0] = cr
    o_ref[0] = (h_scr[...] * _gelu_tanh(gb)).astype(o_ref.dtype)


def _rglru(pc, cbuf8, h0, cw, cb, wri, bri, lam, *, t):
    b, l, dg2 = pc.shape
    dg = dg2 // 2
    full = lambda shape: pl.BlockSpec(shape, lambda bi, i: (0,) * len(shape))
    return pl.pallas_call(
        functools.partial(_rglru_kernel, t=t, dg=dg),
        grid=(b, l // t),
        in_specs=[
            pl.BlockSpec((1, t, dg2), lambda bi, i: (bi, i, 0)),
            pl.BlockSpec((1, SUBLANES, dg), lambda bi, i: (bi, 0, 0)),
            pl.BlockSpec((1, 1, dg), lambda bi, i: (bi, 0, 0)),
            full((CONV_W, dg)), full((1, dg)), full((dg, dg2)), full((1, dg2)), full((1, dg)),
        ],
        out_specs=[
            pl.BlockSpec((1, t, dg), lambda bi, i: (bi, i, 0)),
            pl.BlockSpec((1, 1, dg), lambda bi, i: (bi, 0, 0)),
        ],
        out_shape=[jax.ShapeDtypeStruct((b, l, dg), BF16), jax.ShapeDtypeStruct((b, 1, dg), F32)],
        scratch_shapes=[pltpu.VMEM((SUBLANES + t, dg), F32), pltpu.VMEM((1, dg), F32),
                        pltpu.VMEM((t, dg), F32)],
        compiler_params=_params("parallel", "arbitrary"),
        name="rglru",
    )(pc, cbuf8, h0, cw, cb, wri, bri, lam)


def _band_prompt_kernel(q_ref, kc_ref, vc_ref, kp_ref, vp_ref, bias_ref, o_ref, *, qb, qp, hd):
    span = N_PREV * CHUNK
    q = q_ref[0] * (hd ** -0.5)
    k = jnp.concatenate([kp_ref[0], kc_ref[0]], axis=0)
    v = jnp.concatenate([vp_ref[0], vc_ref[0]], axis=0)
    units = [(h, p) for h in range(H_D) for p in range(qb // qp)]
    col = lambda h: slice(h * hd, (h + 1) * hd)
    win = lambda p: slice(qb - span + p * qp, qb + (p + 1) * qp)
    s = [_mm_nt(q[p * qp:(p + 1) * qp, col(h)], k[win(p), col(h)]) + bias_ref[0, p, h] for h, p in units]
    e = [jnp.exp(z - jnp.max(z, axis=-1, keepdims=True)) for z in s]
    den = [jnp.sum(z, axis=-1, keepdims=True) for z in e]
    o = [_mm(z, v[win(p), col(h)]) / d for z, d, (h, p) in zip(e, den, units)]
    npc = qb // qp
    o_ref[0] = jnp.concatenate([jnp.concatenate(o[h * npc:(h + 1) * npc], axis=0) for h in range(H_D)],
                               axis=-1).astype(o_ref.dtype)


def _band_prompt(pd, bias, *, qb, qp):
    b, l, w3 = pd.shape
    w = w3 // 3
    hd = w // H_D
    prev = lambda bi, i: jnp.maximum(i - 1, 0)
    return pl.pallas_call(
        functools.partial(_band_prompt_kernel, qb=qb, qp=qp, hd=hd),
        grid=(b, l // qb),
        in_specs=[
            pl.BlockSpec((1, qb, w), lambda bi, i: (bi, i, 0)),
            pl.BlockSpec((1, qb, w), lambda bi, i: (bi, i, 1)),
            pl.BlockSpec((1, qb, w), lambda bi, i: (bi, i, 2)),
            pl.BlockSpec((1, qb, w), lambda bi, i: (bi, prev(bi, i), 1)),
            pl.BlockSpec((1, qb, w), lambda bi, i: (bi, prev(bi, i), 2)),
            pl.BlockSpec((1,) + bias.shape[1:], lambda bi, i: (jnp.minimum(i, 1), 0, 0, 0, 0)),
        ],
        out_specs=pl.BlockSpec((1, qb, w), lambda bi, i: (bi, i, 0)),
        out_shape=jax.ShapeDtypeStruct((b, l, w), BF16),
        compiler_params=_params("parallel", "arbitrary"),
        name="band_prompt",
    )(pd, pd, pd, pd, pd, bias)


def _band_sample_kernel(qkv_ref, kc_ref, vc_ref, bc_ref, bn_ref, o_ref, *, hd):
    w = H_D * hd
    x = qkv_ref[0]
    q, kn, vn = x[:, :w], x[:, w:2 * w], x[:, 2 * w:]
    kc = kc_ref[0]
    vc = vc_ref[0]
    outs = []
    for h in range(H_D):
        sl = slice(h * hd, (h + 1) * hd)
        sc = _mm_nt(q[:, sl], kc[:, sl]) * (hd ** -0.5) + bc_ref[h]
        sn = _mm_nt(q[:, sl], kn[:, sl]) * (hd ** -0.5) + bn_ref[h]
        m = jnp.maximum(jnp.max(sc, axis=-1, keepdims=True), jnp.max(sn, axis=-1, keepdims=True))
        pc = jnp.exp(sc - m)
        pn = jnp.exp(sn - m)
        den = jnp.sum(pc, axis=-1, keepdims=True) + jnp.sum(pn, axis=-1, keepdims=True)
        outs.append(_mm(pc / den, vc[:, sl]) + _mm(pn / den, vn[:, sl]))
    o_ref[0] = jnp.concatenate(outs, axis=-1).astype(o_ref.dtype)


def _band_sample(pd, k_cache, v_cache, bias_c, bias_n):
    b, l, w3 = pd.shape
    w = w3 // 3
    hd = w // H_D
    rows = k_cache.shape[1]
    return pl.pallas_call(
        functools.partial(_band_sample_kernel, hd=hd),
        grid=(b,),
        in_specs=[
            pl.BlockSpec((1, l, w3), lambda bi: (bi, 0, 0)),
            pl.BlockSpec((1, rows, w), lambda bi: (bi, 0, 0)),
            pl.BlockSpec((1, rows, w), lambda bi: (bi, 0, 0)),
            pl.BlockSpec((H_D, l, rows), lambda bi: (0, 0, 0)),
            pl.BlockSpec((H_D, l, l), lambda bi: (0, 0, 0)),
        ],
        out_specs=pl.BlockSpec((1, l, w), lambda bi: (bi, 0, 0)),
        out_shape=jax.ShapeDtypeStruct((b, l, w), BF16),
        compiler_params=_params("parallel"),
        name="band_sample",
    )(pd, k_cache, v_cache, bias_c, bias_n)


def _row_parts(rows, parts):
    if rows % (parts * 2 * SUBLANES):
        parts = 1
    step = rows // parts
    return [slice(n * step, (n + 1) * step) for n in range(parts)]


def _outproj_kernel(x_ref, oa_ref, ob_ref, oc_ref, od_ref, w_ref, g_ref, b_ref, o_ref, *, alpha, dg):
    for sl in _row_parts(x_ref.shape[0], 2):
        mix = jnp.dot(oa_ref[sl, :], w_ref[0:dg, :], preferred_element_type=F32)
        for n, r in enumerate((ob_ref, oc_ref, od_ref), start=1):
            mix = mix + jnp.dot(r[sl, :], w_ref[n * dg:(n + 1) * dg, :], preferred_element_type=F32)
        o_ref[sl, :] = _layer_norm(alpha * x_ref[sl, :] + mix, g_ref[...], b_ref[...])


def _outproj(x, oa, ob, oc, od, w_bf16, layer, g, b, *, alpha, tm=512):
    m, d = x.shape
    dg = oa.shape[1]
    tm = _tile(m, tm)
    row = lambda width: pl.BlockSpec((tm, width), lambda i: (i, 0))
    full = lambda shape: pl.BlockSpec(shape, lambda i: (0,) * len(shape))
    return pl.pallas_call(
        functools.partial(_outproj_kernel, alpha=alpha, dg=dg),
        grid=(m // tm,),
        in_specs=[row(d), row(dg), row(dg), row(dg), row(dg), _layer_spec(w_bf16, layer), full((1, d)),
                  full((1, d))],
        out_specs=row(d),
        out_shape=jax.ShapeDtypeStruct((m, d), F32),
        compiler_params=_params("parallel"),
        name="outproj_ln",
    )(x, oa, ob, oc, od, w_bf16, g, b)


def _xattn_kernel(x_ref, mk_ref, mv_ref, wq_ref, wo_ref, g_ref, b_ref, o_ref, *, alpha, hd):
    mk = mk_ref[0].astype(BF16)
    mv = mv_ref[0].astype(BF16)
    parts = _row_parts(x_ref.shape[1], 2)
    heads = [slice(h * hd, (h + 1) * hd) for h in range(H_X)]
    xs = [x_ref[0, sl, :] for sl in parts]
    q = [(_mm(x, wq_ref[...]) * (hd ** -0.5)).astype(BF16) for x in xs]
    s = [[_mm_nt(qp[:, hs], mk[:, hs]) for hs in heads] for qp in q]
    e = [[jnp.exp(z - jnp.max(z, axis=-1, keepdims=True)) for z in sp] for sp in s]
    pr = [[z * (1.0 / jnp.sum(z, axis=-1, keepdims=True)) for z in ep] for ep in e]
    pv = [jnp.concatenate([_mm(z, mv[:, hs]) for z, hs in zip(pp, heads)], axis=-1) for pp in pr]
    att = [_mm(z, wo_ref[...]) for z in pv]
    for sl, x, z in zip(parts, xs, att):
        o_ref[0, sl, :] = _layer_norm(alpha * x + z, g_ref[...], b_ref[...])


def _xattn(x, mk, mv, wq, wo, layer, g, b, *, alpha, tm=512):
    bsz, l, d = x.shape
    n_mem = mk.shape[1]
    hd = d // H_X
    tm = _tile(l, tm)
    full = lambda shape: pl.BlockSpec(shape, lambda bi, i: (0,) * len(shape))
    return pl.pallas_call(
        functools.partial(_xattn_kernel, alpha=alpha, hd=hd),
        grid=(bsz, l // tm),
        in_specs=[
            pl.BlockSpec((1, tm, d), lambda bi, i: (bi, i, 0)),
            pl.BlockSpec((1, n_mem, d), lambda bi, i: (bi, 0, 0)),
            pl.BlockSpec((1, n_mem, d), lambda bi, i: (bi, 0, 0)),
            _layer_spec(wq, layer), _layer_spec(wo, layer), full((1, d)), full((1, d)),
        ],
        out_specs=pl.BlockSpec((1, tm, d), lambda bi, i: (bi, i, 0)),
        out_shape=jax.ShapeDtypeStruct((bsz, l, d), F32),
        compiler_params=_params("parallel", "parallel"),
        name="xattn_ln",
    )(x, mk, mv, wq, wo, g, b)


def _mlp_kernel(x_ref, w1_ref, w2_ref, g_ref, b_ref, o_ref, xb_scr, *, alpha, parts):
    f = pl.program_id(1)

    @pl.when(f == 0)
    def _():
        xb_scr[...] = x_ref[...].astype(BF16)
        o_ref[...] = jnp.zeros_like(o_ref)

    sl = _row_parts(x_ref.shape[0], parts)
    hid = [jnp.maximum(jnp.dot(xb_scr[s, :], w1_ref[...], preferred_element_type=F32), 0.0) for s in sl]
    act = [(z * z).astype(BF16) for z in hid]
    for s, z in zip(sl, act):
        o_ref[s, :] += jnp.dot(z, w2_ref[...], preferred_element_type=F32)

    @pl.when(f == pl.num_programs(1) - 1)
    def _():
        o_ref[...] = _layer_norm(alpha * x_ref[...] + o_ref[...], g_ref[...], b_ref[...])


def _mlp(x, w1, w2, layer, g, b, *, alpha, tm=1024, tf=1024):
    m, d = x.shape
    dff = w1.shape[2]
    tm = _tile(m, tm)
    tf = _tile(dff, tf)
    return pl.pallas_call(
        functools.partial(_mlp_kernel, alpha=alpha, parts=2),
        grid=(m // tm, dff // tf),
        in_specs=[
            pl.BlockSpec((tm, d), lambda i, f: (i, 0)),
            _layer_spec(w1, layer, (d, tf), lambda i, f: (0, f)),
            _layer_spec(w2, layer, (tf, d), lambda i, f: (f, 0)),
            pl.BlockSpec((1, d), lambda i, f: (0, 0)),
            pl.BlockSpec((1, d), lambda i, f: (0, 0)),
        ],
        out_specs=pl.BlockSpec((tm, d), lambda i, f: (i, 0)),
        out_shape=jax.ShapeDtypeStruct((m, d), F32),
        scratch_shapes=[pltpu.VMEM((tm, d), BF16)],
        compiler_params=_params("parallel", "arbitrary"),
        name="mlp_ln",
    )(x, w1, w2, g, b)


def _block_diag(blocks):
    g, r, c = blocks.shape
    eye = jnp.eye(g, dtype=blocks.dtype)
    return (eye[:, None, :, None] * blocks[:, :, None, :]).reshape(g * r, g * c)


def _pad_lanes(v):
    return jnp.pad(v, (0, LANES - v.shape[0]))[None, :]


def _s5_params(lam_re, lam_im, log_dt, b_re, b_im, c_re, c_im):
    dt = jnp.exp(log_dt)[:, None]
    mag = jnp.exp(lam_re * dt)
    ar, ai = mag * jnp.cos(lam_im * dt), mag * jnp.sin(lam_im * dt)
    den = lam_re * lam_re + lam_im * lam_im
    fr = ((ar - 1.0) * lam_re + ai * lam_im) / den
    fi = (ai * lam_re - (ar - 1.0) * lam_im) / den
    bbr = fr[..., None] * b_re - fi[..., None] * b_im
    bbi = fr[..., None] * b_im + fi[..., None] * b_re
    wb = jnp.concatenate([_block_diag(jnp.swapaxes(bbr, 1, 2)), _block_diag(jnp.swapaxes(bbi, 1, 2))], axis=1)
    wc = jnp.concatenate([_block_diag(jnp.swapaxes(c_re, 1, 2)), -_block_diag(jnp.swapaxes(c_im, 1, 2))], axis=0)
    ar, ai = ar.reshape(-1), ai.reshape(-1)
    pows = [(ar, ai)]
    for _ in range(SUBLANES - 1):
        pr, pi = pows[-1]
        pows.append((pr * ar - pi * ai, pr * ai + pi * ar))
    cat = lambda idx: jnp.stack([jnp.concatenate(pows[n]) for n in idx], axis=0)
    row = jnp.arange(SUBLANES)[:, None]
    shift_pows = jnp.stack([jnp.where(row >= s, cat((s - 1,)), 0.0) for s in (2, 4)])
    wbr, wbi = wb[:, :ar.shape[0]], wb[:, ar.shape[0]:]
    wb_lag = jnp.concatenate([ar * wbr - ai * wbi, ar * wbi + ai * wbr], axis=1)
    wb2 = jnp.concatenate([wb, wb_lag], axis=0)
    return wb2.astype(BF16), shift_pows, cat(range(SUBLANES)), wc.astype(BF16)


def _rel_bias_table(table, n_rows, n_cols, offset):
    tab = table.astype(F32).T
    rel_min, rel_max = offset - (n_cols - 1), offset + n_rows - 1
    lo, hi = max(rel_min, -REL_CLIP), min(rel_max, REL_CLIP)
    parts = [jnp.repeat(tab[:, :1], lo - rel_min, axis=1), tab[:, lo + REL_CLIP:hi + REL_CLIP + 1],
             jnp.repeat(tab[:, -1:], rel_max - hi, axis=1)]
    ext = jnp.concatenate(parts, axis=1)
    length = n_rows + n_cols - 1
    flipped = jnp.pad(ext[:, ::-1], ((0, 0), (0, 1)))
    shifted = jnp.tile(flipped, (1, n_rows))[:, :n_rows * length].reshape(-1, n_rows, length)
    return shifted[:, :, n_rows - 1:n_rows - 1 + n_cols]


def _band_prompt_bias(table, qb, qp):
    span = N_PREV * CHUNK
    width = span + qp
    bias = _rel_bias_table(table, qp, width, span)
    r = jnp.arange(qp)[:, None] // CHUNK
    j = jnp.arange(width)[None, :]
    in_band = (j // CHUNK >= r) & (j // CHUNK <= r + N_PREV)
    regular = jnp.where(in_band, bias, -jnp.inf)
    first = jnp.stack([jnp.where(in_band & (j >= span - p * qp), bias, -jnp.inf) for p in range(qb // qp)])
    return jnp.stack([first, jnp.broadcast_to(regular, first.shape)])


def _trunk_layer(x, mem_k, mem_v, gdn_conv, gdn_s, s5_h, rg_conv, rg_h, band_k, band_v, p, *, alpha):
    b, l, d = x.shape
    dg = d // N_MIX
    m = b * l
    layer = p["layer"]
    pa, pb, pc, pd, pdb = _inproj(x.reshape(m, d), p["w_in"], layer, (4 * dg, dg, 2 * dg, 3 * dg, LANES))
    pa, pb, pc, pd, pdb = [t.reshape(b, l, -1) for t in (pa, pb, pc, pd, pdb)]

    pad8 = lambda buf: jnp.pad(buf, ((0, 0), (SUBLANES - (CONV_W - 1), 0), (0, 0)))
    chunk = CHUNK if l % CHUNK == 0 else l
    cps = max(1, min(GDN_CHUNKS_PER_STEP, l // chunk))
    o_a, gdn_s_new = _gdn(pa, pdb, pad8(gdn_conv), gdn_s, p["gdn_conv_w"], p["gdn_conv_b"], p["gdn_a_log"],
                          p["gdn_dt_bias"], p["gdn_norm_g"], t=chunk, cps=cps)
    gdn_conv_new = pa[:, l - (CONV_W - 1):, :3 * dg]

    ns = p["s5_wb"].shape[1] // 2
    h0 = jnp.concatenate([s5_h[..., 0].reshape(b, 1, ns), s5_h[..., 1].reshape(b, 1, ns)], axis=-1)
    o_b, h_last = _s5(pb, h0, p["s5_wb"], p["s5_ap"], p["s5_p8"], p["s5_wc"], p["s5_d"], p["s5_w_glu"],
                      p["s5_b_glu"], t=_tile(l, 256))
    s5_h_new = jnp.stack([h_last[:, 0, :ns].reshape(s5_h.shape[:-1]), h_last[:, 0, ns:].reshape(s5_h.shape[:-1])],
                         axis=-1)

    o_c, rg_last = _rglru(pc, pad8(rg_conv), rg_h[:, None, :], p["rg_conv_w"], p["rg_conv_b"], p["rg_wri"],
                          p["rg_bri"], p["rg_lam"], t=_tile(l, 256))
    rg_conv_new = pc[:, l - (CONV_W - 1):, :dg]
    rg_h_new = rg_last[:, 0, :]

    hd = dg // H_D
    if band_k is None:
        qb = N_PREV * CHUNK
        o_d = _band_prompt(pd, p["band_bias_prompt"], qb=qb, qp=BAND_PIECE)
        keep = min(N_PREV * CHUNK, l)
    else:
        rows = band_k.shape[1]
        o_d = _band_sample(pd, band_k.reshape(b, rows, dg), band_v.reshape(b, rows, dg),
                           p["band_bias_cache"], p["band_bias_new"])
        keep = l
    band_k_new = pd[:, l - keep:, dg:2 * dg].reshape(b, keep, H_D, hd)
    band_v_new = pd[:, l - keep:, 2 * dg:].reshape(b, keep, H_D, hd)

    x2 = _outproj(x.reshape(m, d), o_a.reshape(m, dg), o_b.reshape(m, dg), o_c.reshape(m, dg),
                  o_d.reshape(m, dg), p["w_out"], layer, p["ln_g"][0:1], p["ln_b"][0:1], alpha=alpha)
    x3 = _xattn(x2.reshape(b, l, d), mem_k, mem_v, p["xa_w_q"], p["xa_w_o"], layer, p["ln_g"][1:2],
                p["ln_b"][1:2], alpha=alpha)
    x4 = _mlp(x3.reshape(m, d), p["mlp_w1"], p["mlp_w2"], layer, p["ln_g"][2:3], p["ln_b"][2:3], alpha=alpha)
    return x4.reshape(b, l, d), (gdn_conv_new, gdn_s_new, s5_h_new, rg_conv_new, rg_h_new, band_k_new, band_v_new)


def kernel(x_prompt, x_sample, state_gdn_conv, state_gdn, state_s5, state_rglru_conv, state_rglru, cache_band_k, cache_band_v, cache_mem_k, cache_mem_v, mem_prompt, w_in, w_out, ln_g, ln_b, gdn_conv_w, gdn_conv_b, gdn_a_log, gdn_dt_bias, gdn_norm_g, s5_lam_re, s5_lam_im, s5_log_dt, s5_b_re, s5_b_im, s5_c_re, s5_c_im, s5_d, s5_w_glu, s5_b_glu, rg_conv_w, rg_conv_b, rg_w_r, rg_b_r, rg_w_i, rg_b_i, rg_lam, band_rel_bias, xa_w_q, xa_w_k, xa_w_v, xa_w_o, mlp_w1, mlp_w2):
    depth = w_in.shape[0]
    bp, lp, d = x_prompt.shape
    bs, ls, _ = x_sample.shape
    n_mem = mem_prompt.shape[1]
    dg = d // N_MIX
    hd_x = d // H_X
    alpha = (2.0 * depth) ** 0.25
    band_rows = cache_band_k.shape[2]

    sizes = (3 * dg, dg, H_A, H_A, dg, dg, dg, 3 * dg)
    offs = [0]
    for s in sizes:
        offs.append(offs[-1] + s)

    w_db = jnp.pad(w_in[:, :, offs[2]:offs[4]], ((0, 0), (0, 0), (0, LANES - 2 * H_A)))
    w_in_bf = jnp.concatenate([w_in[:, :, offs[0]:offs[2]], w_in[:, :, offs[4]:offs[5]], w_in[:, :, offs[5]:offs[7]],
                               w_in[:, :, offs[7]:offs[8]], w_db], axis=2).astype(BF16)
    w_out_bf, xa_w_q_bf, xa_w_k_bf, xa_w_v_bf, xa_w_o_bf, mlp_w1_bf, mlp_w2_bf = [
        w.astype(BF16) for w in (w_out, xa_w_q, xa_w_k, xa_w_v, xa_w_o, mlp_w1, mlp_w2)]

    xp, xs = x_prompt, x_sample
    p_states, s_states = [], []
    for l in range(depth):
        s5_wb, s5_ap, s5_p8, s5_wc = _s5_params(s5_lam_re[l], s5_lam_im[l], s5_log_dt[l], s5_b_re[l], s5_b_im[l],
                                                s5_c_re[l], s5_c_im[l])
        p = {
            "layer": l, "w_in": w_in_bf, "w_out": w_out_bf, "ln_g": ln_g[l], "ln_b": ln_b[l],
            "gdn_conv_w": gdn_conv_w[l], "gdn_conv_b": gdn_conv_b[l][None, :],
            "gdn_a_log": _pad_lanes(gdn_a_log[l]), "gdn_dt_bias": _pad_lanes(gdn_dt_bias[l]),
            "gdn_norm_g": gdn_norm_g[l][None, :],
            "s5_wb": s5_wb, "s5_ap": s5_ap, "s5_p8": s5_p8, "s5_wc": s5_wc,
            "s5_d": s5_d[l].reshape(1, dg), "s5_w_glu": s5_w_glu[l].astype(BF16), "s5_b_glu": s5_b_glu[l][None, :],
            "rg_conv_w": rg_conv_w[l], "rg_conv_b": rg_conv_b[l][None, :],
            "rg_wri": jnp.concatenate([_block_diag(rg_w_r[l]), _block_diag(rg_w_i[l])], axis=1).astype(BF16),
            "rg_bri": jnp.concatenate([rg_b_r[l], rg_b_i[l]])[None, :], "rg_lam": rg_lam[l][None, :],
            "band_bias_prompt": _band_prompt_bias(band_rel_bias[l], N_PREV * CHUNK, BAND_PIECE),
            "band_bias_cache": _rel_bias_table(band_rel_bias[l], ls, band_rows, band_rows),
            "band_bias_new": _rel_bias_table(band_rel_bias[l], ls, ls, 0),
            "xa_w_q": xa_w_q_bf, "xa_w_o": xa_w_o_bf, "mlp_w1": mlp_w1_bf, "mlp_w2": mlp_w2_bf,
        }
        mem2 = mem_prompt.reshape(bp * n_mem, d)
        mk = _matmul(mem2, xa_w_k_bf, l).reshape(bp, n_mem, d)
        mv = _matmul(mem2, xa_w_v_bf, l).reshape(bp, n_mem, d)
        xp, st_p = _trunk_layer(
            xp, mk, mv,
            jnp.zeros((bp, CONV_W - 1, 3 * dg), F32), jnp.zeros((bp, H_A, dg // H_A, dg // H_A), F32),
            jnp.zeros((bp, dg // S5_CH, P_B, 2), F32), jnp.zeros((bp, CONV_W - 1, dg), F32),
            jnp.zeros((bp, dg), F32), None, None, p, alpha=alpha)
        p_states.append(st_p + (mk.reshape(bp, n_mem, H_X, hd_x), mv.reshape(bp, n_mem, H_X, hd_x)))
        xs, st_s = _trunk_layer(
            xs, cache_mem_k[l].reshape(bs, n_mem, d), cache_mem_v[l].reshape(bs, n_mem, d),
            state_gdn_conv[l], state_gdn[l], state_s5[l], state_rglru_conv[l], state_rglru[l],
            cache_band_k[l], cache_band_v[l], p, alpha=alpha)
        s_states.append(st_s)

    def stk(states, i):
        return jnp.stack([st[i] for st in states], axis=0)

    return (xp, xs,
            stk(p_states, 0), stk(p_states, 1), stk(p_states, 2), stk(p_states, 3), stk(p_states, 4),
            stk(p_states, 5), stk(p_states, 6), stk(p_states, 7), stk(p_states, 8),
            stk(s_states, 0), stk(s_states, 1), stk(s_states, 2), stk(s_states, 3), stk(s_states, 4),
            stk(s_states, 5), stk(s_states, 6))
```

```python
import functools
import math

import jax
import jax.numpy as jnp
from jax import lax
from jax.experimental import pallas as pl
from jax.experimental.pallas import tpu as pltpu

F32 = jnp.float32
BF16 = jnp.bfloat16
HIGHEST = lax.Precision.HIGHEST

N_MIX = 4
CONV_W = 4
CHUNK = 64
H_A = 4
S5_CH = 16
P_B = 64
H_C = 4
RG_C = 8.0
H_D = 4
N_PREV = 8
REL_CLIP = 128
H_X = 4
LN_EPS = 1e-5
NORM_EPS = 1e-6

LANES = 128
SUBLANES = 8
VMEM_LIMIT_BYTES = 56 * 1024 * 1024
GDN_BLOCK = 128
GDN_CHUNKS_PER_STEP = 8
BAND_PIECE = 256
XATTN_SEQS_PER_STEP = 4


def _params(*semantics):
    return pltpu.CompilerParams(dimension_semantics=semantics, vmem_limit_bytes=VMEM_LIMIT_BYTES)


def _tile(n, pref):
    t = min(n, pref)
    while n % t:
        t -= SUBLANES
    return t


def _mm(a, b):
    return jnp.dot(a.astype(BF16), b.astype(BF16), preferred_element_type=F32)


def _mm_nt(a, b):
    return lax.dot_general(a.astype(BF16), b.astype(BF16), (((1,), (1,)), ((), ())),
                           preferred_element_type=F32)


def _mm_tn(a, b):
    return lax.dot_general(a.astype(BF16), b.astype(BF16), (((0,), (0,)), ((), ())),
                           preferred_element_type=F32)


def _mm_f32(a, b):
    return jnp.dot(a, b, precision=HIGHEST, preferred_element_type=F32)


def _sigmoid(x):
    return 1.0 / (1.0 + jnp.exp(-x))


def _softplus(x):
    return jnp.maximum(x, 0.0) + jnp.log1p(jnp.exp(-jnp.abs(x)))


def _gelu_tanh(x):
    c = math.sqrt(2.0 / math.pi)
    return 0.5 * x * (1.0 + jnp.tanh(c * (x + 0.044715 * (x * x * x))))


def _layer_norm(z, g, b):
    mu = jnp.mean(z, axis=-1, keepdims=True)
    zc = z - mu
    var = jnp.mean(zc * zc, axis=-1, keepdims=True)
    return zc * lax.rsqrt(var + LN_EPS) * g + b


def _matmul_kernel(x_ref, w_ref, o_ref):
    o_ref[...] = _mm(x_ref[...], w_ref[...])


def _layer_spec(w, layer, block=None, index=None):
    block = tuple(w.shape[1:]) if block is None else block
    index = (lambda *_: (0,) * len(block)) if index is None else index
    return pl.BlockSpec((None,) + block, lambda *g: (layer,) + tuple(index(*g)))


def _matmul(x, w_bf16, layer, tm=512):
    m, k = x.shape
    n = w_bf16.shape[2]
    tm = _tile(m, tm)
    return pl.pallas_call(
        _matmul_kernel,
        grid=(m // tm,),
        in_specs=[pl.BlockSpec((tm, k), lambda i: (i, 0)), _layer_spec(w_bf16, layer)],
        out_specs=pl.BlockSpec((tm, n), lambda i: (i, 0)),
        out_shape=jax.ShapeDtypeStruct((m, n), F32),
        compiler_params=_params("parallel"),
        name="matmul",
    )(x, w_bf16)


def _inproj_kernel(x_ref, w_ref, *o_refs, bounds):
    xb = x_ref[...].astype(BF16)
    for o_ref, (s, e) in zip(o_refs, bounds):
        o_ref[...] = jnp.dot(xb, w_ref[:, s:e], preferred_element_type=F32)


def _inproj(x, w_bf16, layer, widths, tm=512):
    m, k = x.shape
    tm = _tile(m, tm)
    bounds, s = [], 0
    for w in widths:
        bounds.append((s, s + w))
        s += w
    return pl.pallas_call(
        functools.partial(_inproj_kernel, bounds=tuple(bounds)),
        grid=(m // tm,),
        in_specs=[pl.BlockSpec((tm, k), lambda i: (i, 0)), _layer_spec(w_bf16, layer)],
        out_specs=[pl.BlockSpec((tm, w), lambda i: (i, 0)) for w in widths],
        out_shape=[jax.ShapeDtypeStruct((m, w), F32) for w in widths],
        compiler_params=_params("parallel"),
        name="inproj",
    )(x, w_bf16)


def _inverse_masks(r, c, t):
    neg_diag8 = jnp.where((r >> 3) == (c >> 3), -1.0, 0.0)
    offs, lb = [], 3
    while (1 << lb) < t:
        off = ((r >> (lb + 1)) == (c >> (lb + 1))) & (((r >> lb) & 1) == 1) & (((c >> lb) & 1) == 0)
        offs.append(jnp.where(off, 1.0, 0.0))
        lb += 1
    return neg_diag8, offs


def _unit_lower_inverse_offdiag(a_list, masks):
    neg_diag8, offs = masks
    n1 = [a * neg_diag8 for a in a_list]
    n2 = [_mm(x, x) for x in n1]
    n3 = [_mm(x, x2) for x, x2 in zip(n1, n2)]
    n4 = [_mm(x2, x2) for x2 in n2]
    p = [x + x2 + x3 for x, x2, x3 in zip(n1, n2, n3)]
    pn4 = [_mm(pp, x4) for pp, x4 in zip(p, n4)]
    y = [pp + x4 + px for pp, x4, px in zip(p, n4, pn4)]
    for off in offs:
        m = [a * off for a in a_list]
        z = [mm + _mm(yy, mm) for yy, mm in zip(y, m)]
        zy = [_mm(zz, yy) for zz, yy in zip(z, y)]
        y = [yy - (zz + zzy) for yy, zz, zzy in zip(y, z, zy)]
    return y


def _split3(x):
    h1 = x.astype(BF16)
    r1 = x - h1.astype(F32)
    h2 = r1.astype(BF16)
    h3 = (r1 - h2.astype(F32)).astype(BF16)
    return h1, h2, h3


def _gdn_kernel(qkv_ref, gate_ref, db_ref, cbuf_ref, s0_ref, cw_ref, cb_ref, alog_ref, dtb_ref, ng_ref,
                o_ref, sfin_ref, xp_scr, s_scr, o_scr, *, t, cps, dk):
    i = pl.program_id(1)
    tb = t * cps
    nh = H_A
    dq = nh * dk

    @pl.when(i == 0)
    def _():
        xp_scr[0:SUBLANES, :] = cbuf_ref[0]
        s_scr[...] = s0_ref[0]

    x = qkv_ref[0]
    xp_scr[SUBLANES:SUBLANES + tb, :] = x
    base = SUBLANES - (CONV_W - 1)
    y = xp_scr[base:base + tb, :] * cw_ref[0:1, :]
    for j in range(1, CONV_W):
        y = y + xp_scr[base + j:base + j + tb, :] * cw_ref[j:j + 1, :]
    y = y + cb_ref[...]
    xp_scr[0:SUBLANES, :] = x[tb - SUBLANES:tb, :]
    y = y * _sigmoid(y)

    db = db_ref[0]
    log_a = -jnp.exp(alog_ref[...]) * _softplus(db + dtb_ref[...])
    beta_all = _sigmoid(db)

    bs = min(tb, GDN_BLOCK)
    lt = t.bit_length() - 1
    r = lax.broadcasted_iota(jnp.int32, (bs, bs), 0)
    c = lax.broadcasted_iota(jnp.int32, (bs, bs), 1)
    same = (r >> lt) == (c >> lt)
    causal_neg = jnp.where(same & (r >= c), 0.0, -jnp.inf)
    strict_f = jnp.where(same & (r > c), 1.0, 0.0)
    tril = jnp.where(same & (r >= c), 1.0, 0.0).astype(BF16)
    striu = jnp.where(same & (r < c), 1.0, 0.0).astype(BF16)
    inv_masks = _inverse_masks(r, c, t)

    blocks = list(range(0, tb, bs))
    pairs = [(bi, h) for bi in range(len(blocks)) for h in range(nh)]
    la3 = [_split3(log_a[b0:b0 + bs, :]) for b0 in blocks]
    gc = [sum(jnp.dot(tril, part, preferred_element_type=F32) for part in parts) for parts in la3]
    rv = [sum(jnp.dot(striu, part, preferred_element_type=F32) for part in parts) for parts in la3]
    eg = [jnp.exp(g) for g in gc]
    erv = [jnp.exp(g) for g in rv]
    e_tot = [jnp.exp(g + g2) for g, g2 in zip(gc, rv)]
    gc_rows = [g.T for g in gc]

    def head_cols(z, bi, h, off):
        b0 = blocks[bi]
        return z[b0:b0 + bs, off + h * dk:off + (h + 1) * dk]

    q = [head_cols(y, bi, h, 0) for bi, h in pairs]
    k = [head_cols(y, bi, h, dq) for bi, h in pairs]
    v = [head_cols(y, bi, h, 2 * dq) for bi, h in pairs]
    q = [z * lax.rsqrt(jnp.sum(z * z, axis=-1, keepdims=True) + NORM_EPS) * (dk ** -0.5) for z in q]
    k = [z * lax.rsqrt(jnp.sum(z * z, axis=-1, keepdims=True) + NORM_EPS) for z in k]
    decay = [jnp.exp(gc[bi][:, h:h + 1] - gc_rows[bi][h:h + 1, :] + causal_neg) for bi, h in pairs]
    beta = [beta_all[blocks[bi]:blocks[bi] + bs, nh + h:nh + h + 1] for bi, h in pairs]
    eg_col = [eg[bi][:, h:h + 1] for bi, h in pairs]
    kk = [_mm_nt(z, z) for z in k]
    qk = [_mm_nt(zq, zk) for zq, zk in zip(q, k)]
    a = [(b * z * d) * strict_f for b, z, d in zip(beta, kk, decay)]
    qk = [z * d for z, d in zip(qk, decay)]
    y_inv = _unit_lower_inverse_offdiag(a, inv_masks)
    rhs = [jnp.concatenate([zv * b, zk * (b * e)], axis=-1) for zv, zk, b, e in zip(v, k, beta, eg_col)]
    sol = [z + _mm(yi, z) for yi, z in zip(y_inv, rhs)]
    q_dec = [z * e for z, e in zip(q, eg_col)]
    k_dec = [z * erv[bi][:, h:h + 1] for z, (bi, h) in zip(k, pairs)]

    states = [s_scr[h] for h in range(nh)]
    v_news = [[] for _ in pairs]
    o_inter = [[] for _ in pairs]
    for bi in range(len(blocks)):
        for r0 in range(0, bs, t):
            idx = [bi * nh + h for h in range(nh)]
            ws = [_mm(jnp.concatenate([sol[n][r0:r0 + t, dk:], q_dec[n][r0:r0 + t]], axis=0), states[h])
                  for h, n in enumerate(idx)]
            v_new = [sol[n][r0:r0 + t, :dk] - z[:t] for n, z in zip(idx, ws)]
            kv = [_mm_tn(k_dec[n][r0:r0 + t], z) for n, z in zip(idx, v_new)]
            states = [s * e_tot[bi][r0:r0 + 1, h:h + 1] + z for h, (s, z) in enumerate(zip(states, kv))]
            for n, z, z2 in zip(idx, v_new, ws):
                v_news[n].append(z)
                o_inter[n].append(z2[t:])
    o = [jnp.concatenate(oi, axis=0) + _mm(z, jnp.concatenate(vn, axis=0))
         for oi, z, vn in zip(o_inter, qk, v_news)]
    o = [z * lax.rsqrt(jnp.mean(z * z, axis=-1, keepdims=True) + NORM_EPS) * ng_ref[...] for z in o]
    for (bi, h), z in zip(pairs, o):
        o_scr[blocks[bi]:blocks[bi] + bs, h * dk:(h + 1) * dk] = z
    for h in range(nh):
        s_scr[h] = states[h]

    g = gate_ref[0]
    o_ref[0] = (o_scr[...] * (g * _sigmoid(g))).astype(o_ref.dtype)

    @pl.when(i == pl.num_programs(1) - 1)
    def _():
        sfin_ref[0] = s_scr[...]


def _gdn(pa, pdb, cbuf8, s0, cw, cb, alog, dtb, ng, *, t, cps):
    b, l, _ = pa.shape
    nh, dk = s0.shape[1], s0.shape[2]
    dq = nh * dk
    tb = t * cps
    full = lambda shape: pl.BlockSpec(shape, lambda bi, i: (0,) * len(shape))
    return pl.pallas_call(
        functools.partial(_gdn_kernel, t=t, cps=cps, dk=dk),
        grid=(b, l // tb),
        in_specs=[
            pl.BlockSpec((1, tb, 3 * dq), lambda bi, i: (bi, i, 0)),
            pl.BlockSpec((1, tb, dq), lambda bi, i: (bi, i, 3)),
            pl.BlockSpec((1, tb, LANES), lambda bi, i: (bi, i, 0)),
            pl.BlockSpec((1, SUBLANES, 3 * dq), lambda bi, i: (bi, 0, 0)),
            pl.BlockSpec((1, nh, dk, dk), lambda bi, i: (bi, 0, 0, 0)),
            full((CONV_W, 3 * dq)), full((1, 3 * dq)), full((1, LANES)), full((1, LANES)), full((1, dk)),
        ],
        out_specs=[
            pl.BlockSpec((1, tb, dq), lambda bi, i: (bi, i, 0)),
            pl.BlockSpec((1, nh, dk, dk), lambda bi, i: (bi, 0, 0, 0)),
        ],
        out_shape=[jax.ShapeDtypeStruct((b, l, dq), BF16), jax.ShapeDtypeStruct((b, nh, dk, dk), F32)],
        scratch_shapes=[pltpu.VMEM((SUBLANES + tb, 3 * dq), F32), pltpu.VMEM((nh, dk, dk), F32),
                        pltpu.VMEM((tb, dq), F32)],
        compiler_params=_params("parallel", "arbitrary"),
        name="gdn",
    )(pa, pa, pdb, cbuf8, s0, cw, cb, alog, dtb, ng)


def _s5_kernel(u_ref, h0_ref, wb_ref, ap_ref, p8_ref, wc_ref, d_ref, wg_ref, bg_ref,
               o_ref, hl_ref, carry_scr, h_scr, *, t, ns):
    i = pl.program_id(1)

    @pl.when(i == 0)
    def _():
        carry_scr[...] = h0_ref[0]

    u = u_ref[0]
    ng = t // SUBLANES
    dg = u.shape[1]
    first = lax.broadcasted_iota(jnp.int32, (SUBLANES, dg), 0) == 0
    u_prev = jnp.where(first, 0.0, pltpu.roll(u.reshape(ng, SUBLANES, dg), 1, 1)).reshape(t, dg)
    x = _mm(jnp.concatenate([u, u_prev], axis=-1), wb_ref[...])
    xr = x[:, :ns].reshape(ng, SUBLANES, ns)
    xi = x[:, ns:].reshape(ng, SUBLANES, ns)
    for lvl in range(2):
        s = 2 << lvl
        pr, pi = ap_ref[lvl, :, :ns], ap_ref[lvl, :, ns:]
        sr = pltpu.roll(xr, s, 1)
        si = pltpu.roll(xi, s, 1)
        xr, xi = xr + (pr * sr - pi * si), xi + (pr * si + pi * sr)
    p8r, p8i = p8_ref[:, :ns], p8_ref[:, ns:]
    cr, ci = carry_scr[:, :ns], carry_scr[:, ns:]
    for j in range(ng):
        sl = slice(j * SUBLANES, (j + 1) * SUBLANES)
        br = xr[j] + (p8r * cr - p8i * ci)
        bi = xi[j] + (p8r * ci + p8i * cr)
        h_scr[sl, :ns] = br
        h_scr[sl, ns:] = bi
        cr, ci = br[SUBLANES - 1:SUBLANES], bi[SUBLANES - 1:SUBLANES]
    carry_scr[:, :ns] = cr
    carry_scr[:, ns:] = ci
    hl_ref[0] = carry_scr[...]

    y = _mm(h_scr[...], wc_ref[...]) + d_ref[...] * u
    y = _gelu_tanh(y)
    z = _mm(y, wg_ref[...]) + bg_ref[...]
    o_ref[0] = (y * _sigmoid(z)).astype(o_ref.dtype)


def _s5(pb, h0, wb, ap, p8, wc, d, wg, bg, *, t):
    b, l, dg = pb.shape
    ns2 = wb.shape[1]
    ns = ns2 // 2
    full = lambda shape: pl.BlockSpec(shape, lambda bi, i: (0,) * len(shape))
    return pl.pallas_call(
        functools.partial(_s5_kernel, t=t, ns=ns),
        grid=(b, l // t),
        in_specs=[
            pl.BlockSpec((1, t, dg), lambda bi, i: (bi, i, 0)),
            pl.BlockSpec((1, 1, ns2), lambda bi, i: (bi, 0, 0)),
            full((2 * dg, ns2)), full((2, SUBLANES, ns2)), full((SUBLANES, ns2)), full((ns2, dg)), full((1, dg)),
            full((dg, dg)), full((1, dg)),
        ],
        out_specs=[
            pl.BlockSpec((1, t, dg), lambda bi, i: (bi, i, 0)),
            pl.BlockSpec((1, 1, ns2), lambda bi, i: (bi, 0, 0)),
        ],
        out_shape=[jax.ShapeDtypeStruct((b, l, dg), BF16), jax.ShapeDtypeStruct((b, 1, ns2), F32)],
        scratch_shapes=[pltpu.VMEM((1, ns2), F32), pltpu.VMEM((t, ns2), F32)],
        compiler_params=_params("parallel", "arbitrary"),
        name="s5",
    )(pb, h0, wb, ap, p8, wc, d, wg, bg)


def _rglru_kernel(xg_ref, cbuf_ref, h0_ref, cw_ref, cb_ref, wri_ref, bri_ref, lam_ref,
                  o_ref, hl_ref, xp_scr, carry_scr, h_scr, *, t, dg):
    i = pl.program_id(1)

    @pl.when(i == 0)
    def _():
        xp_scr[0:SUBLANES, :] = cbuf_ref[0]
        carry_scr[...] = h0_ref[0]

    x = xg_ref[0][:, :dg]
    gb = xg_ref[0][:, dg:]
    xp_scr[SUBLANES:SUBLANES + t, :] = x
    base = SUBLANES - (CONV_W - 1)
    y = xp_scr[base:base + t, :] * cw_ref[0:1, :]
    for j in range(1, CONV_W):
        y = y + xp_scr[base + j:base + j + t, :] * cw_ref[j:j + 1, :]
    y = y + cb_ref[...]
    xp_scr[0:SUBLANES, :] = x[t - SUBLANES:t, :]

    ri = _mm(y, wri_ref[...]) + bri_ref[...]
    rg = _sigmoid(ri[:, :dg])
    ig = _sigmoid(ri[:, dg:])
    log_a = (-RG_C * rg) * _softplus(-lam_ref[...])
    a = jnp.exp(log_a)
    th = jnp.tanh(log_a)
    xin = jnp.sqrt(-2.0 * th / (1.0 - th)) * (ig * y)

    ng = t // SUBLANES
    a = a.reshape(ng, SUBLANES, dg)
    xin = xin.reshape(ng, SUBLANES, dg)
    row = lax.broadcasted_iota(jnp.int32, (SUBLANES, dg), 0)
    for lvl in range(3):
        s = 1 << lvl
        keep = row >= s
        a_s = jnp.where(keep, pltpu.roll(a, s, 1), 1.0)
        x_s = jnp.where(keep, pltpu.roll(xin, s, 1), 0.0)
        xin = a * x_s + xin
        a = a * a_s
    cr = carry_scr[...]
    for j in range(ng):
        sl = slice(j * SUBLANES, (j + 1) * SUBLANES)
        hb = xin[j] + a[j] * cr
        h_scr[sl, :] = hb
        cr = hb[SUBLANES - 1:SUBLANES]
    carry_scr[...] = cr
    hl_ref[0] = cr
    o_ref[0] = (h_scr[...] * _gelu_tanh(gb)).astype(o_ref.dtype)


def _rglru(pc, cbuf8, h0, cw, cb, wri, bri, lam, *, t):
    b, l, dg2 = pc.shape
    dg = dg2 // 2
    full = lambda shape: pl.BlockSpec(shape, lambda bi, i: (0,) * len(shape))
    return pl.pallas_call(
        functools.partial(_rglru_kernel, t=t, dg=dg),
        grid=(b, l // t),
        in_specs=[
            pl.BlockSpec((1, t, dg2), lambda bi, i: (bi, i, 0)),
            pl.BlockSpec((1, SUBLANES, dg), lambda bi, i: (bi, 0, 0)),
            pl.BlockSpec((1, 1, dg), lambda bi, i: (bi, 0, 0)),
            full((CONV_W, dg)), full((1, dg)), full((dg, dg2)), full((1, dg2)), full((1, dg)),
        ],
        out_specs=[
            pl.BlockSpec((1, t, dg), lambda bi, i: (bi, i, 0)),
            pl.BlockSpec((1, 1, dg), lambda bi, i: (bi, 0, 0)),
        ],
        out_shape=[jax.ShapeDtypeStruct((b, l, dg), BF16), jax.ShapeDtypeStruct((b, 1, dg), F32)],
        scratch_shapes=[pltpu.VMEM((SUBLANES + t, dg), F32), pltpu.VMEM((1, dg), F32),
                        pltpu.VMEM((t, dg), F32)],
        compiler_params=_params("parallel", "arbitrary"),
        name="rglru",
    )(pc, cbuf8, h0, cw, cb, wri, bri, lam)


def _band_prompt_kernel(q_ref, kc_ref, vc_ref, kp_ref, vp_ref, bias_ref, o_ref, *, qb, qp, hd):
    span = N_PREV * CHUNK
    q = q_ref[0] * (hd ** -0.5)
    k = jnp.concatenate([kp_ref[0], kc_ref[0]], axis=0)
    v = jnp.concatenate([vp_ref[0], vc_ref[0]], axis=0)
    units = [(h, p) for h in range(H_D) for p in range(qb // qp)]
    col = lambda h: slice(h * hd, (h + 1) * hd)
    win = lambda p: slice(qb - span + p * qp, qb + (p + 1) * qp)
    s = [_mm_nt(q[p * qp:(p + 1) * qp, col(h)], k[win(p), col(h)]) + bias_ref[0, p, h] for h, p in units]
    e = [jnp.exp(z - jnp.max(z, axis=-1, keepdims=True)) for z in s]
    den = [jnp.sum(z, axis=-1, keepdims=True) for z in e]
    o = [_mm(z, v[win(p), col(h)]) / d for z, d, (h, p) in zip(e, den, units)]
    npc = qb // qp
    o_ref[0] = jnp.concatenate([jnp.concatenate(o[h * npc:(h + 1) * npc], axis=0) for h in range(H_D)],
                               axis=-1).astype(o_ref.dtype)


def _band_prompt(pd, bias, *, qb, qp):
    b, l, w3 = pd.shape
    w = w3 // 3
    hd = w // H_D
    prev = lambda bi, i: jnp.maximum(i - 1, 0)
    return pl.pallas_call(
        functools.partial(_band_prompt_kernel, qb=qb, qp=qp, hd=hd),
        grid=(b, l // qb),
        in_specs=[
            pl.BlockSpec((1, qb, w), lambda bi, i: (bi, i, 0)),
            pl.BlockSpec((1, qb, w), lambda bi, i: (bi, i, 1)),
            pl.BlockSpec((1, qb, w), lambda bi, i: (bi, i, 2)),
            pl.BlockSpec((1, qb, w), lambda bi, i: (bi, prev(bi, i), 1)),
            pl.BlockSpec((1, qb, w), lambda bi, i: (bi, prev(bi, i), 2)),
            pl.BlockSpec((1,) + bias.shape[1:], lambda bi, i: (jnp.minimum(i, 1), 0, 0, 0, 0)),
        ],
        out_specs=pl.BlockSpec((1, qb, w), lambda bi, i: (bi, i, 0)),
        out_shape=jax.ShapeDtypeStruct((b, l, w), BF16),
        compiler_params=_params("parallel", "arbitrary"),
        name="band_prompt",
    )(pd, pd, pd, pd, pd, bias)


def _band_sample_kernel(qkv_ref, kt_ref, vt_ref, bc_ref, bn_ref, o_ref, *, hd):
    w = H_D * hd
    x = qkv_ref[0]
    q, kn, vn = x[:, :w] * (hd ** -0.5), x[:, w:2 * w], x[:, 2 * w:]
    heads = [slice(h * hd, (h + 1) * hd) for h in range(H_D)]
    sc = [_mm(q[:, sl], kt_ref[h]) + bc_ref[h] for h, sl in enumerate(heads)]
    sn = [_mm_nt(q[:, sl], kn[:, sl]) + bn_ref[h] for h, sl in enumerate(heads)]
    m = [jnp.maximum(jnp.max(c, axis=-1, keepdims=True), jnp.max(n, axis=-1, keepdims=True)) for c, n in zip(sc, sn)]
    pc = [jnp.exp(c - z) for c, z in zip(sc, m)]
    pn = [jnp.exp(n - z) for n, z in zip(sn, m)]
    den = [jnp.sum(c, axis=-1, keepdims=True) + jnp.sum(n, axis=-1, keepdims=True) for c, n in zip(pc, pn)]
    outs = [(_mm_nt(c, vt_ref[h]) + _mm(n, vn[:, sl])) / d
            for h, (sl, c, n, d) in enumerate(zip(heads, pc, pn, den))]
    o_ref[0] = jnp.concatenate(outs, axis=-1).astype(o_ref.dtype)


def _band_sample(pd, kt_cache, vt_cache, layer, bias_c, bias_n):
    b, l, w3 = pd.shape
    w = w3 // 3
    hd = w // H_D
    rows = kt_cache.shape[-1]
    cache_spec = pl.BlockSpec((None, None, H_D, hd, rows), lambda bi: (layer, bi, 0, 0, 0))
    return pl.pallas_call(
        functools.partial(_band_sample_kernel, hd=hd),
        grid=(b,),
        in_specs=[
            pl.BlockSpec((1, l, w3), lambda bi: (bi, 0, 0)),
            cache_spec, cache_spec,
            pl.BlockSpec((H_D, l, rows), lambda bi: (0, 0, 0)),
            pl.BlockSpec((H_D, l, l), lambda bi: (0, 0, 0)),
        ],
        out_specs=pl.BlockSpec((1, l, w), lambda bi: (bi, 0, 0)),
        out_shape=jax.ShapeDtypeStruct((b, l, w), BF16),
        compiler_params=_params("parallel"),
        name="band_sample",
    )(pd, kt_cache, vt_cache, bias_c, bias_n)


def _row_parts(rows, parts):
    if rows % (parts * 2 * SUBLANES):
        parts = 1
    step = rows // parts
    return [slice(n * step, (n + 1) * step) for n in range(parts)]


def _mix_xattn_kernel(x_ref, oa_ref, ob_ref, oc_ref, od_ref, wm_ref, mk_ref, mv_ref, wq_ref, wo_ref, g_ref, b_ref,
                      o_ref, *, alpha, hd, head_axis):
    nb, tm, d = x_ref.shape
    dg = oa_ref.shape[2]
    heads = [slice(h * hd, (h + 1) * hd) for h in range(H_X)]
    if head_axis:
        n_mem = mk_ref.shape[1]
        mk = [mk_ref[n].reshape(n_mem * H_X, hd).astype(BF16) for n in range(nb)]
        mv = [mv_ref[n].reshape(n_mem * H_X, hd).astype(BF16) for n in range(nb)]
    else:
        mk = [[mk_ref[n, :, hs].astype(BF16) for hs in heads] for n in range(nb)]
        mv = [[mv_ref[n, :, hs].astype(BF16) for hs in heads] for n in range(nb)]
    if nb == 1:
        parts = _row_parts(tm, 2)
        rows_of = lambda ref, sl: ref[0, sl, :]
        units = [(n, slice(0, sl.stop - sl.start), 0) for n, sl in enumerate(parts)]
    else:
        parts = [slice(0, nb * tm)]
        rows_of = lambda ref, sl: ref[...].reshape(nb * tm, ref.shape[2])
        units = [(0, slice(n * tm, (n + 1) * tm), n) for n in range(nb)]
    mix = [sum(jnp.dot(rows_of(r, sl), wm_ref[n * dg:(n + 1) * dg, :], preferred_element_type=F32)
               for n, r in enumerate((oa_ref, ob_ref, oc_ref, od_ref))) for sl in parts]
    xs = [_layer_norm(alpha * rows_of(x_ref, sl) + z, g_ref[0:1, :], b_ref[0:1, :]) for sl, z in zip(parts, mix)]
    q = [(_mm(x, wq_ref[...]) * (hd ** -0.5)).astype(BF16) for x in xs]
    if head_axis:
        rows_u = units[0][1].stop - units[0][1].start
        row = lax.broadcasted_iota(jnp.int32, (H_X * rows_u, n_mem * H_X), 0)
        r_head = sum((row >= h * rows_u).astype(jnp.int32) for h in range(1, H_X))
        c_head = lax.broadcasted_iota(jnp.int32, (H_X * rows_u, n_mem * H_X), 1) & (H_X - 1)
        own_head = jnp.where(r_head == c_head, 0.0, -jnp.inf)
        qs = [jnp.concatenate([q[p][rows, hs] for hs in heads], axis=0) for p, rows, _ in units]
        s = [_mm_nt(z, mk[n]) + own_head for z, (_, _, n) in zip(qs, units)]
        e = [jnp.exp(z - jnp.max(z, axis=-1, keepdims=True)) for z in s]
        pr = [z * (1.0 / jnp.sum(z, axis=-1, keepdims=True)) for z in e]
        pv = [_mm(z, mv[n]) for z, (_, _, n) in zip(pr, units)]
        pv = [jnp.concatenate([z[h * rows_u:(h + 1) * rows_u] for h in range(H_X)], axis=-1) for z in pv]
    else:
        s = [[_mm_nt(q[p][rows, hs], kh) for hs, kh in zip(heads, mk[n])] for p, rows, n in units]
        e = [[jnp.exp(z - jnp.max(z, axis=-1, keepdims=True)) for z in su] for su in s]
        pr = [[z * (1.0 / jnp.sum(z, axis=-1, keepdims=True)) for z in eu] for eu in e]
        pv = [jnp.concatenate([_mm(z, vh) for z, vh in zip(pu, mv[n])], axis=-1) for pu, (_, _, n) in zip(pr, units)]
    pv = [jnp.concatenate([z for z, (p, _, _) in zip(pv, units) if p == n], axis=0) for n in range(len(parts))]
    att = [_mm(z, wo_ref[...]) for z in pv]
    out = [_layer_norm(alpha * x + z, g_ref[1:2, :], b_ref[1:2, :]) for x, z in zip(xs, att)]
    if nb == 1:
        for sl, z in zip(parts, out):
            o_ref[0, sl, :] = z
    else:
        o_ref[...] = out[0].reshape(nb, tm, d)


def _mix_xattn(x, mixed, w_out, mk, mv, wq, wo, layer, g, b, *, alpha, rows=512):
    bsz, l, d = x.shape
    dg = mixed[0].shape[2]
    hd = d // H_X
    tm = _tile(l, rows)
    nb = max(1, min(bsz, XATTN_SEQS_PER_STEP, rows // l)) if tm == l else 1
    while bsz % nb:
        nb -= 1
    full = lambda shape: pl.BlockSpec(shape, lambda bi, i: (0,) * len(shape))
    row = lambda width: pl.BlockSpec((nb, tm, width), lambda bi, i: (bi, i, 0))
    if mk.ndim == 3:
        mem_spec = pl.BlockSpec((nb,) + mk.shape[1:], lambda bi, i: (bi, 0, 0))
    else:
        mem_spec = pl.BlockSpec((None, nb) + mk.shape[2:], lambda bi, i: (layer, bi, 0, 0, 0))
    return pl.pallas_call(
        functools.partial(_mix_xattn_kernel, alpha=alpha, hd=hd, head_axis=mk.ndim != 3),
        grid=(bsz // nb, l // tm),
        in_specs=[
            row(d), row(dg), row(dg), row(dg), row(dg), _layer_spec(w_out, layer),
            mem_spec, mem_spec,
            _layer_spec(wq, layer), _layer_spec(wo, layer), full((2, d)), full((2, d)),
        ],
        out_specs=row(d),
        out_shape=jax.ShapeDtypeStruct((bsz, l, d), F32),
        compiler_params=_params("parallel", "parallel"),
        name="mix_xattn_ln",
    )(x, *mixed, w_out, mk, mv, wq, wo, g, b)


def _mlp_kernel(x_ref, w1_ref, w2_ref, g_ref, b_ref, o_ref, xb_scr, *, alpha, parts):
    f = pl.program_id(1)

    @pl.when(f == 0)
    def _():
        xb_scr[...] = x_ref[...].astype(BF16)
        o_ref[...] = jnp.zeros_like(o_ref)

    sl = _row_parts(x_ref.shape[0], parts)
    hid = [jnp.maximum(jnp.dot(xb_scr[s, :], w1_ref[...], preferred_element_type=F32), 0.0) for s in sl]
    act = [(z * z).astype(BF16) for z in hid]
    for s, z in zip(sl, act):
        o_ref[s, :] += jnp.dot(z, w2_ref[...], preferred_element_type=F32)

    @pl.when(f == pl.num_programs(1) - 1)
    def _():
        o_ref[...] = _layer_norm(alpha * x_ref[...] + o_ref[...], g_ref[...], b_ref[...])


def _mlp(x, w1, w2, layer, g, b, *, alpha, tm=1024, tf=1024):
    m, d = x.shape
    dff = w1.shape[2]
    tm = _tile(m, tm)
    tf = _tile(dff, tf)
    return pl.pallas_call(
        functools.partial(_mlp_kernel, alpha=alpha, parts=2),
        grid=(m // tm, dff // tf),
        in_specs=[
            pl.BlockSpec((tm, d), lambda i, f: (i, 0)),
            _layer_spec(w1, layer, (d, tf), lambda i, f: (0, f)),
            _layer_spec(w2, layer, (tf, d), lambda i, f: (f, 0)),
            pl.BlockSpec((1, d), lambda i, f: (0, 0)),
            pl.BlockSpec((1, d), lambda i, f: (0, 0)),
        ],
        out_specs=pl.BlockSpec((tm, d), lambda i, f: (i, 0)),
        out_shape=jax.ShapeDtypeStruct((m, d), F32),
        scratch_shapes=[pltpu.VMEM((tm, d), BF16)],
        compiler_params=_params("parallel", "arbitrary"),
        name="mlp_ln",
    )(x, w1, w2, g, b)


def _block_diag(blocks):
    g, r, c = blocks.shape
    eye = jnp.eye(g, dtype=blocks.dtype)
    return (eye[:, None, :, None] * blocks[:, :, None, :]).reshape(g * r, g * c)


def _pad_lanes(v):
    return jnp.pad(v, (0, LANES - v.shape[0]))[None, :]


def _s5_params(lam_re, lam_im, log_dt, b_re, b_im, c_re, c_im):
    dt = jnp.exp(log_dt)[:, None]
    mag = jnp.exp(lam_re * dt)
    ar, ai = mag * jnp.cos(lam_im * dt), mag * jnp.sin(lam_im * dt)
    den = lam_re * lam_re + lam_im * lam_im
    fr = ((ar - 1.0) * lam_re + ai * lam_im) / den
    fi = (ai * lam_re - (ar - 1.0) * lam_im) / den
    bbr = fr[..., None] * b_re - fi[..., None] * b_im
    bbi = fr[..., None] * b_im + fi[..., None] * b_re
    wb = jnp.concatenate([_block_diag(jnp.swapaxes(bbr, 1, 2)), _block_diag(jnp.swapaxes(bbi, 1, 2))], axis=1)
    wc = jnp.concatenate([_block_diag(jnp.swapaxes(c_re, 1, 2)), -_block_diag(jnp.swapaxes(c_im, 1, 2))], axis=0)
    ar, ai = ar.reshape(-1), ai.reshape(-1)
    pows = [(ar, ai)]
    for _ in range(SUBLANES - 1):
        pr, pi = pows[-1]
        pows.append((pr * ar - pi * ai, pr * ai + pi * ar))
    cat = lambda idx: jnp.stack([jnp.concatenate(pows[n]) for n in idx], axis=0)
    row = jnp.arange(SUBLANES)[:, None]
    shift_pows = jnp.stack([jnp.where(row >= s, cat((s - 1,)), 0.0) for s in (2, 4)])
    wbr, wbi = wb[:, :ar.shape[0]], wb[:, ar.shape[0]:]
    wb_lag = jnp.concatenate([ar * wbr - ai * wbi, ar * wbi + ai * wbr], axis=1)
    wb2 = jnp.concatenate([wb, wb_lag], axis=0)
    return wb2.astype(BF16), shift_pows, cat(range(SUBLANES)), wc.astype(BF16)


def _rel_bias_table(table, n_rows, n_cols, offset):
    tab = table.astype(F32).T
    rel_min, rel_max = offset - (n_cols - 1), offset + n_rows - 1
    lo, hi = max(rel_min, -REL_CLIP), min(rel_max, REL_CLIP)
    parts = [jnp.repeat(tab[:, :1], lo - rel_min, axis=1), tab[:, lo + REL_CLIP:hi + REL_CLIP + 1],
             jnp.repeat(tab[:, -1:], rel_max - hi, axis=1)]
    ext = jnp.concatenate(parts, axis=1)
    length = n_rows + n_cols - 1
    flipped = jnp.pad(ext[:, ::-1], ((0, 0), (0, 1)))
    shifted = jnp.tile(flipped, (1, n_rows))[:, :n_rows * length].reshape(-1, n_rows, length)
    return shifted[:, :, n_rows - 1:n_rows - 1 + n_cols]


def _band_prompt_bias(table, qb, qp):
    span = N_PREV * CHUNK
    width = span + qp
    bias = _rel_bias_table(table, qp, width, span)
    r = jnp.arange(qp)[:, None] // CHUNK
    j = jnp.arange(width)[None, :]
    in_band = (j // CHUNK >= r) & (j // CHUNK <= r + N_PREV)
    regular = jnp.where(in_band, bias, -jnp.inf)
    first = jnp.stack([jnp.where(in_band & (j >= span - p * qp), bias, -jnp.inf) for p in range(qb // qp)])
    return jnp.stack([first, jnp.broadcast_to(regular, first.shape)])


def _trunk_layer(x, mem_k, mem_v, gdn_conv, gdn_s, s5_h, rg_conv, rg_h, band_k, band_v, p, *, alpha):
    b, l, d = x.shape
    dg = d // N_MIX
    m = b * l
    layer = p["layer"]
    pa, pb, pc, pd, pdb = _inproj(x.reshape(m, d), p["w_in"], layer, (4 * dg, dg, 2 * dg, 3 * dg, LANES))
    pa, pb, pc, pd, pdb = [t.reshape(b, l, -1) for t in (pa, pb, pc, pd, pdb)]

    pad8 = lambda buf: jnp.pad(buf, ((0, 0), (SUBLANES - (CONV_W - 1), 0), (0, 0)))
    chunk = CHUNK if l % CHUNK == 0 else l
    cps = max(1, min(GDN_CHUNKS_PER_STEP, l // chunk))
    o_a, gdn_s_new = _gdn(pa, pdb, pad8(gdn_conv), gdn_s, p["gdn_conv_w"], p["gdn_conv_b"], p["gdn_a_log"],
                          p["gdn_dt_bias"], p["gdn_norm_g"], t=chunk, cps=cps)
    gdn_conv_new = pa[:, l - (CONV_W - 1):, :3 * dg]

    ns = p["s5_wb"].shape[1] // 2
    h0 = jnp.concatenate([s5_h[..., 0].reshape(b, 1, ns), s5_h[..., 1].reshape(b, 1, ns)], axis=-1)
    o_b, h_last = _s5(pb, h0, p["s5_wb"], p["s5_ap"], p["s5_p8"], p["s5_wc"], p["s5_d"], p["s5_w_glu"],
                      p["s5_b_glu"], t=_tile(l, 256))
    s5_h_new = jnp.stack([h_last[:, 0, :ns].reshape(s5_h.shape[:-1]), h_last[:, 0, ns:].reshape(s5_h.shape[:-1])],
                         axis=-1)

    o_c, rg_last = _rglru(pc, pad8(rg_conv), rg_h[:, None, :], p["rg_conv_w"], p["rg_conv_b"], p["rg_wri"],
                          p["rg_bri"], p["rg_lam"], t=_tile(l, 256))
    rg_conv_new = pc[:, l - (CONV_W - 1):, :dg]
    rg_h_new = rg_last[:, 0, :]

    hd = dg // H_D
    if band_k is None:
        qb = N_PREV * CHUNK
        o_d = _band_prompt(pd, p["band_bias_prompt"], qb=qb, qp=BAND_PIECE)
        keep = min(N_PREV * CHUNK, l)
    else:
        o_d = _band_sample(pd, band_k, band_v, layer, p["band_bias_cache"], p["band_bias_new"])
        keep = l
    band_k_new = pd[:, l - keep:, dg:2 * dg].reshape(b, keep, H_D, hd)
    band_v_new = pd[:, l - keep:, 2 * dg:].reshape(b, keep, H_D, hd)

    x3 = _mix_xattn(x, (o_a, o_b, o_c, o_d), p["w_out"], mem_k, mem_v, p["xa_w_q"], p["xa_w_o"], layer,
                    p["ln_g"][0:2], p["ln_b"][0:2], alpha=alpha)
    x4 = _mlp(x3.reshape(m, d), p["mlp_w1"], p["mlp_w2"], layer, p["ln_g"][2:3], p["ln_b"][2:3], alpha=alpha)
    return x4.reshape(b, l, d), (gdn_conv_new, gdn_s_new, s5_h_new, rg_conv_new, rg_h_new, band_k_new, band_v_new)


def kernel(x_prompt, x_sample, state_gdn_conv, state_gdn, state_s5, state_rglru_conv, state_rglru, cache_band_k, cache_band_v, cache_mem_k, cache_mem_v, mem_prompt, w_in, w_out, ln_g, ln_b, gdn_conv_w, gdn_conv_b, gdn_a_log, gdn_dt_bias, gdn_norm_g, s5_lam_re, s5_lam_im, s5_log_dt, s5_b_re, s5_b_im, s5_c_re, s5_c_im, s5_d, s5_w_glu, s5_b_glu, rg_conv_w, rg_conv_b, rg_w_r, rg_b_r, rg_w_i, rg_b_i, rg_lam, band_rel_bias, xa_w_q, xa_w_k, xa_w_v, xa_w_o, mlp_w1, mlp_w2):
    depth = w_in.shape[0]
    bp, lp, d = x_prompt.shape
    bs, ls, _ = x_sample.shape
    n_mem = mem_prompt.shape[1]
    dg = d // N_MIX
    hd_x = d // H_X
    alpha = (2.0 * depth) ** 0.25
    band_rows = cache_band_k.shape[2]

    sizes = (3 * dg, dg, H_A, H_A, dg, dg, dg, 3 * dg)
    offs = [0]
    for s in sizes:
        offs.append(offs[-1] + s)

    w_db = jnp.pad(w_in[:, :, offs[2]:offs[4]], ((0, 0), (0, 0), (0, LANES - 2 * H_A)))
    w_in_bf = jnp.concatenate([w_in[:, :, offs[0]:offs[2]], w_in[:, :, offs[4]:offs[5]], w_in[:, :, offs[5]:offs[7]],
                               w_in[:, :, offs[7]:offs[8]], w_db], axis=2).astype(BF16)
    w_out_bf, xa_w_q_bf, xa_w_k_bf, xa_w_v_bf, xa_w_o_bf, mlp_w1_bf, mlp_w2_bf = [
        w.astype(BF16) for w in (w_out, xa_w_q, xa_w_k, xa_w_v, xa_w_o, mlp_w1, mlp_w2)]

    band_kt = jnp.transpose(cache_band_k, (0, 1, 3, 4, 2))
    band_vt = jnp.transpose(cache_band_v, (0, 1, 3, 4, 2))

    xp, xs = x_prompt, x_sample
    p_states, s_states = [], []
    for l in range(depth):
        s5_wb, s5_ap, s5_p8, s5_wc = _s5_params(s5_lam_re[l], s5_lam_im[l], s5_log_dt[l], s5_b_re[l], s5_b_im[l],
                                                s5_c_re[l], s5_c_im[l])
        p = {
            "layer": l, "w_in": w_in_bf, "w_out": w_out_bf, "ln_g": ln_g[l], "ln_b": ln_b[l],
            "gdn_conv_w": gdn_conv_w[l], "gdn_conv_b": gdn_conv_b[l][None, :],
            "gdn_a_log": _pad_lanes(gdn_a_log[l]), "gdn_dt_bias": _pad_lanes(gdn_dt_bias[l]),
            "gdn_norm_g": gdn_norm_g[l][None, :],
            "s5_wb": s5_wb, "s5_ap": s5_ap, "s5_p8": s5_p8, "s5_wc": s5_wc,
            "s5_d": s5_d[l].reshape(1, dg), "s5_w_glu": s5_w_glu[l].astype(BF16), "s5_b_glu": s5_b_glu[l][None, :],
            "rg_conv_w": rg_conv_w[l], "rg_conv_b": rg_conv_b[l][None, :],
            "rg_wri": jnp.concatenate([_block_diag(rg_w_r[l]), _block_diag(rg_w_i[l])], axis=1).astype(BF16),
            "rg_bri": jnp.concatenate([rg_b_r[l], rg_b_i[l]])[None, :], "rg_lam": rg_lam[l][None, :],
            "band_bias_prompt": _band_prompt_bias(band_rel_bias[l], N_PREV * CHUNK, BAND_PIECE),
            "band_bias_cache": _rel_bias_table(band_rel_bias[l], ls, band_rows, band_rows),
            "band_bias_new": _rel_bias_table(band_rel_bias[l], ls, ls, 0),
            "xa_w_q": xa_w_q_bf, "xa_w_o": xa_w_o_bf, "mlp_w1": mlp_w1_bf, "mlp_w2": mlp_w2_bf,
        }
        mem2 = mem_prompt.reshape(bp * n_mem, d)
        mk = _matmul(mem2, xa_w_k_bf, l).reshape(bp, n_mem, d)
        mv = _matmul(mem2, xa_w_v_bf, l).reshape(bp, n_mem, d)
        xp, st_p = _trunk_layer(
            xp, mk, mv,
            jnp.zeros((bp, CONV_W - 1, 3 * dg), F32), jnp.zeros((bp, H_A, dg // H_A, dg // H_A), F32),
            jnp.zeros((bp, dg // S5_CH, P_B, 2), F32), jnp.zeros((bp, CONV_W - 1, dg), F32),
            jnp.zeros((bp, dg), F32), None, None, p, alpha=alpha)
        p_states.append(st_p + (mk.reshape(bp, n_mem, H_X, hd_x), mv.reshape(bp, n_mem, H_X, hd_x)))
        xs, st_s = _trunk_layer(
            xs, cache_mem_k, cache_mem_v,
            state_gdn_conv[l], state_gdn[l], state_s5[l], state_rglru_conv[l], state_rglru[l],
            band_kt, band_vt, p, alpha=alpha)
        s_states.append(st_s)

    def stk(states, i):
        return jnp.stack([st[i] for st in states], axis=0)

    return (xp, xs,
            stk(p_states, 0), stk(p_states, 1), stk(p_states, 2), stk(p_states, 3), stk(p_states, 4),
            stk(p_states, 5), stk(p_states, 6), stk(p_states, 7), stk(p_states, 8),
            stk(s_states, 0), stk(s_states, 1), stk(s_states, 2), stk(s_states, 3), stk(s_states, 4),
            stk(s_states, 5), stk(s_states, 6))
```

```python
import functools
import math

import jax
import jax.numpy as jnp
from jax import lax
from jax.experimental import pallas as pl
from jax.experimental.pallas import tpu as pltpu

F32 = jnp.float32
BF16 = jnp.bfloat16
HIGHEST = lax.Precision.HIGHEST

N_MIX = 4
CONV_W = 4
CHUNK = 64
H_A = 4
S5_CH = 16
P_B = 64
H_C = 4
RG_C = 8.0
H_D = 4
N_PREV = 8
REL_CLIP = 128
H_X = 4
LN_EPS = 1e-5
NORM_EPS = 1e-6

LANES = 128
SUBLANES = 8
VMEM_LIMIT_BYTES = 56 * 1024 * 1024
GDN_BLOCK = 128
GDN_CHUNKS_PER_STEP = 8
BAND_PIECE = 256
SCAN_ROWS_PER_STEP = 256
XATTN_SEQS_PER_STEP = 4


def _params(*semantics):
    return pltpu.CompilerParams(dimension_semantics=semantics, vmem_limit_bytes=VMEM_LIMIT_BYTES)


def _tile(n, pref):
    t = min(n, pref)
    while n % t:
        t -= SUBLANES
    return t


def _seqs_per_step(b, l, t):
    nb = max(1, min(b, SCAN_ROWS_PER_STEP // t)) if l == t else 1
    while b % nb:
        nb -= 1
    return nb


def _mm(a, b):
    return jnp.dot(a.astype(BF16), b.astype(BF16), preferred_element_type=F32)


def _mm_nt(a, b):
    return lax.dot_general(a.astype(BF16), b.astype(BF16), (((1,), (1,)), ((), ())),
                           preferred_element_type=F32)


def _mm_tn(a, b):
    return lax.dot_general(a.astype(BF16), b.astype(BF16), (((0,), (0,)), ((), ())),
                           preferred_element_type=F32)


def _mm_f32(a, b):
    return jnp.dot(a, b, precision=HIGHEST, preferred_element_type=F32)


def _sigmoid(x):
    return 1.0 / (1.0 + jnp.exp(-x))


def _softplus(x):
    return jnp.maximum(x, 0.0) + jnp.log1p(jnp.exp(-jnp.abs(x)))


def _gelu_tanh(x):
    c = math.sqrt(2.0 / math.pi)
    return 0.5 * x * (1.0 + jnp.tanh(c * (x + 0.044715 * (x * x * x))))


def _layer_norm(z, g, b):
    mu = jnp.mean(z, axis=-1, keepdims=True)
    zc = z - mu
    var = jnp.mean(zc * zc, axis=-1, keepdims=True)
    return zc * lax.rsqrt(var + LN_EPS) * g + b


def _matmul_kernel(x_ref, w_ref, o_ref):
    o_ref[...] = _mm(x_ref[...], w_ref[...])


def _layer_spec(w, layer, block=None, index=None):
    block = tuple(w.shape[1:]) if block is None else block
    index = (lambda *_: (0,) * len(block)) if index is None else index
    return pl.BlockSpec((None,) + block, lambda *g: (layer,) + tuple(index(*g)))


def _matmul(x, w_bf16, layer, tm=512):
    m, k = x.shape
    n = w_bf16.shape[2]
    tm = _tile(m, tm)
    return pl.pallas_call(
        _matmul_kernel,
        grid=(m // tm,),
        in_specs=[pl.BlockSpec((tm, k), lambda i: (i, 0)), _layer_spec(w_bf16, layer)],
        out_specs=pl.BlockSpec((tm, n), lambda i: (i, 0)),
        out_shape=jax.ShapeDtypeStruct((m, n), F32),
        compiler_params=_params("parallel"),
        name="matmul",
    )(x, w_bf16)


def _inproj_kernel(x_ref, w_ref, *o_refs, bounds):
    xb = x_ref[...].astype(BF16)
    for o_ref, (s, e) in zip(o_refs, bounds):
        o_ref[...] = jnp.dot(xb, w_ref[:, s:e], preferred_element_type=F32)


def _inproj(x, w_bf16, layer, widths, tm=512):
    m, k = x.shape
    tm = _tile(m, tm)
    bounds, s = [], 0
    for w in widths:
        bounds.append((s, s + w))
        s += w
    return pl.pallas_call(
        functools.partial(_inproj_kernel, bounds=tuple(bounds)),
        grid=(m // tm,),
        in_specs=[pl.BlockSpec((tm, k), lambda i: (i, 0)), _layer_spec(w_bf16, layer)],
        out_specs=[pl.BlockSpec((tm, w), lambda i: (i, 0)) for w in widths],
        out_shape=[jax.ShapeDtypeStruct((m, w), F32) for w in widths],
        compiler_params=_params("parallel"),
        name="inproj",
    )(x, w_bf16)


def _inverse_masks(r, c, t):
    neg_diag8 = jnp.where((r >> 3) == (c >> 3), -1.0, 0.0)
    offs, lb = [], 3
    while (1 << lb) < t:
        off = ((r >> (lb + 1)) == (c >> (lb + 1))) & (((r >> lb) & 1) == 1) & (((c >> lb) & 1) == 0)
        offs.append(jnp.where(off, 1.0, 0.0))
        lb += 1
    return neg_diag8, offs


def _unit_lower_inverse_offdiag(a_list, masks):
    neg_diag8, offs = masks
    n1 = [a * neg_diag8 for a in a_list]
    n2 = [_mm(x, x) for x in n1]
    n3 = [_mm(x, x2) for x, x2 in zip(n1, n2)]
    n4 = [_mm(x2, x2) for x2 in n2]
    p = [x + x2 + x3 for x, x2, x3 in zip(n1, n2, n3)]
    pn4 = [_mm(pp, x4) for pp, x4 in zip(p, n4)]
    y = [pp + x4 + px for pp, x4, px in zip(p, n4, pn4)]
    for off in offs:
        m = [a * off for a in a_list]
        z = [mm + _mm(yy, mm) for yy, mm in zip(y, m)]
        zy = [_mm(zz, yy) for zz, yy in zip(z, y)]
        y = [yy - (zz + zzy) for yy, zz, zzy in zip(y, z, zy)]
    return y


def _split3(x):
    h1 = x.astype(BF16)
    r1 = x - h1.astype(F32)
    h2 = r1.astype(BF16)
    h3 = (r1 - h2.astype(F32)).astype(BF16)
    return h1, h2, h3


def _gdn_kernel(qkv_ref, gate_ref, db_ref, cbuf_ref, s0_ref, cw_ref, cb_ref, alog_ref, dtb_ref, ng_ref,
                o_ref, sfin_ref, xp_scr, s_scr, o_scr, *, t, cps, dk):
    i = pl.program_id(1)
    tb = t * cps
    nh = H_A
    dq = nh * dk

    @pl.when(i == 0)
    def _():
        xp_scr[0:SUBLANES, :] = cbuf_ref[0]
        s_scr[...] = s0_ref[0]

    x = qkv_ref[0]
    xp_scr[SUBLANES:SUBLANES + tb, :] = x
    base = SUBLANES - (CONV_W - 1)
    y = xp_scr[base:base + tb, :] * cw_ref[0:1, :]
    for j in range(1, CONV_W):
        y = y + xp_scr[base + j:base + j + tb, :] * cw_ref[j:j + 1, :]
    y = y + cb_ref[...]
    xp_scr[0:SUBLANES, :] = x[tb - SUBLANES:tb, :]
    y = y * _sigmoid(y)

    db = db_ref[0]
    log_a = -jnp.exp(alog_ref[...]) * _softplus(db + dtb_ref[...])
    beta_all = _sigmoid(db)

    bs = min(tb, GDN_BLOCK)
    lt = t.bit_length() - 1
    r = lax.broadcasted_iota(jnp.int32, (bs, bs), 0)
    c = lax.broadcasted_iota(jnp.int32, (bs, bs), 1)
    same = (r >> lt) == (c >> lt)
    causal_neg = jnp.where(same & (r >= c), 0.0, -jnp.inf)
    strict_f = jnp.where(same & (r > c), 1.0, 0.0)
    tril = jnp.where(same & (r >= c), 1.0, 0.0).astype(BF16)
    striu = jnp.where(same & (r < c), 1.0, 0.0).astype(BF16)
    inv_masks = _inverse_masks(r, c, t)

    blocks = list(range(0, tb, bs))
    pairs = [(bi, h) for bi in range(len(blocks)) for h in range(nh)]
    la3 = [_split3(log_a[b0:b0 + bs, :]) for b0 in blocks]
    gc = [sum(jnp.dot(tril, part, preferred_element_type=F32) for part in parts) for parts in la3]
    rv = [sum(jnp.dot(striu, part, preferred_element_type=F32) for part in parts) for parts in la3]
    eg = [jnp.exp(g) for g in gc]
    erv = [jnp.exp(g) for g in rv]
    e_tot = [jnp.exp(g + g2) for g, g2 in zip(gc, rv)]
    gc_rows = [g.T for g in gc]

    def head_cols(z, bi, h, off):
        b0 = blocks[bi]
        return z[b0:b0 + bs, off + h * dk:off + (h + 1) * dk]

    q = [head_cols(y, bi, h, 0) for bi, h in pairs]
    k = [head_cols(y, bi, h, dq) for bi, h in pairs]
    v = [head_cols(y, bi, h, 2 * dq) for bi, h in pairs]
    q = [z * lax.rsqrt(jnp.sum(z * z, axis=-1, keepdims=True) + NORM_EPS) * (dk ** -0.5) for z in q]
    k = [z * lax.rsqrt(jnp.sum(z * z, axis=-1, keepdims=True) + NORM_EPS) for z in k]
    decay = [jnp.exp(gc[bi][:, h:h + 1] - gc_rows[bi][h:h + 1, :] + causal_neg) for bi, h in pairs]
    beta = [beta_all[blocks[bi]:blocks[bi] + bs, nh + h:nh + h + 1] for bi, h in pairs]
    eg_col = [eg[bi][:, h:h + 1] for bi, h in pairs]
    kk = [_mm_nt(z, z) for z in k]
    qk = [_mm_nt(zq, zk) for zq, zk in zip(q, k)]
    a = [(b * z * d) * strict_f for b, z, d in zip(beta, kk, decay)]
    qk = [z * d for z, d in zip(qk, decay)]
    y_inv = _unit_lower_inverse_offdiag(a, inv_masks)
    rhs = [jnp.concatenate([zv * b, zk * (b * e)], axis=-1) for zv, zk, b, e in zip(v, k, beta, eg_col)]
    sol = [z + _mm(yi, z) for yi, z in zip(y_inv, rhs)]
    q_dec = [z * e for z, e in zip(q, eg_col)]
    k_dec = [z * erv[bi][:, h:h + 1] for z, (bi, h) in zip(k, pairs)]

    chunks = [(bi, r0) for bi in range(len(blocks)) for r0 in range(0, bs, t)]
    n_kw = [[_mm_tn(k_dec[bi * nh + h][r0:r0 + t], sol[bi * nh + h][r0:r0 + t]) for h in range(nh)]
            for bi, r0 in chunks]
    states = [s_scr[h] for h in range(nh)]
    starts = []
    for (bi, r0), nk in zip(chunks, n_kw):
        starts.append(states)
        drop = [_mm(z[:, dk:], s) for z, s in zip(nk, states)]
        states = [s * e_tot[bi][r0:r0 + 1, h:h + 1] + (z[:, :dk] - d)
                  for h, (s, z, d) in enumerate(zip(states, nk, drop))]
    ws = [[_mm(jnp.concatenate([sol[bi * nh + h][r0:r0 + t, dk:], q_dec[bi * nh + h][r0:r0 + t]], axis=0), st[h])
           for h in range(nh)] for (bi, r0), st in zip(chunks, starts)]
    v_news = [[] for _ in pairs]
    o_inter = [[] for _ in pairs]
    for (bi, r0), wc in zip(chunks, ws):
        for h, z in enumerate(wc):
            v_news[bi * nh + h].append(sol[bi * nh + h][r0:r0 + t, :dk] - z[:t])
            o_inter[bi * nh + h].append(z[t:])
    o = [jnp.concatenate(oi, axis=0) + _mm(z, jnp.concatenate(vn, axis=0))
         for oi, z, vn in zip(o_inter, qk, v_news)]
    o = [z * lax.rsqrt(jnp.mean(z * z, axis=-1, keepdims=True) + NORM_EPS) * ng_ref[...] for z in o]
    for (bi, h), z in zip(pairs, o):
        o_scr[blocks[bi]:blocks[bi] + bs, h * dk:(h + 1) * dk] = z
    for h in range(nh):
        s_scr[h] = states[h]

    g = gate_ref[0]
    o_ref[0] = (o_scr[...] * (g * _sigmoid(g))).astype(o_ref.dtype)

    @pl.when(i == pl.num_programs(1) - 1)
    def _():
        sfin_ref[0] = s_scr[...]


def _gdn(pa, pdb, cbuf8, s0, params, layer, *, t, cps):
    b, l, _ = pa.shape
    nh, dk = s0.shape[1], s0.shape[2]
    dq = nh * dk
    tb = t * cps
    return pl.pallas_call(
        functools.partial(_gdn_kernel, t=t, cps=cps, dk=dk),
        grid=(b, l // tb),
        in_specs=[
            pl.BlockSpec((1, tb, 3 * dq), lambda bi, i: (bi, i, 0)),
            pl.BlockSpec((1, tb, dq), lambda bi, i: (bi, i, 3)),
            pl.BlockSpec((1, tb, LANES), lambda bi, i: (bi, i, 0)),
            pl.BlockSpec((1, SUBLANES, 3 * dq), lambda bi, i: (bi, 0, 0)),
            pl.BlockSpec((1, nh, dk, dk), lambda bi, i: (bi, 0, 0, 0)),
        ] + [_layer_spec(w, layer) for w in params],
        out_specs=[
            pl.BlockSpec((1, tb, dq), lambda bi, i: (bi, i, 0)),
            pl.BlockSpec((1, nh, dk, dk), lambda bi, i: (bi, 0, 0, 0)),
        ],
        out_shape=[jax.ShapeDtypeStruct((b, l, dq), BF16), jax.ShapeDtypeStruct((b, nh, dk, dk), F32)],
        scratch_shapes=[pltpu.VMEM((SUBLANES + tb, 3 * dq), F32), pltpu.VMEM((nh, dk, dk), F32),
                        pltpu.VMEM((tb, dq), F32)],
        compiler_params=_params("parallel", "arbitrary"),
        name="gdn",
    )(pa, pa, pdb, cbuf8, s0, *params)


def _s5_kernel(u_ref, h0_ref, wb_ref, ap_ref, p8_ref, wc_ref, d_ref, wg_ref, bg_ref,
               o_ref, hl_ref, carry_scr, h_scr, *, t, ns):
    i = pl.program_id(1)
    nb, _, dg = u_ref.shape

    @pl.when(i == 0)
    def _():
        carry_scr[...] = h0_ref[...]

    u = u_ref[...].reshape(nb * t, dg)
    ng = nb * t // SUBLANES
    first = lax.broadcasted_iota(jnp.int32, (SUBLANES, dg), 0) == 0
    u_prev = jnp.where(first, 0.0, pltpu.roll(u.reshape(ng, SUBLANES, dg), 1, 1)).reshape(nb * t, dg)
    x = _mm(jnp.concatenate([u, u_prev], axis=-1), wb_ref[...])
    xr = x[:, :ns].reshape(ng, SUBLANES, ns)
    xi = x[:, ns:].reshape(ng, SUBLANES, ns)
    for lvl in range(2):
        s = 2 << lvl
        pr, pi = ap_ref[lvl, :, :ns], ap_ref[lvl, :, ns:]
        sr = pltpu.roll(xr, s, 1)
        si = pltpu.roll(xi, s, 1)
        xr, xi = xr + (pr * sr - pi * si), xi + (pr * si + pi * sr)
    p8r, p8i = p8_ref[:, :ns], p8_ref[:, ns:]
    for n in range(nb):
        cr, ci = carry_scr[n, :, :ns], carry_scr[n, :, ns:]
        for j in range(n * t // SUBLANES, (n + 1) * t // SUBLANES):
            sl = slice(j * SUBLANES, (j + 1) * SUBLANES)
            br = xr[j] + (p8r * cr - p8i * ci)
            bi = xi[j] + (p8r * ci + p8i * cr)
            h_scr[sl, :ns] = br
            h_scr[sl, ns:] = bi
            cr, ci = br[SUBLANES - 1:SUBLANES], bi[SUBLANES - 1:SUBLANES]
        carry_scr[n, :, :ns] = cr
        carry_scr[n, :, ns:] = ci
    hl_ref[...] = carry_scr[...]

    y = _mm(h_scr[...], wc_ref[...]) + d_ref[...] * u
    y = _gelu_tanh(y)
    z = _mm(y, wg_ref[...]) + bg_ref[...]
    o_ref[...] = (y * _sigmoid(z)).astype(o_ref.dtype).reshape(nb, t, dg)


def _s5(pb, h0, params, layer, *, t):
    b, l, dg = pb.shape
    ns2 = params[0].shape[2]
    ns = ns2 // 2
    nb = _seqs_per_step(b, l, t)
    return pl.pallas_call(
        functools.partial(_s5_kernel, t=t, ns=ns),
        grid=(b // nb, l // t),
        in_specs=[
            pl.BlockSpec((nb, t, dg), lambda bi, i: (bi, i, 0)),
            pl.BlockSpec((nb, 1, ns2), lambda bi, i: (bi, 0, 0)),
        ] + [_layer_spec(w, layer) for w in params],
        out_specs=[
            pl.BlockSpec((nb, t, dg), lambda bi, i: (bi, i, 0)),
            pl.BlockSpec((nb, 1, ns2), lambda bi, i: (bi, 0, 0)),
        ],
        out_shape=[jax.ShapeDtypeStruct((b, l, dg), BF16), jax.ShapeDtypeStruct((b, 1, ns2), F32)],
        scratch_shapes=[pltpu.VMEM((nb, 1, ns2), F32), pltpu.VMEM((nb * t, ns2), F32)],
        compiler_params=_params("parallel", "arbitrary"),
        name="s5",
    )(pb, h0, *params)


def _rglru_kernel(xg_ref, cbuf_ref, h0_ref, cw_ref, cb_ref, wri_ref, bri_ref, lam_ref,
                  o_ref, hl_ref, xp_scr, carry_scr, h_scr, *, t, dg):
    i = pl.program_id(1)
    nb = xg_ref.shape[0]

    @pl.when(i == 0)
    def _():
        xp_scr[:, 0:SUBLANES, :] = cbuf_ref[...]
        carry_scr[...] = h0_ref[...]

    base = SUBLANES - (CONV_W - 1)
    ys = []
    for n in range(nb):
        x = xg_ref[n, :, :dg]
        xp_scr[n, SUBLANES:SUBLANES + t, :] = x
        y = xp_scr[n, base:base + t, :] * cw_ref[0:1, :]
        for j in range(1, CONV_W):
            y = y + xp_scr[n, base + j:base + j + t, :] * cw_ref[j:j + 1, :]
        xp_scr[n, 0:SUBLANES, :] = x[t - SUBLANES:t, :]
        ys.append(y + cb_ref[...])
    y = jnp.concatenate(ys, axis=0)
    gb = xg_ref[...][:, :, dg:].reshape(nb * t, dg)

    ri = _mm(y, wri_ref[...]) + bri_ref[...]
    rg = _sigmoid(ri[:, :dg])
    ig = _sigmoid(ri[:, dg:])
    log_a = (-RG_C * rg) * _softplus(-lam_ref[...])
    a = jnp.exp(log_a)
    th = jnp.tanh(log_a)
    xin = jnp.sqrt(-2.0 * th / (1.0 - th)) * (ig * y)

    ng = nb * t // SUBLANES
    a = a.reshape(ng, SUBLANES, dg)
    xin = xin.reshape(ng, SUBLANES, dg)
    row = lax.broadcasted_iota(jnp.int32, (SUBLANES, dg), 0)
    for lvl in range(3):
        s = 1 << lvl
        keep = row >= s
        a_s = jnp.where(keep, pltpu.roll(a, s, 1), 1.0)
        x_s = jnp.where(keep, pltpu.roll(xin, s, 1), 0.0)
        xin = a * x_s + xin
        a = a * a_s
    for n in range(nb):
        cr = carry_scr[n]
        for j in range(n * t // SUBLANES, (n + 1) * t // SUBLANES):
            hb = xin[j] + a[j] * cr
            h_scr[j * SUBLANES:(j + 1) * SUBLANES, :] = hb
            cr = hb[SUBLANES - 1:SUBLANES]
        carry_scr[n] = cr
    hl_ref[...] = carry_scr[...]
    o_ref[...] = (h_scr[...] * _gelu_tanh(gb)).astype(o_ref.dtype).reshape(nb, t, dg)


def _rglru(pc, cbuf8, h0, params, layer, *, t):
    b, l, dg2 = pc.shape
    dg = dg2 // 2
    nb = _seqs_per_step(b, l, t)
    return pl.pallas_call(
        functools.partial(_rglru_kernel, t=t, dg=dg),
        grid=(b // nb, l // t),
        in_specs=[
            pl.BlockSpec((nb, t, dg2), lambda bi, i: (bi, i, 0)),
            pl.BlockSpec((nb, SUBLANES, dg), lambda bi, i: (bi, 0, 0)),
            pl.BlockSpec((nb, 1, dg), lambda bi, i: (bi, 0, 0)),
        ] + [_layer_spec(w, layer) for w in params],
        out_specs=[
            pl.BlockSpec((nb, t, dg), lambda bi, i: (bi, i, 0)),
            pl.BlockSpec((nb, 1, dg), lambda bi, i: (bi, 0, 0)),
        ],
        out_shape=[jax.ShapeDtypeStruct((b, l, dg), BF16), jax.ShapeDtypeStruct((b, 1, dg), F32)],
        scratch_shapes=[pltpu.VMEM((nb, SUBLANES + t, dg), F32), pltpu.VMEM((nb, 1, dg), F32),
                        pltpu.VMEM((nb * t, dg), F32)],
        compiler_params=_params("parallel", "arbitrary"),
        name="rglru",
    )(pc, cbuf8, h0, *params)


def _band_prompt_kernel(q_ref, kc_ref, vc_ref, kp_ref, vp_ref, bias_ref, o_ref, *, qb, qp, hd):
    span = N_PREV * CHUNK
    q = q_ref[0] * (hd ** -0.5)
    k = jnp.concatenate([kp_ref[0], kc_ref[0]], axis=0)
    v = jnp.concatenate([vp_ref[0], vc_ref[0]], axis=0)
    units = [(h, p) for h in range(H_D) for p in range(qb // qp)]
    col = lambda h: slice(h * hd, (h + 1) * hd)
    win = lambda p: slice(qb - span + p * qp, qb + (p + 1) * qp)
    s = [_mm_nt(q[p * qp:(p + 1) * qp, col(h)], k[win(p), col(h)]) + bias_ref[0, p, h] for h, p in units]
    e = [jnp.exp(z - jnp.max(z, axis=-1, keepdims=True)) for z in s]
    den = [jnp.sum(z, axis=-1, keepdims=True) for z in e]
    o = [_mm(z, v[win(p), col(h)]) / d for z, d, (h, p) in zip(e, den, units)]
    npc = qb // qp
    o_ref[0] = jnp.concatenate([jnp.concatenate(o[h * npc:(h + 1) * npc], axis=0) for h in range(H_D)],
                               axis=-1).astype(o_ref.dtype)


def _band_prompt(pd, bias, layer, *, qb, qp):
    b, l, w3 = pd.shape
    w = w3 // 3
    hd = w // H_D
    prev = lambda bi, i: jnp.maximum(i - 1, 0)
    return pl.pallas_call(
        functools.partial(_band_prompt_kernel, qb=qb, qp=qp, hd=hd),
        grid=(b, l // qb),
        in_specs=[
            pl.BlockSpec((1, qb, w), lambda bi, i: (bi, i, 0)),
            pl.BlockSpec((1, qb, w), lambda bi, i: (bi, i, 1)),
            pl.BlockSpec((1, qb, w), lambda bi, i: (bi, i, 2)),
            pl.BlockSpec((1, qb, w), lambda bi, i: (bi, prev(bi, i), 1)),
            pl.BlockSpec((1, qb, w), lambda bi, i: (bi, prev(bi, i), 2)),
            pl.BlockSpec((None, 1) + bias.shape[2:], lambda bi, i: (layer, jnp.minimum(i, 1), 0, 0, 0, 0)),
        ],
        out_specs=pl.BlockSpec((1, qb, w), lambda bi, i: (bi, i, 0)),
        out_shape=jax.ShapeDtypeStruct((b, l, w), BF16),
        compiler_params=_params("parallel", "arbitrary"),
        name="band_prompt",
    )(pd, pd, pd, pd, pd, bias)


def _band_sample_kernel(qkv_ref, kt_ref, vt_ref, bc_ref, bn_ref, o_ref, *, hd):
    w = H_D * hd
    x = qkv_ref[0]
    q, kn, vn = x[:, :w] * (hd ** -0.5), x[:, w:2 * w], x[:, 2 * w:]
    heads = [slice(h * hd, (h + 1) * hd) for h in range(H_D)]
    sc = [_mm(q[:, sl], kt_ref[h]) + bc_ref[h] for h, sl in enumerate(heads)]
    sn = [_mm_nt(q[:, sl], kn[:, sl]) + bn_ref[h] for h, sl in enumerate(heads)]
    m = [jnp.maximum(jnp.max(c, axis=-1, keepdims=True), jnp.max(n, axis=-1, keepdims=True)) for c, n in zip(sc, sn)]
    pc = [jnp.exp(c - z) for c, z in zip(sc, m)]
    pn = [jnp.exp(n - z) for n, z in zip(sn, m)]
    den = [jnp.sum(c, axis=-1, keepdims=True) + jnp.sum(n, axis=-1, keepdims=True) for c, n in zip(pc, pn)]
    outs = [(_mm_nt(c, vt_ref[h]) + _mm(n, vn[:, sl])) / d
            for h, (sl, c, n, d) in enumerate(zip(heads, pc, pn, den))]
    o_ref[0] = jnp.concatenate(outs, axis=-1).astype(o_ref.dtype)


def _band_sample(pd, kt_cache, vt_cache, layer, bias_c, bias_n):
    b, l, w3 = pd.shape
    w = w3 // 3
    hd = w // H_D
    rows = kt_cache.shape[-1]
    cache_spec = pl.BlockSpec((None, None, H_D, hd, rows), lambda bi: (layer, bi, 0, 0, 0))
    return pl.pallas_call(
        functools.partial(_band_sample_kernel, hd=hd),
        grid=(b,),
        in_specs=[
            pl.BlockSpec((1, l, w3), lambda bi: (bi, 0, 0)),
            cache_spec, cache_spec, _layer_spec(bias_c, layer), _layer_spec(bias_n, layer),
        ],
        out_specs=pl.BlockSpec((1, l, w), lambda bi: (bi, 0, 0)),
        out_shape=jax.ShapeDtypeStruct((b, l, w), BF16),
        compiler_params=_params("parallel"),
        name="band_sample",
    )(pd, kt_cache, vt_cache, bias_c, bias_n)


def _row_parts(rows, parts):
    if rows % (parts * 2 * SUBLANES):
        parts = 1
    step = rows // parts
    return [slice(n * step, (n + 1) * step) for n in range(parts)]


def _mix_xattn_kernel(x_ref, oa_ref, ob_ref, oc_ref, od_ref, wm_ref, mk_ref, mv_ref, wq_ref, wo_ref, g_ref, b_ref,
                      o_ref, *, alpha, hd, head_axis):
    nb, tm, d = x_ref.shape
    dg = oa_ref.shape[2]
    heads = [slice(h * hd, (h + 1) * hd) for h in range(H_X)]
    if head_axis:
        n_mem = mk_ref.shape[1]
        mk = [mk_ref[n].reshape(n_mem * H_X, hd).astype(BF16) for n in range(nb)]
        mv = [mv_ref[n].reshape(n_mem * H_X, hd).astype(BF16) for n in range(nb)]
    else:
        mk = [[mk_ref[n, :, hs].astype(BF16) for hs in heads] for n in range(nb)]
        mv = [[mv_ref[n, :, hs].astype(BF16) for hs in heads] for n in range(nb)]
    if nb == 1:
        parts = _row_parts(tm, 2)
        rows_of = lambda ref, sl: ref[0, sl, :]
        units = [(n, slice(0, sl.stop - sl.start), 0) for n, sl in enumerate(parts)]
    else:
        parts = [slice(0, nb * tm)]
        rows_of = lambda ref, sl: ref[...].reshape(nb * tm, ref.shape[2])
        units = [(0, slice(n * tm, (n + 1) * tm), n) for n in range(nb)]
    mix = [sum(jnp.dot(rows_of(r, sl), wm_ref[n * dg:(n + 1) * dg, :], preferred_element_type=F32)
               for n, r in enumerate((oa_ref, ob_ref, oc_ref, od_ref))) for sl in parts]
    xs = [_layer_norm(alpha * rows_of(x_ref, sl) + z, g_ref[0:1, :], b_ref[0:1, :]) for sl, z in zip(parts, mix)]
    q = [(_mm(x, wq_ref[...]) * (hd ** -0.5)).astype(BF16) for x in xs]
    if head_axis:
        rows_u = units[0][1].stop - units[0][1].start
        row = lax.broadcasted_iota(jnp.int32, (H_X * rows_u, n_mem * H_X), 0)
        r_head = sum((row >= h * rows_u).astype(jnp.int32) for h in range(1, H_X))
        c_head = lax.broadcasted_iota(jnp.int32, (H_X * rows_u, n_mem * H_X), 1) & (H_X - 1)
        own_head = jnp.where(r_head == c_head, 0.0, -jnp.inf)
        qs = [jnp.concatenate([q[p][rows, hs] for hs in heads], axis=0) for p, rows, _ in units]
        s = [_mm_nt(z, mk[n]) + own_head for z, (_, _, n) in zip(qs, units)]
        e = [jnp.exp(z - jnp.max(z, axis=-1, keepdims=True)) for z in s]
        pr = [z * (1.0 / jnp.sum(z, axis=-1, keepdims=True)) for z in e]
        pv = [_mm(z, mv[n]) for z, (_, _, n) in zip(pr, units)]
        pv = [jnp.concatenate([z[h * rows_u:(h + 1) * rows_u] for h in range(H_X)], axis=-1) for z in pv]
    else:
        s = [[_mm_nt(q[p][rows, hs], kh) for hs, kh in zip(heads, mk[n])] for p, rows, n in units]
        e = [[jnp.exp(z - jnp.max(z, axis=-1, keepdims=True)) for z in su] for su in s]
        pr = [[z * (1.0 / jnp.sum(z, axis=-1, keepdims=True)) for z in eu] for eu in e]
        pv = [jnp.concatenate([_mm(z, vh) for z, vh in zip(pu, mv[n])], axis=-1) for pu, (_, _, n) in zip(pr, units)]
    pv = [jnp.concatenate([z for z, (p, _, _) in zip(pv, units) if p == n], axis=0) for n in range(len(parts))]
    att = [_mm(z, wo_ref[...]) for z in pv]
    out = [_layer_norm(alpha * x + z, g_ref[1:2, :], b_ref[1:2, :]) for x, z in zip(xs, att)]
    if nb == 1:
        for sl, z in zip(parts, out):
            o_ref[0, sl, :] = z
    else:
        o_ref[...] = out[0].reshape(nb, tm, d)


def _mix_xattn(x, mixed, w_out, mk, mv, wq, wo, layer, g, b, *, alpha, rows=1024):
    bsz, l, d = x.shape
    dg = mixed[0].shape[2]
    hd = d // H_X
    tm = _tile(l, rows)
    nb = max(1, min(bsz, XATTN_SEQS_PER_STEP, rows // l)) if tm == l else 1
    while bsz % nb:
        nb -= 1
    row = lambda width: pl.BlockSpec((nb, tm, width), lambda bi, i: (bi, i, 0))
    if mk.ndim == 3:
        mem_spec = pl.BlockSpec((nb,) + mk.shape[1:], lambda bi, i: (bi, 0, 0))
    else:
        mem_spec = pl.BlockSpec((None, nb) + mk.shape[2:], lambda bi, i: (layer, bi, 0, 0, 0))
    return pl.pallas_call(
        functools.partial(_mix_xattn_kernel, alpha=alpha, hd=hd, head_axis=mk.ndim != 3),
        grid=(bsz // nb, l // tm),
        in_specs=[
            row(d), row(dg), row(dg), row(dg), row(dg), _layer_spec(w_out, layer),
            mem_spec, mem_spec,
            _layer_spec(wq, layer), _layer_spec(wo, layer), _layer_spec(g, layer), _layer_spec(b, layer),
        ],
        out_specs=row(d),
        out_shape=jax.ShapeDtypeStruct((bsz, l, d), F32),
        compiler_params=_params("parallel", "parallel"),
        name="mix_xattn_ln",
    )(x, *mixed, w_out, mk, mv, wq, wo, g, b)


def _mlp_kernel(x_ref, w1_ref, w2_ref, g_ref, b_ref, o_ref, xb_scr, *, alpha, parts):
    f = pl.program_id(1)

    @pl.when(f == 0)
    def _():
        xb_scr[...] = x_ref[...].astype(BF16)
        o_ref[...] = jnp.zeros_like(o_ref)

    sl = _row_parts(x_ref.shape[0], parts)
    hid = [jnp.maximum(jnp.dot(xb_scr[s, :], w1_ref[...], preferred_element_type=F32), 0.0) for s in sl]
    act = [(z * z).astype(BF16) for z in hid]
    for s, z in zip(sl, act):
        o_ref[s, :] += jnp.dot(z, w2_ref[...], preferred_element_type=F32)

    @pl.when(f == pl.num_programs(1) - 1)
    def _():
        o_ref[...] = _layer_norm(alpha * x_ref[...] + o_ref[...], g_ref[2:3, :], b_ref[2:3, :])


def _mlp(x, w1, w2, layer, g, b, *, alpha, tm=1024, tf=1024):
    m, d = x.shape
    dff = w1.shape[2]
    tm = _tile(m, tm)
    tf = _tile(dff, tf)
    return pl.pallas_call(
        functools.partial(_mlp_kernel, alpha=alpha, parts=2),
        grid=(m // tm, dff // tf),
        in_specs=[
            pl.BlockSpec((tm, d), lambda i, f: (i, 0)),
            _layer_spec(w1, layer, (d, tf), lambda i, f: (0, f)),
            _layer_spec(w2, layer, (tf, d), lambda i, f: (f, 0)),
            _layer_spec(g, layer), _layer_spec(b, layer),
        ],
        out_specs=pl.BlockSpec((tm, d), lambda i, f: (i, 0)),
        out_shape=jax.ShapeDtypeStruct((m, d), F32),
        scratch_shapes=[pltpu.VMEM((tm, d), BF16)],
        compiler_params=_params("parallel", "arbitrary"),
        name="mlp_ln",
    )(x, w1, w2, g, b)


def _block_diag(blocks):
    g, r, c = blocks.shape
    eye = jnp.eye(g, dtype=blocks.dtype)
    return (eye[:, None, :, None] * blocks[:, :, None, :]).reshape(g * r, g * c)


def _pad_lanes(v):
    return jnp.pad(v, (0, LANES - v.shape[0]))[None, :]


def _s5_params(lam_re, lam_im, log_dt, b_re, b_im, c_re, c_im):
    dt = jnp.exp(log_dt)[:, None]
    mag = jnp.exp(lam_re * dt)
    ar, ai = mag * jnp.cos(lam_im * dt), mag * jnp.sin(lam_im * dt)
    den = lam_re * lam_re + lam_im * lam_im
    fr = ((ar - 1.0) * lam_re + ai * lam_im) / den
    fi = (ai * lam_re - (ar - 1.0) * lam_im) / den
    bbr = fr[..., None] * b_re - fi[..., None] * b_im
    bbi = fr[..., None] * b_im + fi[..., None] * b_re
    wb = jnp.concatenate([_block_diag(jnp.swapaxes(bbr, 1, 2)), _block_diag(jnp.swapaxes(bbi, 1, 2))], axis=1)
    wc = jnp.concatenate([_block_diag(jnp.swapaxes(c_re, 1, 2)), -_block_diag(jnp.swapaxes(c_im, 1, 2))], axis=0)
    ar, ai = ar.reshape(-1), ai.reshape(-1)
    pows = [(ar, ai)]
    for _ in range(SUBLANES - 1):
        pr, pi = pows[-1]
        pows.append((pr * ar - pi * ai, pr * ai + pi * ar))
    cat = lambda idx: jnp.stack([jnp.concatenate(pows[n]) for n in idx], axis=0)
    row = jnp.arange(SUBLANES)[:, None]
    shift_pows = jnp.stack([jnp.where(row >= s, cat((s - 1,)), 0.0) for s in (2, 4)])
    wbr, wbi = wb[:, :ar.shape[0]], wb[:, ar.shape[0]:]
    wb_lag = jnp.concatenate([ar * wbr - ai * wbi, ar * wbi + ai * wbr], axis=1)
    wb2 = jnp.concatenate([wb, wb_lag], axis=0)
    return wb2.astype(BF16), shift_pows, cat(range(SUBLANES)), wc.astype(BF16)


def _rel_bias_table(table, n_rows, n_cols, offset):
    tab = table.astype(F32).T
    rel_min, rel_max = offset - (n_cols - 1), offset + n_rows - 1
    lo, hi = max(rel_min, -REL_CLIP), min(rel_max, REL_CLIP)
    parts = [jnp.repeat(tab[:, :1], lo - rel_min, axis=1), tab[:, lo + REL_CLIP:hi + REL_CLIP + 1],
             jnp.repeat(tab[:, -1:], rel_max - hi, axis=1)]
    ext = jnp.concatenate(parts, axis=1)
    length = n_rows + n_cols - 1
    flipped = jnp.pad(ext[:, ::-1], ((0, 0), (0, 1)))
    shifted = jnp.tile(flipped, (1, n_rows))[:, :n_rows * length].reshape(-1, n_rows, length)
    return shifted[:, :, n_rows - 1:n_rows - 1 + n_cols]


def _band_prompt_bias(table, qb, qp):
    span = N_PREV * CHUNK
    width = span + qp
    bias = _rel_bias_table(table, qp, width, span)
    r = jnp.arange(qp)[:, None] // CHUNK
    j = jnp.arange(width)[None, :]
    in_band = (j // CHUNK >= r) & (j // CHUNK <= r + N_PREV)
    regular = jnp.where(in_band, bias, -jnp.inf)
    first = jnp.stack([jnp.where(in_band & (j >= span - p * qp), bias, -jnp.inf) for p in range(qb // qp)])
    return jnp.stack([first, jnp.broadcast_to(regular, first.shape)])


def _trunk_layer(x, mem_k, mem_v, gdn_conv, gdn_s, s5_h, rg_conv, rg_h, band_k, band_v, p, *, alpha):
    b, l, d = x.shape
    dg = d // N_MIX
    m = b * l
    layer = p["layer"]
    pa, pb, pc, pd, pdb = _inproj(x.reshape(m, d), p["w_in"], layer, (4 * dg, dg, 2 * dg, 3 * dg, LANES))
    pa, pb, pc, pd, pdb = [t.reshape(b, l, -1) for t in (pa, pb, pc, pd, pdb)]

    pad8 = lambda buf: jnp.pad(buf, ((0, 0), (SUBLANES - (CONV_W - 1), 0), (0, 0)))
    chunk = CHUNK if l % CHUNK == 0 else l
    cps = max(1, min(GDN_CHUNKS_PER_STEP, l // chunk))
    o_a, gdn_s_new = _gdn(pa, pdb, pad8(gdn_conv), gdn_s, p["gdn"], layer, t=chunk, cps=cps)
    gdn_conv_new = pa[:, l - (CONV_W - 1):, :3 * dg]

    ns = p["s5"][0].shape[2] // 2
    h0 = jnp.concatenate([s5_h[..., 0].reshape(b, 1, ns), s5_h[..., 1].reshape(b, 1, ns)], axis=-1)
    o_b, h_last = _s5(pb, h0, p["s5"], layer, t=_tile(l, 256))
    s5_h_new = jnp.stack([h_last[:, 0, :ns].reshape(s5_h.shape[:-1]), h_last[:, 0, ns:].reshape(s5_h.shape[:-1])],
                         axis=-1)

    o_c, rg_last = _rglru(pc, pad8(rg_conv), rg_h[:, None, :], p["rglru"], layer, t=_tile(l, 256))
    rg_conv_new = pc[:, l - (CONV_W - 1):, :dg]
    rg_h_new = rg_last[:, 0, :]

    hd = dg // H_D
    if band_k is None:
        qb = N_PREV * CHUNK
        o_d = _band_prompt(pd, p["band_bias_prompt"], layer, qb=qb, qp=BAND_PIECE)
        keep = min(N_PREV * CHUNK, l)
    else:
        o_d = _band_sample(pd, band_k, band_v, layer, p["band_bias_cache"], p["band_bias_new"])
        keep = l
    band_k_new = pd[:, l - keep:, dg:2 * dg].reshape(b, keep, H_D, hd)
    band_v_new = pd[:, l - keep:, 2 * dg:].reshape(b, keep, H_D, hd)

    x3 = _mix_xattn(x, (o_a, o_b, o_c, o_d), p["w_out"], mem_k, mem_v, p["xa_w_q"], p["xa_w_o"], layer,
                    p["ln_g"], p["ln_b"], alpha=alpha)
    x4 = _mlp(x3.reshape(m, d), p["mlp_w1"], p["mlp_w2"], layer, p["ln_g"], p["ln_b"], alpha=alpha)
    return x4.reshape(b, l, d), (gdn_conv_new, gdn_s_new, s5_h_new, rg_conv_new, rg_h_new, band_k_new, band_v_new)


def kernel(x_prompt, x_sample, state_gdn_conv, state_gdn, state_s5, state_rglru_conv, state_rglru, cache_band_k, cache_band_v, cache_mem_k, cache_mem_v, mem_prompt, w_in, w_out, ln_g, ln_b, gdn_conv_w, gdn_conv_b, gdn_a_log, gdn_dt_bias, gdn_norm_g, s5_lam_re, s5_lam_im, s5_log_dt, s5_b_re, s5_b_im, s5_c_re, s5_c_im, s5_d, s5_w_glu, s5_b_glu, rg_conv_w, rg_conv_b, rg_w_r, rg_b_r, rg_w_i, rg_b_i, rg_lam, band_rel_bias, xa_w_q, xa_w_k, xa_w_v, xa_w_o, mlp_w1, mlp_w2):
    depth = w_in.shape[0]
    bp, lp, d = x_prompt.shape
    bs, ls, _ = x_sample.shape
    n_mem = mem_prompt.shape[1]
    dg = d // N_MIX
    hd_x = d // H_X
    alpha = (2.0 * depth) ** 0.25
    band_rows = cache_band_k.shape[2]

    sizes = (3 * dg, dg, H_A, H_A, dg, dg, dg, 3 * dg)
    offs = [0]
    for s in sizes:
        offs.append(offs[-1] + s)

    w_db = jnp.pad(w_in[:, :, offs[2]:offs[4]], ((0, 0), (0, 0), (0, LANES - 2 * H_A)))
    w_in_bf = jnp.concatenate([w_in[:, :, offs[0]:offs[2]], w_in[:, :, offs[4]:offs[5]], w_in[:, :, offs[5]:offs[7]],
                               w_in[:, :, offs[7]:offs[8]], w_db], axis=2).astype(BF16)
    w_out_bf, xa_w_q_bf, xa_w_k_bf, xa_w_v_bf, xa_w_o_bf, mlp_w1_bf, mlp_w2_bf = [
        w.astype(BF16) for w in (w_out, xa_w_q, xa_w_k, xa_w_v, xa_w_o, mlp_w1, mlp_w2)]

    band_kt = jnp.transpose(cache_band_k, (0, 1, 3, 4, 2))
    band_vt = jnp.transpose(cache_band_v, (0, 1, 3, 4, 2))

    over_layers = jax.vmap
    s5_wb, s5_ap, s5_p8, s5_wc = over_layers(_s5_params)(s5_lam_re, s5_lam_im, s5_log_dt, s5_b_re, s5_b_im,
                                                         s5_c_re, s5_c_im)
    rg_wri = jnp.concatenate([over_layers(_block_diag)(rg_w_r), over_layers(_block_diag)(rg_w_i)], axis=2)
    shared = {
        "w_in": w_in_bf, "w_out": w_out_bf, "ln_g": ln_g, "ln_b": ln_b,
        "gdn": (gdn_conv_w, gdn_conv_b[:, None, :], over_layers(_pad_lanes)(gdn_a_log),
                over_layers(_pad_lanes)(gdn_dt_bias), gdn_norm_g[:, None, :]),
        "s5": (s5_wb, s5_ap, s5_p8, s5_wc, s5_d.reshape(depth, 1, dg), s5_w_glu.astype(BF16), s5_b_glu[:, None, :]),
        "rglru": (rg_conv_w, rg_conv_b[:, None, :], rg_wri.astype(BF16),
                  jnp.concatenate([rg_b_r, rg_b_i], axis=1)[:, None, :], rg_lam[:, None, :]),
        "band_bias_prompt": over_layers(lambda t: _band_prompt_bias(t, N_PREV * CHUNK, BAND_PIECE))(band_rel_bias),
        "band_bias_cache": over_layers(lambda t: _rel_bias_table(t, ls, band_rows, band_rows))(band_rel_bias),
        "band_bias_new": over_layers(lambda t: _rel_bias_table(t, ls, ls, 0))(band_rel_bias),
        "xa_w_q": xa_w_q_bf, "xa_w_o": xa_w_o_bf, "mlp_w1": mlp_w1_bf, "mlp_w2": mlp_w2_bf,
    }

    xp, xs = x_prompt, x_sample
    p_states, s_states = [], []
    for l in range(depth):
        p = dict(shared, layer=l)
        mem2 = mem_prompt.reshape(bp * n_mem, d)
        mk = _matmul(mem2, xa_w_k_bf, l).reshape(bp, n_mem, d)
        mv = _matmul(mem2, xa_w_v_bf, l).reshape(bp, n_mem, d)
        xp, st_p = _trunk_layer(
            xp, mk, mv,
            jnp.zeros((bp, CONV_W - 1, 3 * dg), F32), jnp.zeros((bp, H_A, dg // H_A, dg // H_A), F32),
            jnp.zeros((bp, dg // S5_CH, P_B, 2), F32), jnp.zeros((bp, CONV_W - 1, dg), F32),
            jnp.zeros((bp, dg), F32), None, None, p, alpha=alpha)
        p_states.append(st_p + (mk.reshape(bp, n_mem, H_X, hd_x), mv.reshape(bp, n_mem, H_X, hd_x)))
        xs, st_s = _trunk_layer(
            xs, cache_mem_k, cache_mem_v,
            state_gdn_conv[l], state_gdn[l], state_s5[l], state_rglru_conv[l], state_rglru[l],
            band_kt, band_vt, p, alpha=alpha)
        s_states.append(st_s)

    def stk(states, i):
        return jnp.stack([st[i] for st in states], axis=0)

    return (xp, xs,
            stk(p_states, 0), stk(p_states, 1), stk(p_states, 2), stk(p_states, 3), stk(p_states, 4),
            stk(p_states, 5), stk(p_states, 6), stk(p_states, 7), stk(p_states, 8),
            stk(s_states, 0), stk(s_states, 1), stk(s_states, 2), stk(s_states, 3), stk(s_states, 4),
            stk(s_states, 5), stk(s_states, 6))
```

```python
import functools
import math

import jax
import jax.numpy as jnp
from jax import lax
from jax.experimental import pallas as pl
from jax.experimental.pallas import tpu as pltpu

F32 = jnp.float32
BF16 = jnp.bfloat16
HIGHEST = lax.Precision.HIGHEST

N_MIX = 4
CONV_W = 4
CHUNK = 64
H_A = 4
S5_CH = 16
P_B = 64
H_C = 4
RG_C = 8.0
H_D = 4
N_PREV = 8
REL_CLIP = 128
H_X = 4
LN_EPS = 1e-5
NORM_EPS = 1e-6

LANES = 128
SUBLANES = 8
VMEM_LIMIT_BYTES = 56 * 1024 * 1024
GDN_BLOCK = 128
GDN_CHUNKS_PER_STEP = 8
BAND_PIECE = 256
SCAN_ROWS_PER_STEP = 1024
XATTN_SEQS_PER_STEP = 4


def _params(*semantics):
    return pltpu.CompilerParams(dimension_semantics=semantics, vmem_limit_bytes=VMEM_LIMIT_BYTES)


def _tile(n, pref):
    t = min(n, pref)
    while n % t:
        t -= SUBLANES
    return t


def _seqs_per_step(b, l, t):
    nb = max(1, min(b, SCAN_ROWS_PER_STEP // t)) if l == t else 1
    while b % nb:
        nb -= 1
    return nb


def _mm(a, b):
    return jnp.dot(a.astype(BF16), b.astype(BF16), preferred_element_type=F32)


def _mm_nt(a, b):
    return lax.dot_general(a.astype(BF16), b.astype(BF16), (((1,), (1,)), ((), ())),
                           preferred_element_type=F32)


def _mm_tn(a, b):
    return lax.dot_general(a.astype(BF16), b.astype(BF16), (((0,), (0,)), ((), ())),
                           preferred_element_type=F32)


def _mm_f32(a, b):
    return jnp.dot(a, b, precision=HIGHEST, preferred_element_type=F32)


def _sigmoid(x):
    return 1.0 / (1.0 + jnp.exp(-x))


def _softplus(x):
    return jnp.maximum(x, 0.0) + jnp.log1p(jnp.exp(-jnp.abs(x)))


def _gelu_tanh(x):
    c = math.sqrt(2.0 / math.pi)
    return 0.5 * x * (1.0 + jnp.tanh(c * (x + 0.044715 * (x * x * x))))


def _layer_norm(z, g, b):
    mu = jnp.mean(z, axis=-1, keepdims=True)
    zc = z - mu
    var = jnp.mean(zc * zc, axis=-1, keepdims=True)
    return zc * lax.rsqrt(var + LN_EPS) * g + b


def _matmul_kernel(x_ref, w_ref, o_ref):
    o_ref[...] = _mm(x_ref[...], w_ref[...])


def _layer_spec(w, layer, block=None, index=None):
    block = tuple(w.shape[1:]) if block is None else block
    index = (lambda *_: (0,) * len(block)) if index is None else index
    return pl.BlockSpec((None,) + block, lambda *g: (layer,) + tuple(index(*g)))


def _matmul(x, w_bf16, layer, tm=512):
    m, k = x.shape
    n = w_bf16.shape[2]
    tm = _tile(m, tm)
    return pl.pallas_call(
        _matmul_kernel,
        grid=(m // tm,),
        in_specs=[pl.BlockSpec((tm, k), lambda i: (i, 0)), _layer_spec(w_bf16, layer)],
        out_specs=pl.BlockSpec((tm, n), lambda i: (i, 0)),
        out_shape=jax.ShapeDtypeStruct((m, n), F32),
        compiler_params=_params("parallel"),
        name="matmul",
    )(x, w_bf16)


def _inproj_kernel(x_ref, w_ref, *o_refs, bounds):
    xb = x_ref[...].astype(BF16)
    for o_ref, (s, e) in zip(o_refs, bounds):
        o_ref[...] = jnp.dot(xb, w_ref[:, s:e], preferred_element_type=F32)


def _inproj(x, w_bf16, layer, widths, tm=512):
    m, k = x.shape
    tm = _tile(m, tm)
    bounds, s = [], 0
    for w in widths:
        bounds.append((s, s + w))
        s += w
    return pl.pallas_call(
        functools.partial(_inproj_kernel, bounds=tuple(bounds)),
        grid=(m // tm,),
        in_specs=[pl.BlockSpec((tm, k), lambda i: (i, 0)), _layer_spec(w_bf16, layer)],
        out_specs=[pl.BlockSpec((tm, w), lambda i: (i, 0)) for w in widths],
        out_shape=[jax.ShapeDtypeStruct((m, w), F32) for w in widths],
        compiler_params=_params("parallel"),
        name="inproj",
    )(x, w_bf16)


def _inverse_masks(r, c, t):
    neg_diag8 = jnp.where((r >> 3) == (c >> 3), -1.0, 0.0)
    offs, lb = [], 3
    while (1 << lb) < t:
        off = ((r >> (lb + 1)) == (c >> (lb + 1))) & (((r >> lb) & 1) == 1) & (((c >> lb) & 1) == 0)
        offs.append(jnp.where(off, 1.0, 0.0))
        lb += 1
    return neg_diag8, offs


def _unit_lower_inverse_offdiag(a_list, masks):
    neg_diag8, offs = masks
    n1 = [a * neg_diag8 for a in a_list]
    n2 = [_mm(x, x) for x in n1]
    n3 = [_mm(x, x2) for x, x2 in zip(n1, n2)]
    n4 = [_mm(x2, x2) for x2 in n2]
    p = [x + x2 + x3 for x, x2, x3 in zip(n1, n2, n3)]
    pn4 = [_mm(pp, x4) for pp, x4 in zip(p, n4)]
    y = [pp + x4 + px for pp, x4, px in zip(p, n4, pn4)]
    for off in offs:
        m = [a * off for a in a_list]
        z = [mm + _mm(yy, mm) for yy, mm in zip(y, m)]
        zy = [_mm(zz, yy) for zz, yy in zip(z, y)]
        y = [yy - (zz + zzy) for yy, zz, zzy in zip(y, z, zy)]
    return y


def _split3(x):
    h1 = x.astype(BF16)
    r1 = x - h1.astype(F32)
    h2 = r1.astype(BF16)
    h3 = (r1 - h2.astype(F32)).astype(BF16)
    return h1, h2, h3


def _gdn_kernel(qkv_ref, gate_ref, db_ref, cbuf_ref, s0_ref, cw_ref, cb_ref, alog_ref, dtb_ref, ng_ref,
                o_ref, sfin_ref, xp_scr, s_scr, o_scr, *, t, cps, dk):
    i = pl.program_id(1)
    tb = t * cps
    nh = H_A
    dq = nh * dk

    @pl.when(i == 0)
    def _():
        xp_scr[0:SUBLANES, :] = cbuf_ref[0]
        s_scr[...] = s0_ref[0]

    x = qkv_ref[0]
    xp_scr[SUBLANES:SUBLANES + tb, :] = x
    base = SUBLANES - (CONV_W - 1)
    y = xp_scr[base:base + tb, :] * cw_ref[0:1, :]
    for j in range(1, CONV_W):
        y = y + xp_scr[base + j:base + j + tb, :] * cw_ref[j:j + 1, :]
    y = y + cb_ref[...]
    xp_scr[0:SUBLANES, :] = x[tb - SUBLANES:tb, :]
    y = y * _sigmoid(y)

    db = db_ref[0]
    log_a = -jnp.exp(alog_ref[...]) * _softplus(db + dtb_ref[...])
    beta_all = _sigmoid(db)

    bs = min(tb, GDN_BLOCK)
    lt = t.bit_length() - 1
    r = lax.broadcasted_iota(jnp.int32, (bs, bs), 0)
    c = lax.broadcasted_iota(jnp.int32, (bs, bs), 1)
    same = (r >> lt) == (c >> lt)
    causal_neg = jnp.where(same & (r >= c), 0.0, -jnp.inf)
    strict_f = jnp.where(same & (r > c), 1.0, 0.0)
    tril = jnp.where(same & (r >= c), 1.0, 0.0).astype(BF16)
    striu = jnp.where(same & (r < c), 1.0, 0.0).astype(BF16)
    inv_masks = _inverse_masks(r, c, t)

    blocks = list(range(0, tb, bs))
    pairs = [(bi, h) for bi in range(len(blocks)) for h in range(nh)]
    la3 = [_split3(log_a[b0:b0 + bs, :]) for b0 in blocks]
    gc = [sum(jnp.dot(tril, part, preferred_element_type=F32) for part in parts) for parts in la3]
    rv = [sum(jnp.dot(striu, part, preferred_element_type=F32) for part in parts) for parts in la3]
    eg = [jnp.exp(g) for g in gc]
    erv = [jnp.exp(g) for g in rv]
    e_tot = [jnp.exp(g + g2) for g, g2 in zip(gc, rv)]
    gc_rows = [g.T for g in gc]

    def head_cols(z, bi, h, off):
        b0 = blocks[bi]
        return z[b0:b0 + bs, off + h * dk:off + (h + 1) * dk]

    q = [head_cols(y, bi, h, 0) for bi, h in pairs]
    k = [head_cols(y, bi, h, dq) for bi, h in pairs]
    v = [head_cols(y, bi, h, 2 * dq) for bi, h in pairs]
    q = [z * lax.rsqrt(jnp.sum(z * z, axis=-1, keepdims=True) + NORM_EPS) * (dk ** -0.5) for z in q]
    k = [z * lax.rsqrt(jnp.sum(z * z, axis=-1, keepdims=True) + NORM_EPS) for z in k]
    decay = [jnp.exp(gc[bi][:, h:h + 1] - gc_rows[bi][h:h + 1, :] + causal_neg) for bi, h in pairs]
    beta = [beta_all[blocks[bi]:blocks[bi] + bs, nh + h:nh + h + 1] for bi, h in pairs]
    eg_col = [eg[bi][:, h:h + 1] for bi, h in pairs]
    kk = [_mm_nt(z, z) for z in k]
    qk = [_mm_nt(zq, zk) for zq, zk in zip(q, k)]
    a = [(b * z * d) * strict_f for b, z, d in zip(beta, kk, decay)]
    qk = [z * d for z, d in zip(qk, decay)]
    y_inv = _unit_lower_inverse_offdiag(a, inv_masks)
    rhs = [jnp.concatenate([zv * b, zk * (b * e)], axis=-1) for zv, zk, b, e in zip(v, k, beta, eg_col)]
    sol = [z + _mm(yi, z) for yi, z in zip(y_inv, rhs)]
    q_dec = [z * e for z, e in zip(q, eg_col)]
    k_dec = [z * erv[bi][:, h:h + 1] for z, (bi, h) in zip(k, pairs)]

    chunks = [(bi, r0) for bi in range(len(blocks)) for r0 in range(0, bs, t)]
    n_kw = [[_mm_tn(k_dec[bi * nh + h][r0:r0 + t], sol[bi * nh + h][r0:r0 + t]) for h in range(nh)]
            for bi, r0 in chunks]
    states = [s_scr[h] for h in range(nh)]
    starts = []
    for (bi, r0), nk in zip(chunks, n_kw):
        starts.append(states)
        drop = [_mm(z[:, dk:], s) for z, s in zip(nk, states)]
        states = [s * e_tot[bi][r0:r0 + 1, h:h + 1] + (z[:, :dk] - d)
                  for h, (s, z, d) in enumerate(zip(states, nk, drop))]
    ws = [[_mm(jnp.concatenate([sol[bi * nh + h][r0:r0 + t, dk:], q_dec[bi * nh + h][r0:r0 + t]], axis=0), st[h])
           for h in range(nh)] for (bi, r0), st in zip(chunks, starts)]
    v_news = [[] for _ in pairs]
    o_inter = [[] for _ in pairs]
    for (bi, r0), wc in zip(chunks, ws):
        for h, z in enumerate(wc):
            v_news[bi * nh + h].append(sol[bi * nh + h][r0:r0 + t, :dk] - z[:t])
            o_inter[bi * nh + h].append(z[t:])
    o = [jnp.concatenate(oi, axis=0) + _mm(z, jnp.concatenate(vn, axis=0))
         for oi, z, vn in zip(o_inter, qk, v_news)]
    o = [z * lax.rsqrt(jnp.mean(z * z, axis=-1, keepdims=True) + NORM_EPS) * ng_ref[...] for z in o]
    for (bi, h), z in zip(pairs, o):
        o_scr[blocks[bi]:blocks[bi] + bs, h * dk:(h + 1) * dk] = z
    for h in range(nh):
        s_scr[h] = states[h]

    g = gate_ref[0]
    o_ref[0] = (o_scr[...] * (g * _sigmoid(g))).astype(o_ref.dtype)

    @pl.when(i == pl.num_programs(1) - 1)
    def _():
        sfin_ref[0] = s_scr[...]


def _gdn(pa, pdb, cbuf8, s0, params, layer, *, t, cps):
    b, l, _ = pa.shape
    nh, dk = s0.shape[1], s0.shape[2]
    dq = nh * dk
    tb = t * cps
    return pl.pallas_call(
        functools.partial(_gdn_kernel, t=t, cps=cps, dk=dk),
        grid=(b, l // tb),
        in_specs=[
            pl.BlockSpec((1, tb, 3 * dq), lambda bi, i: (bi, i, 0)),
            pl.BlockSpec((1, tb, dq), lambda bi, i: (bi, i, 3)),
            pl.BlockSpec((1, tb, LANES), lambda bi, i: (bi, i, 0)),
            pl.BlockSpec((1, SUBLANES, 3 * dq), lambda bi, i: (bi, 0, 0)),
            pl.BlockSpec((1, nh, dk, dk), lambda bi, i: (bi, 0, 0, 0)),
        ] + [_layer_spec(w, layer) for w in params],
        out_specs=[
            pl.BlockSpec((1, tb, dq), lambda bi, i: (bi, i, 0)),
            pl.BlockSpec((1, nh, dk, dk), lambda bi, i: (bi, 0, 0, 0)),
        ],
        out_shape=[jax.ShapeDtypeStruct((b, l, dq), BF16), jax.ShapeDtypeStruct((b, nh, dk, dk), F32)],
        scratch_shapes=[pltpu.VMEM((SUBLANES + tb, 3 * dq), F32), pltpu.VMEM((nh, dk, dk), F32),
                        pltpu.VMEM((tb, dq), F32)],
        compiler_params=_params("parallel", "arbitrary"),
        name="gdn",
    )(pa, pa, pdb, cbuf8, s0, *params)


def _s5_kernel(u_ref, h0_ref, wb_ref, ap_ref, p8_ref, wc_ref, d_ref, wg_ref, bg_ref,
               o_ref, hl_ref, carry_scr, h_scr, *, t, ns):
    i = pl.program_id(1)
    nb, _, dg = u_ref.shape

    @pl.when(i == 0)
    def _():
        carry_scr[...] = h0_ref[...]

    u = u_ref[...].reshape(nb * t, dg)
    ng = nb * t // SUBLANES
    first = lax.broadcasted_iota(jnp.int32, (SUBLANES, dg), 0) == 0
    u_prev = jnp.where(first, 0.0, pltpu.roll(u.reshape(ng, SUBLANES, dg), 1, 1)).reshape(nb * t, dg)
    x = _mm(jnp.concatenate([u, u_prev], axis=-1), wb_ref[...])
    xr = x[:, :ns].reshape(ng, SUBLANES, ns)
    xi = x[:, ns:].reshape(ng, SUBLANES, ns)
    for lvl in range(2):
        s = 2 << lvl
        pr, pi = ap_ref[lvl, :, :ns], ap_ref[lvl, :, ns:]
        sr = pltpu.roll(xr, s, 1)
        si = pltpu.roll(xi, s, 1)
        xr, xi = xr + (pr * sr - pi * si), xi + (pr * si + pi * sr)
    p8r, p8i = p8_ref[:, :ns], p8_ref[:, ns:]
    for n in range(nb):
        cr, ci = carry_scr[n, :, :ns], carry_scr[n, :, ns:]
        for j in range(n * t // SUBLANES, (n + 1) * t // SUBLANES):
            sl = slice(j * SUBLANES, (j + 1) * SUBLANES)
            br = xr[j] + (p8r * cr - p8i * ci)
            bi = xi[j] + (p8r * ci + p8i * cr)
            h_scr[sl, :ns] = br
            h_scr[sl, ns:] = bi
            cr, ci = br[SUBLANES - 1:SUBLANES], bi[SUBLANES - 1:SUBLANES]
        carry_scr[n, :, :ns] = cr
        carry_scr[n, :, ns:] = ci
    hl_ref[...] = carry_scr[...]

    y = _mm(h_scr[...], wc_ref[...]) + d_ref[...] * u
    y = _gelu_tanh(y)
    z = _mm(y, wg_ref[...]) + bg_ref[...]
    o_ref[...] = (y * _sigmoid(z)).astype(o_ref.dtype).reshape(nb, t, dg)


def _s5(pb, h0, params, layer, *, t):
    b, l, dg = pb.shape
    ns2 = params[0].shape[2]
    ns = ns2 // 2
    nb = _seqs_per_step(b, l, t)
    return pl.pallas_call(
        functools.partial(_s5_kernel, t=t, ns=ns),
        grid=(b // nb, l // t),
        in_specs=[
            pl.BlockSpec((nb, t, dg), lambda bi, i: (bi, i, 0)),
            pl.BlockSpec((nb, 1, ns2), lambda bi, i: (bi, 0, 0)),
        ] + [_layer_spec(w, layer) for w in params],
        out_specs=[
            pl.BlockSpec((nb, t, dg), lambda bi, i: (bi, i, 0)),
            pl.BlockSpec((nb, 1, ns2), lambda bi, i: (bi, 0, 0)),
        ],
        out_shape=[jax.ShapeDtypeStruct((b, l, dg), BF16), jax.ShapeDtypeStruct((b, 1, ns2), F32)],
        scratch_shapes=[pltpu.VMEM((nb, 1, ns2), F32), pltpu.VMEM((nb * t, ns2), F32)],
        compiler_params=_params("parallel", "arbitrary"),
        name="s5",
    )(pb, h0, *params)


def _rglru_kernel(xg_ref, cbuf_ref, h0_ref, cw_ref, cb_ref, wri_ref, bri_ref, lam_ref,
                  o_ref, hl_ref, xp_scr, carry_scr, h_scr, *, t, dg):
    i = pl.program_id(1)
    nb = xg_ref.shape[0]

    @pl.when(i == 0)
    def _():
        xp_scr[:, 0:SUBLANES, :] = cbuf_ref[...]
        carry_scr[...] = h0_ref[...]

    base = SUBLANES - (CONV_W - 1)
    ys = []
    for n in range(nb):
        x = xg_ref[n, :, :dg]
        xp_scr[n, SUBLANES:SUBLANES + t, :] = x
        y = xp_scr[n, base:base + t, :] * cw_ref[0:1, :]
        for j in range(1, CONV_W):
            y = y + xp_scr[n, base + j:base + j + t, :] * cw_ref[j:j + 1, :]
        xp_scr[n, 0:SUBLANES, :] = x[t - SUBLANES:t, :]
        ys.append(y + cb_ref[...])
    y = jnp.concatenate(ys, axis=0)
    gb = xg_ref[...][:, :, dg:].reshape(nb * t, dg)

    ri = _mm(y, wri_ref[...]) + bri_ref[...]
    rg = _sigmoid(ri[:, :dg])
    ig = _sigmoid(ri[:, dg:])
    log_a = (-RG_C * rg) * _softplus(-lam_ref[...])
    a = jnp.exp(log_a)
    th = jnp.tanh(log_a)
    xin = jnp.sqrt(-2.0 * th / (1.0 - th)) * (ig * y)

    ng = nb * t // SUBLANES
    a = a.reshape(ng, SUBLANES, dg)
    xin = xin.reshape(ng, SUBLANES, dg)
    row = lax.broadcasted_iota(jnp.int32, (SUBLANES, dg), 0)
    for lvl in range(3):
        s = 1 << lvl
        keep = row >= s
        a_s = jnp.where(keep, pltpu.roll(a, s, 1), 1.0)
        x_s = jnp.where(keep, pltpu.roll(xin, s, 1), 0.0)
        xin = a * x_s + xin
        a = a * a_s
    for n in range(nb):
        cr = carry_scr[n]
        for j in range(n * t // SUBLANES, (n + 1) * t // SUBLANES):
            hb = xin[j] + a[j] * cr
            h_scr[j * SUBLANES:(j + 1) * SUBLANES, :] = hb
            cr = hb[SUBLANES - 1:SUBLANES]
        carry_scr[n] = cr
    hl_ref[...] = carry_scr[...]
    o_ref[...] = (h_scr[...] * _gelu_tanh(gb)).astype(o_ref.dtype).reshape(nb, t, dg)


def _rglru(pc, cbuf8, h0, params, layer, *, t):
    b, l, dg2 = pc.shape
    dg = dg2 // 2
    nb = _seqs_per_step(b, l, t)
    return pl.pallas_call(
        functools.partial(_rglru_kernel, t=t, dg=dg),
        grid=(b // nb, l // t),
        in_specs=[
            pl.BlockSpec((nb, t, dg2), lambda bi, i: (bi, i, 0)),
            pl.BlockSpec((nb, SUBLANES, dg), lambda bi, i: (bi, 0, 0)),
            pl.BlockSpec((nb, 1, dg), lambda bi, i: (bi, 0, 0)),
        ] + [_layer_spec(w, layer) for w in params],
        out_specs=[
            pl.BlockSpec((nb, t, dg), lambda bi, i: (bi, i, 0)),
            pl.BlockSpec((nb, 1, dg), lambda bi, i: (bi, 0, 0)),
        ],
        out_shape=[jax.ShapeDtypeStruct((b, l, dg), BF16), jax.ShapeDtypeStruct((b, 1, dg), F32)],
        scratch_shapes=[pltpu.VMEM((nb, SUBLANES + t, dg), F32), pltpu.VMEM((nb, 1, dg), F32),
                        pltpu.VMEM((nb * t, dg), F32)],
        compiler_params=_params("parallel", "arbitrary"),
        name="rglru",
    )(pc, cbuf8, h0, *params)


def _band_prompt_kernel(q_ref, kc_ref, vc_ref, kp_ref, vp_ref, bias_ref, o_ref, *, qb, qp, hd):
    span = N_PREV * CHUNK
    q = q_ref[0] * (hd ** -0.5)
    k = jnp.concatenate([kp_ref[0], kc_ref[0]], axis=0)
    v = jnp.concatenate([vp_ref[0], vc_ref[0]], axis=0)
    units = [(h, p) for h in range(H_D) for p in range(qb // qp)]
    col = lambda h: slice(h * hd, (h + 1) * hd)
    win = lambda p: slice(qb - span + p * qp, qb + (p + 1) * qp)
    s, e, den, o = {}, {}, {}, []
    for n in range(len(units) + 2):
        if n < len(units):
            h, p = units[n]
            s[n] = _mm_nt(q[p * qp:(p + 1) * qp, col(h)], k[win(p), col(h)]) + bias_ref[0, p, h]
        if 0 <= n - 1 < len(units):
            z = s.pop(n - 1)
            e[n - 1] = jnp.exp(z - jnp.max(z, axis=-1, keepdims=True))
            den[n - 1] = jnp.sum(e[n - 1], axis=-1, keepdims=True)
        if 0 <= n - 2 < len(units):
            h, p = units[n - 2]
            o.append(_mm(e.pop(n - 2), v[win(p), col(h)]) / den.pop(n - 2))
    npc = qb // qp
    o_ref[0] = jnp.concatenate([jnp.concatenate(o[h * npc:(h + 1) * npc], axis=0) for h in range(H_D)],
                               axis=-1).astype(o_ref.dtype)


def _band_prompt(pd, bias, layer, *, qb, qp):
    b, l, w3 = pd.shape
    w = w3 // 3
    hd = w // H_D
    prev = lambda bi, i: jnp.maximum(i - 1, 0)
    return pl.pallas_call(
        functools.partial(_band_prompt_kernel, qb=qb, qp=qp, hd=hd),
        grid=(b, l // qb),
        in_specs=[
            pl.BlockSpec((1, qb, w), lambda bi, i: (bi, i, 0)),
            pl.BlockSpec((1, qb, w), lambda bi, i: (bi, i, 1)),
            pl.BlockSpec((1, qb, w), lambda bi, i: (bi, i, 2)),
            pl.BlockSpec((1, qb, w), lambda bi, i: (bi, prev(bi, i), 1)),
            pl.BlockSpec((1, qb, w), lambda bi, i: (bi, prev(bi, i), 2)),
            pl.BlockSpec((None, 1) + bias.shape[2:], lambda bi, i: (layer, jnp.minimum(i, 1), 0, 0, 0, 0)),
        ],
        out_specs=pl.BlockSpec((1, qb, w), lambda bi, i: (bi, i, 0)),
        out_shape=jax.ShapeDtypeStruct((b, l, w), BF16),
        compiler_params=_params("parallel", "arbitrary"),
        name="band_prompt",
    )(pd, pd, pd, pd, pd, bias)


def _band_sample_kernel(qkv_ref, kt_ref, vt_ref, bc_ref, bn_ref, o_ref, *, hd):
    nb = qkv_ref.shape[0]
    w = H_D * hd
    units = [(n, h, slice(h * hd, (h + 1) * hd)) for n in range(nb) for h in range(H_D)]
    q = [qkv_ref[n, :, :w] * (hd ** -0.5) for n in range(nb)]
    kn = [qkv_ref[n, :, w:2 * w] for n in range(nb)]
    vn = [qkv_ref[n, :, 2 * w:] for n in range(nb)]
    sc = [_mm(q[n][:, sl], kt_ref[n, h]) + bc_ref[h] for n, h, sl in units]
    sn = [_mm_nt(q[n][:, sl], kn[n][:, sl]) + bn_ref[h] for n, h, sl in units]
    m = [jnp.maximum(jnp.max(c, axis=-1, keepdims=True), jnp.max(z, axis=-1, keepdims=True)) for c, z in zip(sc, sn)]
    pc = [jnp.exp(c - z) for c, z in zip(sc, m)]
    pn = [jnp.exp(c - z) for c, z in zip(sn, m)]
    den = [jnp.sum(c, axis=-1, keepdims=True) + jnp.sum(z, axis=-1, keepdims=True) for c, z in zip(pc, pn)]
    outs = [(_mm_nt(c, vt_ref[n, h]) + _mm(z, vn[n][:, sl])) / d
            for (n, h, sl), c, z, d in zip(units, pc, pn, den)]
    for n in range(nb):
        o_ref[n] = jnp.concatenate(outs[n * H_D:(n + 1) * H_D], axis=-1).astype(o_ref.dtype)


def _band_sample(pd, kt_cache, vt_cache, layer, bias_c, bias_n):
    b, l, w3 = pd.shape
    w = w3 // 3
    hd = w // H_D
    rows = kt_cache.shape[-1]
    nb = XATTN_SEQS_PER_STEP
    while b % nb:
        nb -= 1
    cache_spec = pl.BlockSpec((None, nb, H_D, hd, rows), lambda bi: (layer, bi, 0, 0, 0))
    return pl.pallas_call(
        functools.partial(_band_sample_kernel, hd=hd),
        grid=(b // nb,),
        in_specs=[
            pl.BlockSpec((nb, l, w3), lambda bi: (bi, 0, 0)),
            cache_spec, cache_spec, _layer_spec(bias_c, layer), _layer_spec(bias_n, layer),
        ],
        out_specs=pl.BlockSpec((nb, l, w), lambda bi: (bi, 0, 0)),
        out_shape=jax.ShapeDtypeStruct((b, l, w), BF16),
        compiler_params=_params("parallel"),
        name="band_sample",
    )(pd, kt_cache, vt_cache, bias_c, bias_n)


def _row_parts(rows, parts):
    if rows % (parts * 2 * SUBLANES):
        parts = 1
    step = rows // parts
    return [slice(n * step, (n + 1) * step) for n in range(parts)]


def _mix_xattn_kernel(x_ref, oa_ref, ob_ref, oc_ref, od_ref, wm_ref, mk_ref, mv_ref, wq_ref, wo_ref, g_ref, b_ref,
                      o_ref, *, alpha, hd, head_axis):
    nb, tm, d = x_ref.shape
    dg = oa_ref.shape[2]
    heads = [slice(h * hd, (h + 1) * hd) for h in range(H_X)]
    if head_axis:
        n_mem = mk_ref.shape[1]
        mk = [mk_ref[n].reshape(n_mem * H_X, hd).astype(BF16) for n in range(nb)]
        mv = [mv_ref[n].reshape(n_mem * H_X, hd).astype(BF16) for n in range(nb)]
    else:
        mk = [[mk_ref[n, :, hs].astype(BF16) for hs in heads] for n in range(nb)]
        mv = [[mv_ref[n, :, hs].astype(BF16) for hs in heads] for n in range(nb)]
    if nb == 1:
        parts = _row_parts(tm, 2)
        rows_of = lambda ref, sl: ref[0, sl, :]
        units = [(n, slice(0, sl.stop - sl.start), 0) for n, sl in enumerate(parts)]
    else:
        parts = [slice(0, nb * tm)]
        rows_of = lambda ref, sl: ref[...].reshape(nb * tm, ref.shape[2])
        units = [(0, slice(n * tm, (n + 1) * tm), n) for n in range(nb)]
    mix = [sum(jnp.dot(rows_of(r, sl), wm_ref[n * dg:(n + 1) * dg, :], preferred_element_type=F32)
               for n, r in enumerate((oa_ref, ob_ref, oc_ref, od_ref))) for sl in parts]
    xs = [_layer_norm(alpha * rows_of(x_ref, sl) + z, g_ref[0:1, :], b_ref[0:1, :]) for sl, z in zip(parts, mix)]
    q = [(_mm(x, wq_ref[...]) * (hd ** -0.5)).astype(BF16) for x in xs]
    if head_axis:
        rows_u = units[0][1].stop - units[0][1].start
        row = lax.broadcasted_iota(jnp.int32, (H_X * rows_u, n_mem * H_X), 0)
        r_head = sum((row >= h * rows_u).astype(jnp.int32) for h in range(1, H_X))
        c_head = lax.broadcasted_iota(jnp.int32, (H_X * rows_u, n_mem * H_X), 1) & (H_X - 1)
        own_head = jnp.where(r_head == c_head, 0.0, -jnp.inf)
        qs = [jnp.concatenate([q[p][rows, hs] for hs in heads], axis=0) for p, rows, _ in units]
        s = [_mm_nt(z, mk[n]) + own_head for z, (_, _, n) in zip(qs, units)]
        e = [jnp.exp(z - jnp.max(z, axis=-1, keepdims=True)) for z in s]
        pr = [z * (1.0 / jnp.sum(z, axis=-1, keepdims=True)) for z in e]
        pv = [_mm(z, mv[n]) for z, (_, _, n) in zip(pr, units)]
        pv = [jnp.concatenate([z[h * rows_u:(h + 1) * rows_u] for h in range(H_X)], axis=-1) for z in pv]
    else:
        s = [[_mm_nt(q[p][rows, hs], kh) for hs, kh in zip(heads, mk[n])] for p, rows, n in units]
        e = [[jnp.exp(z - jnp.max(z, axis=-1, keepdims=True)) for z in su] for su in s]
        pr = [[z * (1.0 / jnp.sum(z, axis=-1, keepdims=True)) for z in eu] for eu in e]
        pv = [jnp.concatenate([_mm(z, vh) for z, vh in zip(pu, mv[n])], axis=-1) for pu, (_, _, n) in zip(pr, units)]
    pv = [jnp.concatenate([z for z, (p, _, _) in zip(pv, units) if p == n], axis=0) for n in range(len(parts))]
    att = [_mm(z, wo_ref[...]) for z in pv]
    out = [_layer_norm(alpha * x + z, g_ref[1:2, :], b_ref[1:2, :]) for x, z in zip(xs, att)]
    if nb == 1:
        for sl, z in zip(parts, out):
            o_ref[0, sl, :] = z
    else:
        o_ref[...] = out[0].reshape(nb, tm, d)


def _mix_xattn(x, mixed, w_out, mk, mv, wq, wo, layer, g, b, *, alpha, rows=1024):
    bsz, l, d = x.shape
    dg = mixed[0].shape[2]
    hd = d // H_X
    tm = _tile(l, rows)
    nb = max(1, min(bsz, XATTN_SEQS_PER_STEP, rows // l)) if tm == l else 1
    while bsz % nb:
        nb -= 1
    row = lambda width: pl.BlockSpec((nb, tm, width), lambda bi, i: (bi, i, 0))
    if mk.ndim == 3:
        mem_spec = pl.BlockSpec((nb,) + mk.shape[1:], lambda bi, i: (bi, 0, 0))
    else:
        mem_spec = pl.BlockSpec((None, nb) + mk.shape[2:], lambda bi, i: (layer, bi, 0, 0, 0))
    return pl.pallas_call(
        functools.partial(_mix_xattn_kernel, alpha=alpha, hd=hd, head_axis=mk.ndim != 3),
        grid=(bsz // nb, l // tm),
        in_specs=[
            row(d), row(dg), row(dg), row(dg), row(dg), _layer_spec(w_out, layer),
            mem_spec, mem_spec,
            _layer_spec(wq, layer), _layer_spec(wo, layer), _layer_spec(g, layer), _layer_spec(b, layer),
        ],
        out_specs=row(d),
        out_shape=jax.ShapeDtypeStruct((bsz, l, d), F32),
        compiler_params=_params("parallel", "parallel"),
        name="mix_xattn_ln",
    )(x, *mixed, w_out, mk, mv, wq, wo, g, b)


def _mlp_kernel(x_ref, w1_ref, w2_ref, g_ref, b_ref, o_ref, xb_scr, *, alpha, parts):
    f = pl.program_id(1)

    @pl.when(f == 0)
    def _():
        xb_scr[...] = x_ref[...].astype(BF16)
        o_ref[...] = jnp.zeros_like(o_ref)

    sl = _row_parts(x_ref.shape[0], parts)
    hid = [jnp.maximum(jnp.dot(xb_scr[s, :], w1_ref[...], preferred_element_type=F32), 0.0) for s in sl]
    act = [(z * z).astype(BF16) for z in hid]
    for s, z in zip(sl, act):
        o_ref[s, :] += jnp.dot(z, w2_ref[...], preferred_element_type=F32)

    @pl.when(f == pl.num_programs(1) - 1)
    def _():
        o_ref[...] = _layer_norm(alpha * x_ref[...] + o_ref[...], g_ref[2:3, :], b_ref[2:3, :])


def _mlp(x, w1, w2, layer, g, b, *, alpha, tm=1024, tf=1024):
    m, d = x.shape
    dff = w1.shape[2]
    tm = _tile(m, tm)
    tf = _tile(dff, tf)
    return pl.pallas_call(
        functools.partial(_mlp_kernel, alpha=alpha, parts=2),
        grid=(m // tm, dff // tf),
        in_specs=[
            pl.BlockSpec((tm, d), lambda i, f: (i, 0)),
            _layer_spec(w1, layer, (d, tf), lambda i, f: (0, f)),
            _layer_spec(w2, layer, (tf, d), lambda i, f: (f, 0)),
            _layer_spec(g, layer), _layer_spec(b, layer),
        ],
        out_specs=pl.BlockSpec((tm, d), lambda i, f: (i, 0)),
        out_shape=jax.ShapeDtypeStruct((m, d), F32),
        scratch_shapes=[pltpu.VMEM((tm, d), BF16)],
        compiler_params=_params("parallel", "arbitrary"),
        name="mlp_ln",
    )(x, w1, w2, g, b)


def _block_diag(blocks):
    g, r, c = blocks.shape
    eye = jnp.eye(g, dtype=blocks.dtype)
    return (eye[:, None, :, None] * blocks[:, :, None, :]).reshape(g * r, g * c)


def _pad_lanes(v):
    return jnp.pad(v, (0, LANES - v.shape[0]))[None, :]


def _s5_params(lam_re, lam_im, log_dt, b_re, b_im, c_re, c_im):
    dt = jnp.exp(log_dt)[:, None]
    mag = jnp.exp(lam_re * dt)
    ar, ai = mag * jnp.cos(lam_im * dt), mag * jnp.sin(lam_im * dt)
    den = lam_re * lam_re + lam_im * lam_im
    fr = ((ar - 1.0) * lam_re + ai * lam_im) / den
    fi = (ai * lam_re - (ar - 1.0) * lam_im) / den
    bbr = fr[..., None] * b_re - fi[..., None] * b_im
    bbi = fr[..., None] * b_im + fi[..., None] * b_re
    wb = jnp.concatenate([_block_diag(jnp.swapaxes(bbr, 1, 2)), _block_diag(jnp.swapaxes(bbi, 1, 2))], axis=1)
    wc = jnp.concatenate([_block_diag(jnp.swapaxes(c_re, 1, 2)), -_block_diag(jnp.swapaxes(c_im, 1, 2))], axis=0)
    ar, ai = ar.reshape(-1), ai.reshape(-1)
    pows = [(ar, ai)]
    for _ in range(SUBLANES - 1):
        pr, pi = pows[-1]
        pows.append((pr * ar - pi * ai, pr * ai + pi * ar))
    cat = lambda idx: jnp.stack([jnp.concatenate(pows[n]) for n in idx], axis=0)
    row = jnp.arange(SUBLANES)[:, None]
    shift_pows = jnp.stack([jnp.where(row >= s, cat((s - 1,)), 0.0) for s in (2, 4)])
    wbr, wbi = wb[:, :ar.shape[0]], wb[:, ar.shape[0]:]
    wb_lag = jnp.concatenate([ar * wbr - ai * wbi, ar * wbi + ai * wbr], axis=1)
    wb2 = jnp.concatenate([wb, wb_lag], axis=0)
    return wb2.astype(BF16), shift_pows, cat(range(SUBLANES)), wc.astype(BF16)


def _rel_bias_table(table, n_rows, n_cols, offset):
    tab = table.astype(F32).T
    rel_min, rel_max = offset - (n_cols - 1), offset + n_rows - 1
    lo, hi = max(rel_min, -REL_CLIP), min(rel_max, REL_CLIP)
    parts = [jnp.repeat(tab[:, :1], lo - rel_min, axis=1), tab[:, lo + REL_CLIP:hi + REL_CLIP + 1],
             jnp.repeat(tab[:, -1:], rel_max - hi, axis=1)]
    ext = jnp.concatenate(parts, axis=1)
    length = n_rows + n_cols - 1
    flipped = jnp.pad(ext[:, ::-1], ((0, 0), (0, 1)))
    shifted = jnp.tile(flipped, (1, n_rows))[:, :n_rows * length].reshape(-1, n_rows, length)
    return shifted[:, :, n_rows - 1:n_rows - 1 + n_cols]


def _clipped_bias_run(tab, rel_first, count):
    n_hi = min(max(rel_first - REL_CLIP, 0), count)
    n_lo = min(max(-REL_CLIP - (rel_first - count + 1), 0), count)
    mid = count - n_hi - n_lo
    top = min(rel_first, REL_CLIP) + REL_CLIP
    return jnp.concatenate([jnp.repeat(tab[:, -1:], n_hi, axis=1), tab[:, top - mid + 1:top + 1][:, ::-1],
                            jnp.repeat(tab[:, :1], n_lo, axis=1)], axis=1)


def _band_bias_kernel(f_ref, o_ref, *, qp, span):
    pieces, width = o_ref.shape[1], o_ref.shape[3]
    bias = pltpu.roll(jnp.broadcast_to(f_ref[...], (qp, f_ref.shape[1])), 0, 1, stride=1, stride_axis=0)[:, :width]
    shift = CHUNK.bit_length() - 1
    r_chunk = lax.broadcasted_iota(jnp.int32, (qp, width), 0) >> shift
    col = lax.broadcasted_iota(jnp.int32, (qp, width), 1)
    in_band = ((col >> shift) >= r_chunk) & ((col >> shift) <= r_chunk + N_PREV)
    for p in range(pieces):
        o_ref[0, p] = jnp.where(in_band & (col >= span - p * qp), bias, -jnp.inf)
        o_ref[1, p] = jnp.where(in_band, bias, -jnp.inf)


def _band_prompt_bias(tables, qb, qp):
    depth, _, nh = tables.shape
    span = N_PREV * CHUNK
    width = span + qp
    period = -(-(width + qp - 1) // LANES) * LANES
    tab = jnp.swapaxes(tables.astype(F32), 1, 2).reshape(depth * nh, -1)
    f = jnp.concatenate([_clipped_bias_run(tab, span, width), jnp.zeros((depth * nh, period - width - (qp - 1)), F32),
                         _clipped_bias_run(tab, span + qp - 1, qp - 1)], axis=1).reshape(depth, nh, 1, period)
    pieces = qb // qp
    return pl.pallas_call(
        functools.partial(_band_bias_kernel, qp=qp, span=span),
        grid=(depth, nh),
        in_specs=[pl.BlockSpec((None, None, 1, period), lambda l, h: (l, h, 0, 0))],
        out_specs=pl.BlockSpec((None, 2, pieces, None, qp, width), lambda l, h: (l, 0, 0, h, 0, 0)),
        out_shape=jax.ShapeDtypeStruct((depth, 2, pieces, nh, qp, width), F32),
        compiler_params=_params("parallel", "parallel"),
        name="band_bias",
    )(f)


def _trunk_layer(x, mem_k, mem_v, gdn_conv, gdn_s, s5_h, rg_conv, rg_h, band_k, band_v, p, *, alpha):
    b, l, d = x.shape
    dg = d // N_MIX
    m = b * l
    layer = p["layer"]
    pa, pb, pc, pd, pdb = _inproj(x.reshape(m, d), p["w_in"], layer, (4 * dg, dg, 2 * dg, 3 * dg, LANES))
    pa, pb, pc, pd, pdb = [t.reshape(b, l, -1) for t in (pa, pb, pc, pd, pdb)]

    pad8 = lambda buf: jnp.pad(buf, ((0, 0), (SUBLANES - (CONV_W - 1), 0), (0, 0)))
    chunk = CHUNK if l % CHUNK == 0 else l
    cps = max(1, min(GDN_CHUNKS_PER_STEP, l // chunk))
    o_a, gdn_s_new = _gdn(pa, pdb, pad8(gdn_conv), gdn_s, p["gdn"], layer, t=chunk, cps=cps)
    gdn_conv_new = pa[:, l - (CONV_W - 1):, :3 * dg]

    ns = p["s5"][0].shape[2] // 2
    h0 = jnp.concatenate([s5_h[..., 0].reshape(b, 1, ns), s5_h[..., 1].reshape(b, 1, ns)], axis=-1)
    o_b, h_last = _s5(pb, h0, p["s5"], layer, t=_tile(l, SCAN_ROWS_PER_STEP))
    s5_h_new = jnp.stack([h_last[:, 0, :ns].reshape(s5_h.shape[:-1]), h_last[:, 0, ns:].reshape(s5_h.shape[:-1])],
                         axis=-1)

    o_c, rg_last = _rglru(pc, pad8(rg_conv), rg_h[:, None, :], p["rglru"], layer, t=_tile(l, SCAN_ROWS_PER_STEP))
    rg_conv_new = pc[:, l - (CONV_W - 1):, :dg]
    rg_h_new = rg_last[:, 0, :]

    hd = dg // H_D
    if band_k is None:
        qb = N_PREV * CHUNK
        o_d = _band_prompt(pd, p["band_bias_prompt"], layer, qb=qb, qp=BAND_PIECE)
        keep = min(N_PREV * CHUNK, l)
    else:
        o_d = _band_sample(pd, band_k, band_v, layer, p["band_bias_cache"], p["band_bias_new"])
        keep = l
    band_k_new = pd[:, l - keep:, dg:2 * dg].reshape(b, keep, H_D, hd)
    band_v_new = pd[:, l - keep:, 2 * dg:].reshape(b, keep, H_D, hd)

    x3 = _mix_xattn(x, (o_a, o_b, o_c, o_d), p["w_out"], mem_k, mem_v, p["xa_w_q"], p["xa_w_o"], layer,
                    p["ln_g"], p["ln_b"], alpha=alpha)
    x4 = _mlp(x3.reshape(m, d), p["mlp_w1"], p["mlp_w2"], layer, p["ln_g"], p["ln_b"], alpha=alpha)
    return x4.reshape(b, l, d), (gdn_conv_new, gdn_s_new, s5_h_new, rg_conv_new, rg_h_new, band_k_new, band_v_new)


def kernel(x_prompt, x_sample, state_gdn_conv, state_gdn, state_s5, state_rglru_conv, state_rglru, cache_band_k, cache_band_v, cache_mem_k, cache_mem_v, mem_prompt, w_in, w_out, ln_g, ln_b, gdn_conv_w, gdn_conv_b, gdn_a_log, gdn_dt_bias, gdn_norm_g, s5_lam_re, s5_lam_im, s5_log_dt, s5_b_re, s5_b_im, s5_c_re, s5_c_im, s5_d, s5_w_glu, s5_b_glu, rg_conv_w, rg_conv_b, rg_w_r, rg_b_r, rg_w_i, rg_b_i, rg_lam, band_rel_bias, xa_w_q, xa_w_k, xa_w_v, xa_w_o, mlp_w1, mlp_w2):
    depth = w_in.shape[0]
    bp, lp, d = x_prompt.shape
    bs, ls, _ = x_sample.shape
    n_mem = mem_prompt.shape[1]
    dg = d // N_MIX
    hd_x = d // H_X
    alpha = (2.0 * depth) ** 0.25
    band_rows = cache_band_k.shape[2]

    sizes = (3 * dg, dg, H_A, H_A, dg, dg, dg, 3 * dg)
    offs = [0]
    for s in sizes:
        offs.append(offs[-1] + s)

    w_db = jnp.pad(w_in[:, :, offs[2]:offs[4]], ((0, 0), (0, 0), (0, LANES - 2 * H_A)))
    w_in_bf = jnp.concatenate([w_in[:, :, offs[0]:offs[2]], w_in[:, :, offs[4]:offs[5]], w_in[:, :, offs[5]:offs[7]],
                               w_in[:, :, offs[7]:offs[8]], w_db], axis=2).astype(BF16)
    w_out_bf, xa_w_q_bf, xa_w_k_bf, xa_w_v_bf, xa_w_o_bf, mlp_w1_bf, mlp_w2_bf = [
        w.astype(BF16) for w in (w_out, xa_w_q, xa_w_k, xa_w_v, xa_w_o, mlp_w1, mlp_w2)]

    band_kt = jnp.transpose(cache_band_k, (0, 1, 3, 4, 2))
    band_vt = jnp.transpose(cache_band_v, (0, 1, 3, 4, 2))

    over_layers = jax.vmap
    s5_wb, s5_ap, s5_p8, s5_wc = over_layers(_s5_params)(s5_lam_re, s5_lam_im, s5_log_dt, s5_b_re, s5_b_im,
                                                         s5_c_re, s5_c_im)
    rg_wri = jnp.concatenate([over_layers(_block_diag)(rg_w_r), over_layers(_block_diag)(rg_w_i)], axis=2)
    shared = {
        "w_in": w_in_bf, "w_out": w_out_bf, "ln_g": ln_g, "ln_b": ln_b,
        "gdn": (gdn_conv_w, gdn_conv_b[:, None, :], over_layers(_pad_lanes)(gdn_a_log),
                over_layers(_pad_lanes)(gdn_dt_bias), gdn_norm_g[:, None, :]),
        "s5": (s5_wb, s5_ap, s5_p8, s5_wc, s5_d.reshape(depth, 1, dg), s5_w_glu.astype(BF16), s5_b_glu[:, None, :]),
        "rglru": (rg_conv_w, rg_conv_b[:, None, :], rg_wri.astype(BF16),
                  jnp.concatenate([rg_b_r, rg_b_i], axis=1)[:, None, :], rg_lam[:, None, :]),
        "band_bias_prompt": _band_prompt_bias(band_rel_bias, N_PREV * CHUNK, BAND_PIECE),
        "band_bias_cache": over_layers(lambda t: _rel_bias_table(t, ls, band_rows, band_rows))(band_rel_bias),
        "band_bias_new": over_layers(lambda t: _rel_bias_table(t, ls, ls, 0))(band_rel_bias),
        "xa_w_q": xa_w_q_bf, "xa_w_o": xa_w_o_bf, "mlp_w1": mlp_w1_bf, "mlp_w2": mlp_w2_bf,
    }

    xp, xs = x_prompt, x_sample
    p_states, s_states = [], []
    for l in range(depth):
        p = dict(shared, layer=l)
        mem2 = mem_prompt.reshape(bp * n_mem, d)
        mk = _matmul(mem2, xa_w_k_bf, l).reshape(bp, n_mem, d)
        mv = _matmul(mem2, xa_w_v_bf, l).reshape(bp, n_mem, d)
        xp, st_p = _trunk_layer(
            xp, mk, mv,
            jnp.zeros((bp, CONV_W - 1, 3 * dg), F32), jnp.zeros((bp, H_A, dg // H_A, dg // H_A), F32),
            jnp.zeros((bp, dg // S5_CH, P_B, 2), F32), jnp.zeros((bp, CONV_W - 1, dg), F32),
            jnp.zeros((bp, dg), F32), None, None, p, alpha=alpha)
        p_states.append(st_p + (mk.reshape(bp, n_mem, H_X, hd_x), mv.reshape(bp, n_mem, H_X, hd_x)))
        xs, st_s = _trunk_layer(
            xs, cache_mem_k, cache_mem_v,
            state_gdn_conv[l], state_gdn[l], state_s5[l], state_rglru_conv[l], state_rglru[l],
            band_kt, band_vt, p, alpha=alpha)
        s_states.append(st_s)

    def stk(states, i):
        return jnp.stack([st[i] for st in states], axis=0)

    return (xp, xs,
            stk(p_states, 0), stk(p_states, 1), stk(p_states, 2), stk(p_states, 3), stk(p_states, 4),
            stk(p_states, 5), stk(p_states, 6), stk(p_states, 7), stk(p_states, 8),
            stk(s_states, 0), stk(s_states, 1), stk(s_states, 2), stk(s_states, 3), stk(s_states, 4),
            stk(s_states, 5), stk(s_states, 6))
```

```python
import functools
import math

import jax
import jax.numpy as jnp
from jax import lax
from jax.experimental import pallas as pl
from jax.experimental.pallas import tpu as pltpu

F32 = jnp.float32
BF16 = jnp.bfloat16
HIGHEST = lax.Precision.HIGHEST

N_MIX = 4
CONV_W = 4
CHUNK = 64
H_A = 4
S5_CH = 16
P_B = 64
H_C = 4
RG_C = 8.0
H_D = 4
N_PREV = 8
REL_CLIP = 128
H_X = 4
LN_EPS = 1e-5
NORM_EPS = 1e-6

LANES = 128
SUBLANES = 8
VMEM_LIMIT_BYTES = 56 * 1024 * 1024
GDN_BLOCK = 128
GDN_CHUNKS_PER_STEP = 8
BAND_PIECE = 256
SCAN_ROWS_PER_STEP = 1024
XATTN_SEQS_PER_STEP = 4


def _params(*semantics):
    return pltpu.CompilerParams(dimension_semantics=semantics, vmem_limit_bytes=VMEM_LIMIT_BYTES)


def _tile(n, pref):
    t = min(n, pref)
    while n % t:
        t -= SUBLANES
    return t


def _seqs_per_step(b, l, t):
    nb = max(1, min(b, SCAN_ROWS_PER_STEP // t)) if l == t else 1
    while b % nb:
        nb -= 1
    return nb


def _mm(a, b):
    return jnp.dot(a.astype(BF16), b.astype(BF16), preferred_element_type=F32)


def _mm_nt(a, b):
    return lax.dot_general(a.astype(BF16), b.astype(BF16), (((1,), (1,)), ((), ())),
                           preferred_element_type=F32)


def _mm_tn(a, b):
    return lax.dot_general(a.astype(BF16), b.astype(BF16), (((0,), (0,)), ((), ())),
                           preferred_element_type=F32)


def _mm_f32(a, b):
    return jnp.dot(a, b, precision=HIGHEST, preferred_element_type=F32)


def _sigmoid(x):
    return 1.0 / (1.0 + jnp.exp(-x))


def _softplus(x):
    return jnp.maximum(x, 0.0) + jnp.log1p(jnp.exp(-jnp.abs(x)))


def _gelu_tanh(x):
    c = math.sqrt(2.0 / math.pi)
    return 0.5 * x * (1.0 + jnp.tanh(c * (x + 0.044715 * (x * x * x))))


def _layer_norm(z, g, b):
    mu = jnp.mean(z, axis=-1, keepdims=True)
    zc = z - mu
    var = jnp.mean(zc * zc, axis=-1, keepdims=True)
    return zc * lax.rsqrt(var + LN_EPS) * g + b


def _matmul_kernel(x_ref, w_ref, o_ref):
    o_ref[...] = _mm(x_ref[...], w_ref[...])


def _layer_spec(w, layer, block=None, index=None):
    block = tuple(w.shape[1:]) if block is None else block
    index = (lambda *_: (0,) * len(block)) if index is None else index
    return pl.BlockSpec((None,) + block, lambda *g: (layer,) + tuple(index(*g)))


def _matmul(x, w_bf16, layer, tm=512):
    m, k = x.shape
    n = w_bf16.shape[2]
    tm = _tile(m, tm)
    return pl.pallas_call(
        _matmul_kernel,
        grid=(m // tm,),
        in_specs=[pl.BlockSpec((tm, k), lambda i: (i, 0)), _layer_spec(w_bf16, layer)],
        out_specs=pl.BlockSpec((tm, n), lambda i: (i, 0)),
        out_shape=jax.ShapeDtypeStruct((m, n), F32),
        compiler_params=_params("parallel"),
        name="matmul",
    )(x, w_bf16)


def _inproj_kernel(x_ref, w_ref, *o_refs, bounds):
    xb = x_ref[...].astype(BF16)
    for o_ref, (s, e) in zip(o_refs, bounds):
        o_ref[...] = jnp.dot(xb, w_ref[:, s:e], preferred_element_type=F32)


def _inproj(x, w_bf16, layer, widths, tm=1024):
    m, k = x.shape
    tm = _tile(m, tm)
    bounds, s = [], 0
    for w in widths:
        bounds.append((s, s + w))
        s += w
    return pl.pallas_call(
        functools.partial(_inproj_kernel, bounds=tuple(bounds)),
        grid=(m // tm,),
        in_specs=[pl.BlockSpec((tm, k), lambda i: (i, 0)), _layer_spec(w_bf16, layer)],
        out_specs=[pl.BlockSpec((tm, w), lambda i: (i, 0)) for w in widths],
        out_shape=[jax.ShapeDtypeStruct((m, w), F32) for w in widths],
        compiler_params=_params("parallel"),
        name="inproj",
    )(x, w_bf16)


def _inverse_masks(r, c, t):
    neg_diag8 = jnp.where((r >> 3) == (c >> 3), -1.0, 0.0)
    offs, lb = [], 3
    while (1 << lb) < t:
        off = ((r >> (lb + 1)) == (c >> (lb + 1))) & (((r >> lb) & 1) == 1) & (((c >> lb) & 1) == 0)
        offs.append(jnp.where(off, 1.0, 0.0))
        lb += 1
    return neg_diag8, offs


def _unit_lower_inverse_offdiag(a_list, masks):
    neg_diag8, offs = masks
    n1 = [a * neg_diag8 for a in a_list]
    n2 = [_mm(x, x) for x in n1]
    n3 = [_mm(x, x2) for x, x2 in zip(n1, n2)]
    n4 = [_mm(x2, x2) for x2 in n2]
    p = [x + x2 + x3 for x, x2, x3 in zip(n1, n2, n3)]
    pn4 = [_mm(pp, x4) for pp, x4 in zip(p, n4)]
    y = [pp + x4 + px for pp, x4, px in zip(p, n4, pn4)]
    for off in offs:
        m = [a * off for a in a_list]
        z = [mm + _mm(yy, mm) for yy, mm in zip(y, m)]
        zy = [_mm(zz, yy) for zz, yy in zip(z, y)]
        y = [yy - (zz + zzy) for yy, zz, zzy in zip(y, z, zy)]
    return y


def _split3(x):
    h1 = x.astype(BF16)
    r1 = x - h1.astype(F32)
    h2 = r1.astype(BF16)
    h3 = (r1 - h2.astype(F32)).astype(BF16)
    return h1, h2, h3


def _gdn_kernel(qkv_ref, gate_ref, db_ref, cbuf_ref, s0_ref, cw_ref, cb_ref, alog_ref, dtb_ref, ng_ref,
                o_ref, sfin_ref, xp_scr, s_scr, o_scr, *, t, dk):
    i = pl.program_id(1)
    nb, rows, _ = qkv_ref.shape
    tb = nb * rows
    nh = H_A
    dq = nh * dk

    @pl.when(i == 0)
    def _():
        xp_scr[:, 0:SUBLANES, :] = cbuf_ref[...]
        s_scr[...] = s0_ref[...]

    base = SUBLANES - (CONV_W - 1)
    ys = []
    for n in range(nb):
        x = qkv_ref[n]
        xp_scr[n, SUBLANES:SUBLANES + rows, :] = x
        y = xp_scr[n, base:base + rows, :] * cw_ref[0:1, :]
        for j in range(1, CONV_W):
            y = y + xp_scr[n, base + j:base + j + rows, :] * cw_ref[j:j + 1, :]
        xp_scr[n, 0:SUBLANES, :] = x[rows - SUBLANES:rows, :]
        ys.append(y + cb_ref[...])
    y = jnp.concatenate(ys, axis=0)
    y = y * _sigmoid(y)

    db = db_ref[...].reshape(tb, LANES)
    log_a = -jnp.exp(alog_ref[...]) * _softplus(db + dtb_ref[...])
    beta_all = _sigmoid(db)

    bs = min(tb, GDN_BLOCK)
    lt = t.bit_length() - 1
    r = lax.broadcasted_iota(jnp.int32, (bs, bs), 0)
    c = lax.broadcasted_iota(jnp.int32, (bs, bs), 1)
    same = (r >> lt) == (c >> lt)
    causal_neg = jnp.where(same & (r >= c), 0.0, -jnp.inf)
    strict_f = jnp.where(same & (r > c), 1.0, 0.0)
    tril = jnp.where(same & (r >= c), 1.0, 0.0).astype(BF16)
    striu = jnp.where(same & (r < c), 1.0, 0.0).astype(BF16)
    inv_masks = _inverse_masks(r, c, t)

    blocks = list(range(0, tb, bs))
    pairs = [(bi, h) for bi in range(len(blocks)) for h in range(nh)]
    la3 = [_split3(log_a[b0:b0 + bs, :]) for b0 in blocks]
    gc = [sum(jnp.dot(tril, part, preferred_element_type=F32) for part in parts) for parts in la3]
    rv = [sum(jnp.dot(striu, part, preferred_element_type=F32) for part in parts) for parts in la3]
    eg = [jnp.exp(g) for g in gc]
    erv = [jnp.exp(g) for g in rv]
    e_tot = [jnp.exp(g + g2) for g, g2 in zip(gc, rv)]
    gc_rows = [g.T for g in gc]

    def head_cols(z, bi, h, off):
        b0 = blocks[bi]
        return z[b0:b0 + bs, off + h * dk:off + (h + 1) * dk]

    q = [head_cols(y, bi, h, 0) for bi, h in pairs]
    k = [head_cols(y, bi, h, dq) for bi, h in pairs]
    v = [head_cols(y, bi, h, 2 * dq) for bi, h in pairs]
    q = [z * lax.rsqrt(jnp.sum(z * z, axis=-1, keepdims=True) + NORM_EPS) * (dk ** -0.5) for z in q]
    k = [z * lax.rsqrt(jnp.sum(z * z, axis=-1, keepdims=True) + NORM_EPS) for z in k]
    decay = [jnp.exp(gc[bi][:, h:h + 1] - gc_rows[bi][h:h + 1, :] + causal_neg) for bi, h in pairs]
    beta = [beta_all[blocks[bi]:blocks[bi] + bs, nh + h:nh + h + 1] for bi, h in pairs]
    eg_col = [eg[bi][:, h:h + 1] for bi, h in pairs]
    kk = [_mm_nt(z, z) for z in k]
    qk = [_mm_nt(zq, zk) for zq, zk in zip(q, k)]
    a = [(b * z * d) * strict_f for b, z, d in zip(beta, kk, decay)]
    qk = [z * d for z, d in zip(qk, decay)]
    y_inv = _unit_lower_inverse_offdiag(a, inv_masks)
    rhs = [jnp.concatenate([zv * b, zk * (b * e)], axis=-1) for zv, zk, b, e in zip(v, k, beta, eg_col)]
    sol = [z + _mm(yi, z) for yi, z in zip(y_inv, rhs)]
    q_dec = [z * e for z, e in zip(q, eg_col)]
    k_dec = [z * erv[bi][:, h:h + 1] for z, (bi, h) in zip(k, pairs)]

    chunks = [(bi, r0) for bi in range(len(blocks)) for r0 in range(0, bs, t)]
    n_kw = [[_mm_tn(k_dec[bi * nh + h][r0:r0 + t], sol[bi * nh + h][r0:r0 + t]) for h in range(nh)]
            for bi, r0 in chunks]
    states = [s_scr[0, h] for h in range(nh)]
    starts = []
    for ci, ((bi, r0), nk) in enumerate(zip(chunks, n_kw)):
        if nb > 1:
            states = [s_scr[ci, h] for h in range(nh)]
        starts.append(states)
        drop = [_mm(z[:, dk:], s) for z, s in zip(nk, states)]
        states = [s * e_tot[bi][r0:r0 + 1, h:h + 1] + (z[:, :dk] - d)
                  for h, (s, z, d) in enumerate(zip(states, nk, drop))]
        if nb > 1:
            for h in range(nh):
                s_scr[ci, h] = states[h]
    ws = [[_mm(jnp.concatenate([sol[bi * nh + h][r0:r0 + t, dk:], q_dec[bi * nh + h][r0:r0 + t]], axis=0), st[h])
           for h in range(nh)] for (bi, r0), st in zip(chunks, starts)]
    v_news = [[] for _ in pairs]
    o_inter = [[] for _ in pairs]
    for (bi, r0), wc in zip(chunks, ws):
        for h, z in enumerate(wc):
            v_news[bi * nh + h].append(sol[bi * nh + h][r0:r0 + t, :dk] - z[:t])
            o_inter[bi * nh + h].append(z[t:])
    o = [jnp.concatenate(oi, axis=0) + _mm(z, jnp.concatenate(vn, axis=0))
         for oi, z, vn in zip(o_inter, qk, v_news)]
    o = [z * lax.rsqrt(jnp.mean(z * z, axis=-1, keepdims=True) + NORM_EPS) * ng_ref[...] for z in o]
    for (bi, h), z in zip(pairs, o):
        o_scr[blocks[bi]:blocks[bi] + bs, h * dk:(h + 1) * dk] = z
    if nb == 1:
        for h in range(nh):
            s_scr[0, h] = states[h]

    g = gate_ref[...].reshape(tb, dq)
    o_ref[...] = (o_scr[...] * (g * _sigmoid(g))).astype(o_ref.dtype).reshape(nb, rows, dq)

    @pl.when(i == pl.num_programs(1) - 1)
    def _():
        sfin_ref[...] = s_scr[...]


def _gdn(pa, pdb, cbuf8, s0, params, layer, *, t, cps):
    b, l, _ = pa.shape
    nh, dk = s0.shape[1], s0.shape[2]
    dq = nh * dk
    nb = _seqs_per_step(b, l, t)
    rows = t * cps
    return pl.pallas_call(
        functools.partial(_gdn_kernel, t=t, dk=dk),
        grid=(b // nb, l // rows),
        in_specs=[
            pl.BlockSpec((nb, rows, 3 * dq), lambda bi, i: (bi, i, 0)),
            pl.BlockSpec((nb, rows, dq), lambda bi, i: (bi, i, 3)),
            pl.BlockSpec((nb, rows, LANES), lambda bi, i: (bi, i, 0)),
            pl.BlockSpec((nb, SUBLANES, 3 * dq), lambda bi, i: (bi, 0, 0)),
            pl.BlockSpec((nb, nh, dk, dk), lambda bi, i: (bi, 0, 0, 0)),
        ] + [_layer_spec(w, layer) for w in params],
        out_specs=[
            pl.BlockSpec((nb, rows, dq), lambda bi, i: (bi, i, 0)),
            pl.BlockSpec((nb, nh, dk, dk), lambda bi, i: (bi, 0, 0, 0)),
        ],
        out_shape=[jax.ShapeDtypeStruct((b, l, dq), BF16), jax.ShapeDtypeStruct((b, nh, dk, dk), F32)],
        scratch_shapes=[pltpu.VMEM((nb, SUBLANES + rows, 3 * dq), F32), pltpu.VMEM((nb, nh, dk, dk), F32),
                        pltpu.VMEM((nb * rows, dq), F32)],
        compiler_params=_params("parallel", "arbitrary"),
        name="gdn",
    )(pa, pa, pdb, cbuf8, s0, *params)


def _s5_kernel(u_ref, h0_ref, wb_ref, ap_ref, p8_ref, wc_ref, d_ref, wg_ref, bg_ref,
               o_ref, hl_ref, carry_scr, h_scr, *, t, ns):
    i = pl.program_id(1)
    nb, _, dg = u_ref.shape

    @pl.when(i == 0)
    def _():
        carry_scr[...] = h0_ref[...]

    u = u_ref[...].reshape(nb * t, dg)
    ng = nb * t // SUBLANES
    first = lax.broadcasted_iota(jnp.int32, (SUBLANES, dg), 0) == 0
    u_prev = jnp.where(first, 0.0, pltpu.roll(u.reshape(ng, SUBLANES, dg), 1, 1)).reshape(nb * t, dg)
    x = _mm(jnp.concatenate([u, u_prev], axis=-1), wb_ref[...])
    xr = x[:, :ns].reshape(ng, SUBLANES, ns)
    xi = x[:, ns:].reshape(ng, SUBLANES, ns)
    for lvl in range(2):
        s = 2 << lvl
        pr, pi = ap_ref[lvl, :, :ns], ap_ref[lvl, :, ns:]
        sr = pltpu.roll(xr, s, 1)
        si = pltpu.roll(xi, s, 1)
        xr, xi = xr + (pr * sr - pi * si), xi + (pr * si + pi * sr)
    p8r, p8i = p8_ref[:, :ns], p8_ref[:, ns:]
    for n in range(nb):
        cr, ci = carry_scr[n, :, :ns], carry_scr[n, :, ns:]
        for j in range(n * t // SUBLANES, (n + 1) * t // SUBLANES):
            sl = slice(j * SUBLANES, (j + 1) * SUBLANES)
            br = xr[j] + (p8r * cr - p8i * ci)
            bi = xi[j] + (p8r * ci + p8i * cr)
            h_scr[sl, :ns] = br
            h_scr[sl, ns:] = bi
            cr, ci = br[SUBLANES - 1:SUBLANES], bi[SUBLANES - 1:SUBLANES]
        carry_scr[n, :, :ns] = cr
        carry_scr[n, :, ns:] = ci
    hl_ref[...] = carry_scr[...]

    y = _mm(h_scr[...], wc_ref[...]) + d_ref[...] * u
    y = _gelu_tanh(y)
    z = _mm(y, wg_ref[...]) + bg_ref[...]
    o_ref[...] = (y * _sigmoid(z)).astype(o_ref.dtype).reshape(nb, t, dg)


def _s5(pb, h0, params, layer, *, t):
    b, l, dg = pb.shape
    ns2 = params[0].shape[2]
    ns = ns2 // 2
    nb = _seqs_per_step(b, l, t)
    return pl.pallas_call(
        functools.partial(_s5_kernel, t=t, ns=ns),
        grid=(b // nb, l // t),
        in_specs=[
            pl.BlockSpec((nb, t, dg), lambda bi, i: (bi, i, 0)),
            pl.BlockSpec((nb, 1, ns2), lambda bi, i: (bi, 0, 0)),
        ] + [_layer_spec(w, layer) for w in params],
        out_specs=[
            pl.BlockSpec((nb, t, dg), lambda bi, i: (bi, i, 0)),
            pl.BlockSpec((nb, 1, ns2), lambda bi, i: (bi, 0, 0)),
        ],
        out_shape=[jax.ShapeDtypeStruct((b, l, dg), BF16), jax.ShapeDtypeStruct((b, 1, ns2), F32)],
        scratch_shapes=[pltpu.VMEM((nb, 1, ns2), F32), pltpu.VMEM((nb * t, ns2), F32)],
        compiler_params=_params("parallel", "arbitrary"),
        name="s5",
    )(pb, h0, *params)


def _rglru_kernel(xg_ref, cbuf_ref, h0_ref, cw_ref, cb_ref, wri_ref, bri_ref, lam_ref,
                  o_ref, hl_ref, xp_scr, carry_scr, h_scr, *, t, dg):
    i = pl.program_id(1)
    nb = xg_ref.shape[0]

    @pl.when(i == 0)
    def _():
        xp_scr[:, 0:SUBLANES, :] = cbuf_ref[...]
        carry_scr[...] = h0_ref[...]

    base = SUBLANES - (CONV_W - 1)
    ys = []
    for n in range(nb):
        x = xg_ref[n, :, :dg]
        xp_scr[n, SUBLANES:SUBLANES + t, :] = x
        y = xp_scr[n, base:base + t, :] * cw_ref[0:1, :]
        for j in range(1, CONV_W):
            y = y + xp_scr[n, base + j:base + j + t, :] * cw_ref[j:j + 1, :]
        xp_scr[n, 0:SUBLANES, :] = x[t - SUBLANES:t, :]
        ys.append(y + cb_ref[...])
    y = jnp.concatenate(ys, axis=0)
    gb = xg_ref[...][:, :, dg:].reshape(nb * t, dg)

    ri = _mm(y, wri_ref[...]) + bri_ref[...]
    rg = _sigmoid(ri[:, :dg])
    ig = _sigmoid(ri[:, dg:])
    log_a = (-RG_C * rg) * _softplus(-lam_ref[...])
    a = jnp.exp(log_a)
    th = jnp.tanh(log_a)
    xin = jnp.sqrt(-2.0 * th / (1.0 - th)) * (ig * y)

    ng = nb * t // SUBLANES
    a = a.reshape(ng, SUBLANES, dg)
    xin = xin.reshape(ng, SUBLANES, dg)
    row = lax.broadcasted_iota(jnp.int32, (SUBLANES, dg), 0)
    for lvl in range(3):
        s = 1 << lvl
        keep = row >= s
        a_s = jnp.where(keep, pltpu.roll(a, s, 1), 1.0)
        x_s = jnp.where(keep, pltpu.roll(xin, s, 1), 0.0)
        xin = a * x_s + xin
        a = a * a_s
    for n in range(nb):
        cr = carry_scr[n]
        for j in range(n * t // SUBLANES, (n + 1) * t // SUBLANES):
            hb = xin[j] + a[j] * cr
            h_scr[j * SUBLANES:(j + 1) * SUBLANES, :] = hb
            cr = hb[SUBLANES - 1:SUBLANES]
        carry_scr[n] = cr
    hl_ref[...] = carry_scr[...]
    o_ref[...] = (h_scr[...] * _gelu_tanh(gb)).astype(o_ref.dtype).reshape(nb, t, dg)


def _rglru(pc, cbuf8, h0, params, layer, *, t):
    b, l, dg2 = pc.shape
    dg = dg2 // 2
    nb = _seqs_per_step(b, l, t)
    return pl.pallas_call(
        functools.partial(_rglru_kernel, t=t, dg=dg),
        grid=(b // nb, l // t),
        in_specs=[
            pl.BlockSpec((nb, t, dg2), lambda bi, i: (bi, i, 0)),
            pl.BlockSpec((nb, SUBLANES, dg), lambda bi, i: (bi, 0, 0)),
            pl.BlockSpec((nb, 1, dg), lambda bi, i: (bi, 0, 0)),
        ] + [_layer_spec(w, layer) for w in params],
        out_specs=[
            pl.BlockSpec((nb, t, dg), lambda bi, i: (bi, i, 0)),
            pl.BlockSpec((nb, 1, dg), lambda bi, i: (bi, 0, 0)),
        ],
        out_shape=[jax.ShapeDtypeStruct((b, l, dg), BF16), jax.ShapeDtypeStruct((b, 1, dg), F32)],
        scratch_shapes=[pltpu.VMEM((nb, SUBLANES + t, dg), F32), pltpu.VMEM((nb, 1, dg), F32),
                        pltpu.VMEM((nb * t, dg), F32)],
        compiler_params=_params("parallel", "arbitrary"),
        name="rglru",
    )(pc, cbuf8, h0, *params)


def _band_prompt_kernel(q_ref, kc_ref, vc_ref, kp_ref, vp_ref, bias_ref, o_ref, *, qb, qp, hd):
    span = N_PREV * CHUNK
    q = q_ref[0] * (hd ** -0.5)
    k = jnp.concatenate([kp_ref[0], kc_ref[0]], axis=0)
    v = jnp.concatenate([vp_ref[0], vc_ref[0]], axis=0)
    units = [(h, p) for h in range(H_D) for p in range(qb // qp)]
    col = lambda h: slice(h * hd, (h + 1) * hd)
    win = lambda p: slice(qb - span + p * qp, qb + (p + 1) * qp)
    s, e, den, o = {}, {}, {}, []
    for n in range(len(units) + 2):
        if n < len(units):
            h, p = units[n]
            s[n] = _mm_nt(q[p * qp:(p + 1) * qp, col(h)], k[win(p), col(h)]) + bias_ref[0, p, h]
        if 0 <= n - 1 < len(units):
            z = s.pop(n - 1)
            e[n - 1] = jnp.exp(z - jnp.max(z, axis=-1, keepdims=True))
            den[n - 1] = jnp.sum(e[n - 1], axis=-1, keepdims=True)
        if 0 <= n - 2 < len(units):
            h, p = units[n - 2]
            o.append(_mm(e.pop(n - 2), v[win(p), col(h)]) / den.pop(n - 2))
    npc = qb // qp
    o_ref[0] = jnp.concatenate([jnp.concatenate(o[h * npc:(h + 1) * npc], axis=0) for h in range(H_D)],
                               axis=-1).astype(o_ref.dtype)


def _band_prompt(pd, bias, layer, *, qb, qp):
    b, l, w3 = pd.shape
    w = w3 // 3
    hd = w // H_D
    prev = lambda bi, i: jnp.maximum(i - 1, 0)
    return pl.pallas_call(
        functools.partial(_band_prompt_kernel, qb=qb, qp=qp, hd=hd),
        grid=(b, l // qb),
        in_specs=[
            pl.BlockSpec((1, qb, w), lambda bi, i: (bi, i, 0)),
            pl.BlockSpec((1, qb, w), lambda bi, i: (bi, i, 1)),
            pl.BlockSpec((1, qb, w), lambda bi, i: (bi, i, 2)),
            pl.BlockSpec((1, qb, w), lambda bi, i: (bi, prev(bi, i), 1)),
            pl.BlockSpec((1, qb, w), lambda bi, i: (bi, prev(bi, i), 2)),
            pl.BlockSpec((None, 1) + bias.shape[2:], lambda bi, i: (layer, jnp.minimum(i, 1), 0, 0, 0, 0)),
        ],
        out_specs=pl.BlockSpec((1, qb, w), lambda bi, i: (bi, i, 0)),
        out_shape=jax.ShapeDtypeStruct((b, l, w), BF16),
        compiler_params=_params("parallel", "arbitrary"),
        name="band_prompt",
    )(pd, pd, pd, pd, pd, bias)


def _band_sample_kernel(qkv_ref, kt_ref, vt_ref, bc_ref, bn_ref, o_ref, *, hd):
    nb = qkv_ref.shape[0]
    w = H_D * hd
    units = [(n, h, slice(h * hd, (h + 1) * hd)) for n in range(nb) for h in range(H_D)]
    q = [qkv_ref[n, :, :w] * (hd ** -0.5) for n in range(nb)]
    kn = [qkv_ref[n, :, w:2 * w] for n in range(nb)]
    vn = [qkv_ref[n, :, 2 * w:] for n in range(nb)]
    sc = [_mm(q[n][:, sl], kt_ref[n, h]) + bc_ref[h] for n, h, sl in units]
    sn = [_mm_nt(q[n][:, sl], kn[n][:, sl]) + bn_ref[h] for n, h, sl in units]
    m = [jnp.maximum(jnp.max(c, axis=-1, keepdims=True), jnp.max(z, axis=-1, keepdims=True)) for c, z in zip(sc, sn)]
    pc = [jnp.exp(c - z) for c, z in zip(sc, m)]
    pn = [jnp.exp(c - z) for c, z in zip(sn, m)]
    den = [jnp.sum(c, axis=-1, keepdims=True) + jnp.sum(z, axis=-1, keepdims=True) for c, z in zip(pc, pn)]
    outs = [(_mm_nt(c, vt_ref[n, h]) + _mm(z, vn[n][:, sl])) / d
            for (n, h, sl), c, z, d in zip(units, pc, pn, den)]
    for n in range(nb):
        o_ref[n] = jnp.concatenate(outs[n * H_D:(n + 1) * H_D], axis=-1).astype(o_ref.dtype)


def _band_sample(pd, kt_cache, vt_cache, layer, bias_c, bias_n):
    b, l, w3 = pd.shape
    w = w3 // 3
    hd = w // H_D
    rows = kt_cache.shape[-1]
    nb = XATTN_SEQS_PER_STEP
    while b % nb:
        nb -= 1
    cache_spec = pl.BlockSpec((None, nb, H_D, hd, rows), lambda bi: (layer, bi, 0, 0, 0))
    return pl.pallas_call(
        functools.partial(_band_sample_kernel, hd=hd),
        grid=(b // nb,),
        in_specs=[
            pl.BlockSpec((nb, l, w3), lambda bi: (bi, 0, 0)),
            cache_spec, cache_spec, _layer_spec(bias_c, layer), _layer_spec(bias_n, layer),
        ],
        out_specs=pl.BlockSpec((nb, l, w), lambda bi: (bi, 0, 0)),
        out_shape=jax.ShapeDtypeStruct((b, l, w), BF16),
        compiler_params=_params("parallel"),
        name="band_sample",
    )(pd, kt_cache, vt_cache, bias_c, bias_n)


def _row_parts(rows, parts):
    if rows % (parts * 2 * SUBLANES):
        parts = 1
    step = rows // parts
    return [slice(n * step, (n + 1) * step) for n in range(parts)]


def _mix_xattn_kernel(x_ref, oa_ref, ob_ref, oc_ref, od_ref, wm_ref, mk_ref, mv_ref, wq_ref, wo_ref, g_ref, b_ref,
                      o_ref, *, alpha, hd, head_axis):
    nb, tm, d = x_ref.shape
    dg = oa_ref.shape[2]
    heads = [slice(h * hd, (h + 1) * hd) for h in range(H_X)]
    if head_axis:
        n_mem = mk_ref.shape[1]
        mk = [mk_ref[n].reshape(n_mem * H_X, hd).astype(BF16) for n in range(nb)]
        mv = [mv_ref[n].reshape(n_mem * H_X, hd).astype(BF16) for n in range(nb)]
    else:
        mk = [[mk_ref[n, :, hs].astype(BF16) for hs in heads] for n in range(nb)]
        mv = [[mv_ref[n, :, hs].astype(BF16) for hs in heads] for n in range(nb)]
    if nb == 1:
        parts = _row_parts(tm, 2)
        rows_of = lambda ref, sl: ref[0, sl, :]
        units = [(n, slice(0, sl.stop - sl.start), 0) for n, sl in enumerate(parts)]
    else:
        parts = [slice(0, nb * tm)]
        rows_of = lambda ref, sl: ref[...].reshape(nb * tm, ref.shape[2])
        units = [(0, slice(n * tm, (n + 1) * tm), n) for n in range(nb)]
    mix = [sum(jnp.dot(rows_of(r, sl), wm_ref[n * dg:(n + 1) * dg, :], preferred_element_type=F32)
               for n, r in enumerate((oa_ref, ob_ref, oc_ref, od_ref))) for sl in parts]
    xs = [_layer_norm(alpha * rows_of(x_ref, sl) + z, g_ref[0:1, :], b_ref[0:1, :]) for sl, z in zip(parts, mix)]
    q = [(_mm(x, wq_ref[...]) * (hd ** -0.5)).astype(BF16) for x in xs]
    if head_axis:
        rows_u = units[0][1].stop - units[0][1].start
        row = lax.broadcasted_iota(jnp.int32, (H_X * rows_u, n_mem * H_X), 0)
        r_head = sum((row >= h * rows_u).astype(jnp.int32) for h in range(1, H_X))
        c_head = lax.broadcasted_iota(jnp.int32, (H_X * rows_u, n_mem * H_X), 1) & (H_X - 1)
        own_head = jnp.where(r_head == c_head, 0.0, -jnp.inf)
        qs = [jnp.concatenate([q[p][rows, hs] for hs in heads], axis=0) for p, rows, _ in units]
        s = [_mm_nt(z, mk[n]) + own_head for z, (_, _, n) in zip(qs, units)]
        e = [jnp.exp(z - jnp.max(z, axis=-1, keepdims=True)) for z in s]
        pr = [z * (1.0 / jnp.sum(z, axis=-1, keepdims=True)) for z in e]
        pv = [_mm(z, mv[n]) for z, (_, _, n) in zip(pr, units)]
        pv = [jnp.concatenate([z[h * rows_u:(h + 1) * rows_u] for h in range(H_X)], axis=-1) for z in pv]
    else:
        s = [[_mm_nt(q[p][rows, hs], kh) for hs, kh in zip(heads, mk[n])] for p, rows, n in units]
        e = [[jnp.exp(z - jnp.max(z, axis=-1, keepdims=True)) for z in su] for su in s]
        pr = [[z * (1.0 / jnp.sum(z, axis=-1, keepdims=True)) for z in eu] for eu in e]
        pv = [jnp.concatenate([_mm(z, vh) for z, vh in zip(pu, mv[n])], axis=-1) for pu, (_, _, n) in zip(pr, units)]
    pv = [jnp.concatenate([z for z, (p, _, _) in zip(pv, units) if p == n], axis=0) for n in range(len(parts))]
    att = [_mm(z, wo_ref[...]) for z in pv]
    out = [_layer_norm(alpha * x + z, g_ref[1:2, :], b_ref[1:2, :]) for x, z in zip(xs, att)]
    if nb == 1:
        for sl, z in zip(parts, out):
            o_ref[0, sl, :] = z
    else:
        o_ref[...] = out[0].reshape(nb, tm, d)


def _mix_xattn(x, mixed, w_out, mk, mv, wq, wo, layer, g, b, *, alpha, rows=1024):
    bsz, l, d = x.shape
    dg = mixed[0].shape[2]
    hd = d // H_X
    tm = _tile(l, rows)
    nb = max(1, min(bsz, XATTN_SEQS_PER_STEP, rows // l)) if tm == l else 1
    while bsz % nb:
        nb -= 1
    row = lambda width: pl.BlockSpec((nb, tm, width), lambda bi, i: (bi, i, 0))
    if mk.ndim == 3:
        mem_spec = pl.BlockSpec((nb,) + mk.shape[1:], lambda bi, i: (bi, 0, 0))
    else:
        mem_spec = pl.BlockSpec((None, nb) + mk.shape[2:], lambda bi, i: (layer, bi, 0, 0, 0))
    return pl.pallas_call(
        functools.partial(_mix_xattn_kernel, alpha=alpha, hd=hd, head_axis=mk.ndim != 3),
        grid=(bsz // nb, l // tm),
        in_specs=[
            row(d), row(dg), row(dg), row(dg), row(dg), _layer_spec(w_out, layer),
            mem_spec, mem_spec,
            _layer_spec(wq, layer), _layer_spec(wo, layer), _layer_spec(g, layer), _layer_spec(b, layer),
        ],
        out_specs=row(d),
        out_shape=jax.ShapeDtypeStruct((bsz, l, d), F32),
        compiler_params=_params("parallel", "parallel"),
        name="mix_xattn_ln",
    )(x, *mixed, w_out, mk, mv, wq, wo, g, b)


def _mlp_kernel(x_ref, w1_ref, w2_ref, g_ref, b_ref, o_ref, xb_scr, *, alpha, parts):
    f = pl.program_id(1)
    sl = _row_parts(x_ref.shape[0], parts)
    last = pl.num_programs(1) - 1

    def step(first, finish):
        if first:
            for s in sl:
                xb_scr[s, :] = x_ref[s, :].astype(BF16)
        hid = [jnp.maximum(jnp.dot(xb_scr[s, :], w1_ref[...], preferred_element_type=F32), 0.0) for s in sl]
        act = [(z * z).astype(BF16) for z in hid]
        for s, z in zip(sl, act):
            acc = jnp.dot(z, w2_ref[...], preferred_element_type=F32)
            if not first:
                acc = o_ref[s, :] + acc
            o_ref[s, :] = _layer_norm(alpha * x_ref[s, :] + acc, g_ref[2:3, :], b_ref[2:3, :]) if finish else acc

    pl.when(f == 0)(lambda: step(True, False))
    pl.when(jnp.logical_and(f > 0, f < last))(lambda: step(False, False))
    pl.when(f == last)(lambda: step(False, True))


def _mlp(x, w1, w2, layer, g, b, *, alpha, tm=1024, tf=1024):
    m, d = x.shape
    dff = w1.shape[2]
    tm = _tile(m, tm)
    tf = _tile(dff, tf)
    assert dff // tf >= 2, "the kernel distinguishes the first and the last hidden tile"
    return pl.pallas_call(
        functools.partial(_mlp_kernel, alpha=alpha, parts=4),
        grid=(m // tm, dff // tf),
        in_specs=[
            pl.BlockSpec((tm, d), lambda i, f: (i, 0)),
            _layer_spec(w1, layer, (d, tf), lambda i, f: (0, f)),
            _layer_spec(w2, layer, (tf, d), lambda i, f: (f, 0)),
            _layer_spec(g, layer), _layer_spec(b, layer),
        ],
        out_specs=pl.BlockSpec((tm, d), lambda i, f: (i, 0)),
        out_shape=jax.ShapeDtypeStruct((m, d), F32),
        scratch_shapes=[pltpu.VMEM((tm, d), BF16)],
        compiler_params=_params("parallel", "arbitrary"),
        name="mlp_ln",
    )(x, w1, w2, g, b)


def _block_diag(blocks):
    g, r, c = blocks.shape
    eye = jnp.eye(g, dtype=blocks.dtype)
    return (eye[:, None, :, None] * blocks[:, :, None, :]).reshape(g * r, g * c)


def _pad_lanes(v):
    return jnp.pad(v, (0, LANES - v.shape[0]))[None, :]


def _s5_params(lam_re, lam_im, log_dt, b_re, b_im, c_re, c_im):
    dt = jnp.exp(log_dt)[:, None]
    mag = jnp.exp(lam_re * dt)
    ar, ai = mag * jnp.cos(lam_im * dt), mag * jnp.sin(lam_im * dt)
    den = lam_re * lam_re + lam_im * lam_im
    fr = ((ar - 1.0) * lam_re + ai * lam_im) / den
    fi = (ai * lam_re - (ar - 1.0) * lam_im) / den
    bbr = fr[..., None] * b_re - fi[..., None] * b_im
    bbi = fr[..., None] * b_im + fi[..., None] * b_re
    wb = jnp.concatenate([_block_diag(jnp.swapaxes(bbr, 1, 2)), _block_diag(jnp.swapaxes(bbi, 1, 2))], axis=1)
    wc = jnp.concatenate([_block_diag(jnp.swapaxes(c_re, 1, 2)), -_block_diag(jnp.swapaxes(c_im, 1, 2))], axis=0)
    ar, ai = ar.reshape(-1), ai.reshape(-1)
    pows = [(ar, ai)]
    for _ in range(SUBLANES - 1):
        pr, pi = pows[-1]
        pows.append((pr * ar - pi * ai, pr * ai + pi * ar))
    cat = lambda idx: jnp.stack([jnp.concatenate(pows[n]) for n in idx], axis=0)
    row = jnp.arange(SUBLANES)[:, None]
    shift_pows = jnp.stack([jnp.where(row >= s, cat((s - 1,)), 0.0) for s in (2, 4)])
    wbr, wbi = wb[:, :ar.shape[0]], wb[:, ar.shape[0]:]
    wb_lag = jnp.concatenate([ar * wbr - ai * wbi, ar * wbi + ai * wbr], axis=1)
    wb2 = jnp.concatenate([wb, wb_lag], axis=0)
    return wb2.astype(BF16), shift_pows, cat(range(SUBLANES)), wc.astype(BF16)


def _rel_bias_table(table, n_rows, n_cols, offset):
    tab = table.astype(F32).T
    rel_min, rel_max = offset - (n_cols - 1), offset + n_rows - 1
    lo, hi = max(rel_min, -REL_CLIP), min(rel_max, REL_CLIP)
    parts = [jnp.repeat(tab[:, :1], lo - rel_min, axis=1), tab[:, lo + REL_CLIP:hi + REL_CLIP + 1],
             jnp.repeat(tab[:, -1:], rel_max - hi, axis=1)]
    ext = jnp.concatenate(parts, axis=1)
    length = n_rows + n_cols - 1
    flipped = jnp.pad(ext[:, ::-1], ((0, 0), (0, 1)))
    shifted = jnp.tile(flipped, (1, n_rows))[:, :n_rows * length].reshape(-1, n_rows, length)
    return shifted[:, :, n_rows - 1:n_rows - 1 + n_cols]


def _clipped_bias_run(tab, rel_first, count):
    n_hi = min(max(rel_first - REL_CLIP, 0), count)
    n_lo = min(max(-REL_CLIP - (rel_first - count + 1), 0), count)
    mid = count - n_hi - n_lo
    top = min(rel_first, REL_CLIP) + REL_CLIP
    return jnp.concatenate([jnp.repeat(tab[:, -1:], n_hi, axis=1), tab[:, top - mid + 1:top + 1][:, ::-1],
                            jnp.repeat(tab[:, :1], n_lo, axis=1)], axis=1)


def _band_bias_kernel(f_ref, o_ref, *, qp, span):
    pieces, width = o_ref.shape[1], o_ref.shape[3]
    bias = pltpu.roll(jnp.broadcast_to(f_ref[...], (qp, f_ref.shape[1])), 0, 1, stride=1, stride_axis=0)[:, :width]
    shift = CHUNK.bit_length() - 1
    r_chunk = lax.broadcasted_iota(jnp.int32, (qp, width), 0) >> shift
    col = lax.broadcasted_iota(jnp.int32, (qp, width), 1)
    in_band = ((col >> shift) >= r_chunk) & ((col >> shift) <= r_chunk + N_PREV)
    for p in range(pieces):
        o_ref[0, p] = jnp.where(in_band & (col >= span - p * qp), bias, -jnp.inf)
        o_ref[1, p] = jnp.where(in_band, bias, -jnp.inf)


def _band_prompt_bias(tables, qb, qp):
    depth, _, nh = tables.shape
    span = N_PREV * CHUNK
    width = span + qp
    period = -(-(width + qp - 1) // LANES) * LANES
    tab = jnp.swapaxes(tables.astype(F32), 1, 2).reshape(depth * nh, -1)
    f = jnp.concatenate([_clipped_bias_run(tab, span, width), jnp.zeros((depth * nh, period - width - (qp - 1)), F32),
                         _clipped_bias_run(tab, span + qp - 1, qp - 1)], axis=1).reshape(depth, nh, 1, period)
    pieces = qb // qp
    return pl.pallas_call(
        functools.partial(_band_bias_kernel, qp=qp, span=span),
        grid=(depth, nh),
        in_specs=[pl.BlockSpec((None, None, 1, period), lambda l, h: (l, h, 0, 0))],
        out_specs=pl.BlockSpec((None, 2, pieces, None, qp, width), lambda l, h: (l, 0, 0, h, 0, 0)),
        out_shape=jax.ShapeDtypeStruct((depth, 2, pieces, nh, qp, width), F32),
        compiler_params=_params("parallel", "parallel"),
        name="band_bias",
    )(f)


def _trunk_layer(x, mem_k, mem_v, gdn_conv, gdn_s, s5_h, rg_conv, rg_h, band_k, band_v, p, *, alpha):
    b, l, d = x.shape
    dg = d // N_MIX
    m = b * l
    layer = p["layer"]
    pa, pb, pc, pd, pdb = _inproj(x.reshape(m, d), p["w_in"], layer, (4 * dg, dg, 2 * dg, 3 * dg, LANES))
    pa, pb, pc, pd, pdb = [t.reshape(b, l, -1) for t in (pa, pb, pc, pd, pdb)]

    pad8 = lambda buf: jnp.pad(buf, ((0, 0), (SUBLANES - (CONV_W - 1), 0), (0, 0)))
    chunk = CHUNK if l % CHUNK == 0 else l
    cps = max(1, min(GDN_CHUNKS_PER_STEP, l // chunk))
    o_a, gdn_s_new = _gdn(pa, pdb, pad8(gdn_conv), gdn_s, p["gdn"], layer, t=chunk, cps=cps)
    gdn_conv_new = pa[:, l - (CONV_W - 1):, :3 * dg]

    ns = p["s5"][0].shape[2] // 2
    h0 = jnp.concatenate([s5_h[..., 0].reshape(b, 1, ns), s5_h[..., 1].reshape(b, 1, ns)], axis=-1)
    o_b, h_last = _s5(pb, h0, p["s5"], layer, t=_tile(l, SCAN_ROWS_PER_STEP))
    s5_h_new = jnp.stack([h_last[:, 0, :ns].reshape(s5_h.shape[:-1]), h_last[:, 0, ns:].reshape(s5_h.shape[:-1])],
                         axis=-1)

    o_c, rg_last = _rglru(pc, pad8(rg_conv), rg_h[:, None, :], p["rglru"], layer, t=_tile(l, SCAN_ROWS_PER_STEP))
    rg_conv_new = pc[:, l - (CONV_W - 1):, :dg]
    rg_h_new = rg_last[:, 0, :]

    hd = dg // H_D
    if band_k is None:
        qb = N_PREV * CHUNK
        o_d = _band_prompt(pd, p["band_bias_prompt"], layer, qb=qb, qp=BAND_PIECE)
        keep = min(N_PREV * CHUNK, l)
    else:
        o_d = _band_sample(pd, band_k, band_v, layer, p["band_bias_cache"], p["band_bias_new"])
        keep = l
    band_k_new = pd[:, l - keep:, dg:2 * dg].reshape(b, keep, H_D, hd)
    band_v_new = pd[:, l - keep:, 2 * dg:].reshape(b, keep, H_D, hd)

    x3 = _mix_xattn(x, (o_a, o_b, o_c, o_d), p["w_out"], mem_k, mem_v, p["xa_w_q"], p["xa_w_o"], layer,
                    p["ln_g"], p["ln_b"], alpha=alpha)
    x4 = _mlp(x3.reshape(m, d), p["mlp_w1"], p["mlp_w2"], layer, p["ln_g"], p["ln_b"], alpha=alpha)
    return x4.reshape(b, l, d), (gdn_conv_new, gdn_s_new, s5_h_new, rg_conv_new, rg_h_new, band_k_new, band_v_new)


def kernel(x_prompt, x_sample, state_gdn_conv, state_gdn, state_s5, state_rglru_conv, state_rglru, cache_band_k, cache_band_v, cache_mem_k, cache_mem_v, mem_prompt, w_in, w_out, ln_g, ln_b, gdn_conv_w, gdn_conv_b, gdn_a_log, gdn_dt_bias, gdn_norm_g, s5_lam_re, s5_lam_im, s5_log_dt, s5_b_re, s5_b_im, s5_c_re, s5_c_im, s5_d, s5_w_glu, s5_b_glu, rg_conv_w, rg_conv_b, rg_w_r, rg_b_r, rg_w_i, rg_b_i, rg_lam, band_rel_bias, xa_w_q, xa_w_k, xa_w_v, xa_w_o, mlp_w1, mlp_w2):
    depth = w_in.shape[0]
    bp, lp, d = x_prompt.shape
    bs, ls, _ = x_sample.shape
    n_mem = mem_prompt.shape[1]
    dg = d // N_MIX
    hd_x = d // H_X
    alpha = (2.0 * depth) ** 0.25
    band_rows = cache_band_k.shape[2]

    sizes = (3 * dg, dg, H_A, H_A, dg, dg, dg, 3 * dg)
    offs = [0]
    for s in sizes:
        offs.append(offs[-1] + s)

    w_db = jnp.pad(w_in[:, :, offs[2]:offs[4]], ((0, 0), (0, 0), (0, LANES - 2 * H_A)))
    w_in_bf = jnp.concatenate([w_in[:, :, offs[0]:offs[2]], w_in[:, :, offs[4]:offs[5]], w_in[:, :, offs[5]:offs[7]],
                               w_in[:, :, offs[7]:offs[8]], w_db], axis=2).astype(BF16)
    w_out_bf, xa_w_q_bf, xa_w_k_bf, xa_w_v_bf, xa_w_o_bf, mlp_w1_bf, mlp_w2_bf = [
        w.astype(BF16) for w in (w_out, xa_w_q, xa_w_k, xa_w_v, xa_w_o, mlp_w1, mlp_w2)]

    band_kt = jnp.transpose(cache_band_k, (0, 1, 3, 4, 2))
    band_vt = jnp.transpose(cache_band_v, (0, 1, 3, 4, 2))

    over_layers = jax.vmap
    s5_wb, s5_ap, s5_p8, s5_wc = over_layers(_s5_params)(s5_lam_re, s5_lam_im, s5_log_dt, s5_b_re, s5_b_im,
                                                         s5_c_re, s5_c_im)
    rg_wri = jnp.concatenate([over_layers(_block_diag)(rg_w_r), over_layers(_block_diag)(rg_w_i)], axis=2)
    shared = {
        "w_in": w_in_bf, "w_out": w_out_bf, "ln_g": ln_g, "ln_b": ln_b,
        "gdn": (gdn_conv_w, gdn_conv_b[:, None, :], over_layers(_pad_lanes)(gdn_a_log),
                over_layers(_pad_lanes)(gdn_dt_bias), gdn_norm_g[:, None, :]),
        "s5": (s5_wb, s5_ap, s5_p8, s5_wc, s5_d.reshape(depth, 1, dg), s5_w_glu.astype(BF16), s5_b_glu[:, None, :]),
        "rglru": (rg_conv_w, rg_conv_b[:, None, :], rg_wri.astype(BF16),
                  jnp.concatenate([rg_b_r, rg_b_i], axis=1)[:, None, :], rg_lam[:, None, :]),
        "band_bias_prompt": _band_prompt_bias(band_rel_bias, N_PREV * CHUNK, BAND_PIECE),
        "band_bias_cache": over_layers(lambda t: _rel_bias_table(t, ls, band_rows, band_rows))(band_rel_bias),
        "band_bias_new": over_layers(lambda t: _rel_bias_table(t, ls, ls, 0))(band_rel_bias),
        "xa_w_q": xa_w_q_bf, "xa_w_o": xa_w_o_bf, "mlp_w1": mlp_w1_bf, "mlp_w2": mlp_w2_bf,
    }

    xp, xs = x_prompt, x_sample
    p_states, s_states = [], []
    for l in range(depth):
        p = dict(shared, layer=l)
        mem2 = mem_prompt.reshape(bp * n_mem, d)
        mk = _matmul(mem2, xa_w_k_bf, l).reshape(bp, n_mem, d)
        mv = _matmul(mem2, xa_w_v_bf, l).reshape(bp, n_mem, d)
        xp, st_p = _trunk_layer(
            xp, mk, mv,
            jnp.zeros((bp, CONV_W - 1, 3 * dg), F32), jnp.zeros((bp, H_A, dg // H_A, dg // H_A), F32),
            jnp.zeros((bp, dg // S5_CH, P_B, 2), F32), jnp.zeros((bp, CONV_W - 1, dg), F32),
            jnp.zeros((bp, dg), F32), None, None, p, alpha=alpha)
        p_states.append(st_p + (mk.reshape(bp, n_mem, H_X, hd_x), mv.reshape(bp, n_mem, H_X, hd_x)))
        xs, st_s = _trunk_layer(
            xs, cache_mem_k, cache_mem_v,
            state_gdn_conv[l], state_gdn[l], state_s5[l], state_rglru_conv[l], state_rglru[l],
            band_kt, band_vt, p, alpha=alpha)
        s_states.append(st_s)

    def stk(states, i):
        return jnp.stack([st[i] for st in states], axis=0)

    return (xp, xs,
            stk(p_states, 0), stk(p_states, 1), stk(p_states, 2), stk(p_states, 3), stk(p_states, 4),
            stk(p_states, 5), stk(p_states, 6), stk(p_states, 7), stk(p_states, 8),
            stk(s_states, 0), stk(s_states, 1), stk(s_states, 2), stk(s_states, 3), stk(s_states, 4),
            stk(s_states, 5), stk(s_states, 6))
```

```python
import functools
import math

import jax
import jax.numpy as jnp
from jax import lax
from jax.experimental import pallas as pl
from jax.experimental.pallas import tpu as pltpu

F32 = jnp.float32
BF16 = jnp.bfloat16
HIGHEST = lax.Precision.HIGHEST

N_MIX = 4
CONV_W = 4
CHUNK = 64
H_A = 4
S5_CH = 16
P_B = 64
H_C = 4
RG_C = 8.0
H_D = 4
N_PREV = 8
REL_CLIP = 128
H_X = 4
LN_EPS = 1e-5
NORM_EPS = 1e-6

LANES = 128
SUBLANES = 8
VMEM_LIMIT_BYTES = 56 * 1024 * 1024
GDN_BLOCK = 128
GDN_CHUNKS_PER_STEP = 8
BAND_BLOCK = 1024
BAND_PIECE = 128
SCAN_ROWS_PER_STEP = 1024
XATTN_SEQS_PER_STEP = 4


def _params(*semantics):
    return pltpu.CompilerParams(dimension_semantics=semantics, vmem_limit_bytes=VMEM_LIMIT_BYTES)


def _tile(n, pref):
    t = min(n, pref)
    while n % t:
        t -= SUBLANES
    return t


def _seqs_per_step(b, l, t):
    nb = max(1, min(b, SCAN_ROWS_PER_STEP // t)) if l == t else 1
    while b % nb:
        nb -= 1
    return nb


def _mm(a, b):
    return jnp.dot(a.astype(BF16), b.astype(BF16), preferred_element_type=F32)


def _mm_nt(a, b):
    return lax.dot_general(a.astype(BF16), b.astype(BF16), (((1,), (1,)), ((), ())),
                           preferred_element_type=F32)


def _mm_tn(a, b):
    return lax.dot_general(a.astype(BF16), b.astype(BF16), (((0,), (0,)), ((), ())),
                           preferred_element_type=F32)


def _mm_f32(a, b):
    return jnp.dot(a, b, precision=HIGHEST, preferred_element_type=F32)


def _sigmoid(x):
    return 1.0 / (1.0 + jnp.exp(-x))


def _softplus(x):
    return jnp.maximum(x, 0.0) + jnp.log1p(jnp.exp(-jnp.abs(x)))


def _gelu_tanh(x):
    c = math.sqrt(2.0 / math.pi)
    return 0.5 * x * (1.0 + jnp.tanh(c * (x + 0.044715 * (x * x * x))))


def _layer_norm(z, g, b):
    mu = jnp.mean(z, axis=-1, keepdims=True)
    zc = z - mu
    var = jnp.mean(zc * zc, axis=-1, keepdims=True)
    return zc * lax.rsqrt(var + LN_EPS) * g + b


def _matmul_kernel(x_ref, w_ref, o_ref):
    o_ref[...] = _mm(x_ref[...], w_ref[...])


def _layer_spec(w, layer, block=None, index=None):
    block = tuple(w.shape[1:]) if block is None else block
    index = (lambda *_: (0,) * len(block)) if index is None else index
    return pl.BlockSpec((None,) + block, lambda *g: (layer,) + tuple(index(*g)))


def _matmul(x, w_bf16, layer, tm=512):
    m, k = x.shape
    n = w_bf16.shape[2]
    tm = _tile(m, tm)
    return pl.pallas_call(
        _matmul_kernel,
        grid=(m // tm,),
        in_specs=[pl.BlockSpec((tm, k), lambda i: (i, 0)), _layer_spec(w_bf16, layer)],
        out_specs=pl.BlockSpec((tm, n), lambda i: (i, 0)),
        out_shape=jax.ShapeDtypeStruct((m, n), F32),
        compiler_params=_params("parallel"),
        name="matmul",
    )(x, w_bf16)


def _inproj_kernel(x_ref, w_ref, *o_refs, bounds):
    xb = x_ref[...].astype(BF16)
    for o_ref, (s, e) in zip(o_refs, bounds):
        o_ref[...] = jnp.dot(xb, w_ref[:, s:e], preferred_element_type=F32)


def _inproj(x, w_bf16, layer, widths, tm=1024):
    m, k = x.shape
    tm = _tile(m, tm)
    bounds, s = [], 0
    for w in widths:
        bounds.append((s, s + w))
        s += w
    return pl.pallas_call(
        functools.partial(_inproj_kernel, bounds=tuple(bounds)),
        grid=(m // tm,),
        in_specs=[pl.BlockSpec((tm, k), lambda i: (i, 0)), _layer_spec(w_bf16, layer)],
        out_specs=[pl.BlockSpec((tm, w), lambda i: (i, 0)) for w in widths],
        out_shape=[jax.ShapeDtypeStruct((m, w), F32) for w in widths],
        compiler_params=_params("parallel"),
        name="inproj",
    )(x, w_bf16)


def _inverse_masks(r, c, t):
    neg_diag8 = jnp.where((r >> 3) == (c >> 3), -1.0, 0.0)
    offs, lb = [], 3
    while (1 << lb) < t:
        off = ((r >> (lb + 1)) == (c >> (lb + 1))) & (((r >> lb) & 1) == 1) & (((c >> lb) & 1) == 0)
        offs.append(jnp.where(off, 1.0, 0.0))
        lb += 1
    return neg_diag8, offs


def _unit_lower_inverse_offdiag(a_list, masks):
    neg_diag8, offs = masks
    n1 = [a * neg_diag8 for a in a_list]
    n2 = [_mm(x, x) for x in n1]
    n3 = [_mm(x, x2) for x, x2 in zip(n1, n2)]
    n4 = [_mm(x2, x2) for x2 in n2]
    p = [x + x2 + x3 for x, x2, x3 in zip(n1, n2, n3)]
    pn4 = [_mm(pp, x4) for pp, x4 in zip(p, n4)]
    y = [pp + x4 + px for pp, x4, px in zip(p, n4, pn4)]
    for off in offs:
        m = [a * off for a in a_list]
        z = [mm + _mm(yy, mm) for yy, mm in zip(y, m)]
        zy = [_mm(zz, yy) for zz, yy in zip(z, y)]
        y = [yy - (zz + zzy) for yy, zz, zzy in zip(y, z, zy)]
    return y


def _split3(x):
    h1 = x.astype(BF16)
    r1 = x - h1.astype(F32)
    h2 = r1.astype(BF16)
    h3 = (r1 - h2.astype(F32)).astype(BF16)
    return h1, h2, h3


def _gdn_kernel(qkv_ref, gate_ref, db_ref, cbuf_ref, s0_ref, cw_ref, cb_ref, alog_ref, dtb_ref, ng_ref,
                o_ref, sfin_ref, xp_scr, s_scr, o_scr, *, t, dk):
    i = pl.program_id(1)
    nb, rows, _ = qkv_ref.shape
    tb = nb * rows
    nh = H_A
    dq = nh * dk

    @pl.when(i == 0)
    def _():
        xp_scr[:, 0:SUBLANES, :] = cbuf_ref[...]
        s_scr[...] = s0_ref[...]

    base = SUBLANES - (CONV_W - 1)
    ys = []
    for n in range(nb):
        x = qkv_ref[n]
        xp_scr[n, SUBLANES:SUBLANES + rows, :] = x
        y = xp_scr[n, base:base + rows, :] * cw_ref[0:1, :]
        for j in range(1, CONV_W):
            y = y + xp_scr[n, base + j:base + j + rows, :] * cw_ref[j:j + 1, :]
        xp_scr[n, 0:SUBLANES, :] = x[rows - SUBLANES:rows, :]
        ys.append(y + cb_ref[...])
    y = jnp.concatenate(ys, axis=0)
    y = y * _sigmoid(y)

    db = db_ref[...].reshape(tb, LANES)
    log_a = -jnp.exp(alog_ref[...]) * _softplus(db + dtb_ref[...])
    beta_all = _sigmoid(db)

    bs = min(tb, GDN_BLOCK)
    lt = t.bit_length() - 1
    r = lax.broadcasted_iota(jnp.int32, (bs, bs), 0)
    c = lax.broadcasted_iota(jnp.int32, (bs, bs), 1)
    same = (r >> lt) == (c >> lt)
    causal_neg = jnp.where(same & (r >= c), 0.0, -jnp.inf)
    strict_f = jnp.where(same & (r > c), 1.0, 0.0)
    tril = jnp.where(same & (r >= c), 1.0, 0.0).astype(BF16)
    striu = jnp.where(same & (r < c), 1.0, 0.0).astype(BF16)
    inv_masks = _inverse_masks(r, c, t)

    blocks = list(range(0, tb, bs))
    pairs = [(bi, h) for bi in range(len(blocks)) for h in range(nh)]
    la3 = [_split3(log_a[b0:b0 + bs, :]) for b0 in blocks]
    gc = [sum(jnp.dot(tril, part, preferred_element_type=F32) for part in parts) for parts in la3]
    rv = [sum(jnp.dot(striu, part, preferred_element_type=F32) for part in parts) for parts in la3]
    eg = [jnp.exp(g) for g in gc]
    erv = [jnp.exp(g) for g in rv]
    e_tot = [jnp.exp(g + g2) for g, g2 in zip(gc, rv)]
    gc_rows = [g.T for g in gc]

    def head_cols(z, bi, h, off):
        b0 = blocks[bi]
        return z[b0:b0 + bs, off + h * dk:off + (h + 1) * dk]

    q = [head_cols(y, bi, h, 0) for bi, h in pairs]
    k = [head_cols(y, bi, h, dq) for bi, h in pairs]
    v = [head_cols(y, bi, h, 2 * dq) for bi, h in pairs]
    q = [z * lax.rsqrt(jnp.sum(z * z, axis=-1, keepdims=True) + NORM_EPS) * (dk ** -0.5) for z in q]
    k = [z * lax.rsqrt(jnp.sum(z * z, axis=-1, keepdims=True) + NORM_EPS) for z in k]
    decay = [jnp.exp(gc[bi][:, h:h + 1] - gc_rows[bi][h:h + 1, :] + causal_neg) for bi, h in pairs]
    beta = [beta_all[blocks[bi]:blocks[bi] + bs, nh + h:nh + h + 1] for bi, h in pairs]
    eg_col = [eg[bi][:, h:h + 1] for bi, h in pairs]
    kk = [_mm_nt(z, z) for z in k]
    qk = [_mm_nt(zq, zk) for zq, zk in zip(q, k)]
    a = [(b * z * d) * strict_f for b, z, d in zip(beta, kk, decay)]
    qk = [z * d for z, d in zip(qk, decay)]
    y_inv = _unit_lower_inverse_offdiag(a, inv_masks)
    rhs = [jnp.concatenate([zv * b, zk * (b * e)], axis=-1) for zv, zk, b, e in zip(v, k, beta, eg_col)]
    sol = [z + _mm(yi, z) for yi, z in zip(y_inv, rhs)]
    q_dec = [z * e for z, e in zip(q, eg_col)]
    k_dec = [z * erv[bi][:, h:h + 1] for z, (bi, h) in zip(k, pairs)]

    chunks = [(bi, r0) for bi in range(len(blocks)) for r0 in range(0, bs, t)]
    n_kw = [[_mm_tn(k_dec[bi * nh + h][r0:r0 + t], sol[bi * nh + h][r0:r0 + t]) for h in range(nh)]
            for bi, r0 in chunks]
    states = [s_scr[0, h] for h in range(nh)]
    starts = []
    for ci, ((bi, r0), nk) in enumerate(zip(chunks, n_kw)):
        if nb > 1:
            states = [s_scr[ci, h] for h in range(nh)]
        starts.append(states)
        drop = [_mm(z[:, dk:], s) for z, s in zip(nk, states)]
        states = [s * e_tot[bi][r0:r0 + 1, h:h + 1] + (z[:, :dk] - d)
                  for h, (s, z, d) in enumerate(zip(states, nk, drop))]
        if nb > 1:
            for h in range(nh):
                s_scr[ci, h] = states[h]
    ws = [[_mm(jnp.concatenate([sol[bi * nh + h][r0:r0 + t, dk:], q_dec[bi * nh + h][r0:r0 + t]], axis=0), st[h])
           for h in range(nh)] for (bi, r0), st in zip(chunks, starts)]
    v_news = [[] for _ in pairs]
    o_inter = [[] for _ in pairs]
    for (bi, r0), wc in zip(chunks, ws):
        for h, z in enumerate(wc):
            v_news[bi * nh + h].append(sol[bi * nh + h][r0:r0 + t, :dk] - z[:t])
            o_inter[bi * nh + h].append(z[t:])
    o = [jnp.concatenate(oi, axis=0) + _mm(z, jnp.concatenate(vn, axis=0))
         for oi, z, vn in zip(o_inter, qk, v_news)]
    o = [z * lax.rsqrt(jnp.mean(z * z, axis=-1, keepdims=True) + NORM_EPS) * ng_ref[...] for z in o]
    for (bi, h), z in zip(pairs, o):
        o_scr[blocks[bi]:blocks[bi] + bs, h * dk:(h + 1) * dk] = z
    if nb == 1:
        for h in range(nh):
            s_scr[0, h] = states[h]

    g = gate_ref[...].reshape(tb, dq)
    o_ref[...] = (o_scr[...] * (g * _sigmoid(g))).astype(o_ref.dtype).reshape(nb, rows, dq)

    @pl.when(i == pl.num_programs(1) - 1)
    def _():
        sfin_ref[...] = s_scr[...]


def _gdn(pa, pdb, cbuf8, s0, params, layer, *, t, cps):
    b, l, _ = pa.shape
    nh, dk = s0.shape[1], s0.shape[2]
    dq = nh * dk
    nb = _seqs_per_step(b, l, t)
    rows = t * cps
    return pl.pallas_call(
        functools.partial(_gdn_kernel, t=t, dk=dk),
        grid=(b // nb, l // rows),
        in_specs=[
            pl.BlockSpec((nb, rows, 3 * dq), lambda bi, i: (bi, i, 0)),
            pl.BlockSpec((nb, rows, dq), lambda bi, i: (bi, i, 3)),
            pl.BlockSpec((nb, rows, LANES), lambda bi, i: (bi, i, 0)),
            pl.BlockSpec((nb, SUBLANES, 3 * dq), lambda bi, i: (bi, 0, 0)),
            pl.BlockSpec((nb, nh, dk, dk), lambda bi, i: (bi, 0, 0, 0)),
        ] + [_layer_spec(w, layer) for w in params],
        out_specs=[
            pl.BlockSpec((nb, rows, dq), lambda bi, i: (bi, i, 0)),
            pl.BlockSpec((nb, nh, dk, dk), lambda bi, i: (bi, 0, 0, 0)),
        ],
        out_shape=[jax.ShapeDtypeStruct((b, l, dq), BF16), jax.ShapeDtypeStruct((b, nh, dk, dk), F32)],
        scratch_shapes=[pltpu.VMEM((nb, SUBLANES + rows, 3 * dq), F32), pltpu.VMEM((nb, nh, dk, dk), F32),
                        pltpu.VMEM((nb * rows, dq), F32)],
        compiler_params=_params("parallel", "arbitrary"),
        name="gdn",
    )(pa, pa, pdb, cbuf8, s0, *params)


def _s5_kernel(u_ref, h0_ref, wb_ref, ap_ref, p8_ref, wc_ref, d_ref, wg_ref, bg_ref,
               o_ref, hl_ref, carry_scr, h_scr, *, t, ns):
    i = pl.program_id(1)
    nb, _, dg = u_ref.shape

    @pl.when(i == 0)
    def _():
        carry_scr[...] = h0_ref[...]

    u = u_ref[...].reshape(nb * t, dg)
    ng = nb * t // SUBLANES
    first = lax.broadcasted_iota(jnp.int32, (SUBLANES, dg), 0) == 0
    u_prev = jnp.where(first, 0.0, pltpu.roll(u.reshape(ng, SUBLANES, dg), 1, 1)).reshape(nb * t, dg)
    x = _mm(jnp.concatenate([u, u_prev], axis=-1), wb_ref[...])
    xr = x[:, :ns].reshape(ng, SUBLANES, ns)
    xi = x[:, ns:].reshape(ng, SUBLANES, ns)
    for lvl in range(2):
        s = 2 << lvl
        pr, pi = ap_ref[lvl, :, :ns], ap_ref[lvl, :, ns:]
        sr = pltpu.roll(xr, s, 1)
        si = pltpu.roll(xi, s, 1)
        xr, xi = xr + (pr * sr - pi * si), xi + (pr * si + pi * sr)
    p8r, p8i = p8_ref[:, :ns], p8_ref[:, ns:]
    for n in range(nb):
        cr, ci = carry_scr[n, :, :ns], carry_scr[n, :, ns:]
        for j in range(n * t // SUBLANES, (n + 1) * t // SUBLANES):
            sl = slice(j * SUBLANES, (j + 1) * SUBLANES)
            br = xr[j] + (p8r * cr - p8i * ci)
            bi = xi[j] + (p8r * ci + p8i * cr)
            h_scr[sl, :ns] = br
            h_scr[sl, ns:] = bi
            cr, ci = br[SUBLANES - 1:SUBLANES], bi[SUBLANES - 1:SUBLANES]
        carry_scr[n, :, :ns] = cr
        carry_scr[n, :, ns:] = ci
    hl_ref[...] = carry_scr[...]

    y = _mm(h_scr[...], wc_ref[...]) + d_ref[...] * u
    y = _gelu_tanh(y)
    z = _mm(y, wg_ref[...]) + bg_ref[...]
    o_ref[...] = (y * _sigmoid(z)).astype(o_ref.dtype).reshape(nb, t, dg)


def _s5(pb, h0, params, layer, *, t):
    b, l, dg = pb.shape
    ns2 = params[0].shape[2]
    ns = ns2 // 2
    nb = _seqs_per_step(b, l, t)
    return pl.pallas_call(
        functools.partial(_s5_kernel, t=t, ns=ns),
        grid=(b // nb, l // t),
        in_specs=[
            pl.BlockSpec((nb, t, dg), lambda bi, i: (bi, i, 0)),
            pl.BlockSpec((nb, 1, ns2), lambda bi, i: (bi, 0, 0)),
        ] + [_layer_spec(w, layer) for w in params],
        out_specs=[
            pl.BlockSpec((nb, t, dg), lambda bi, i: (bi, i, 0)),
            pl.BlockSpec((nb, 1, ns2), lambda bi, i: (bi, 0, 0)),
        ],
        out_shape=[jax.ShapeDtypeStruct((b, l, dg), BF16), jax.ShapeDtypeStruct((b, 1, ns2), F32)],
        scratch_shapes=[pltpu.VMEM((nb, 1, ns2), F32), pltpu.VMEM((nb * t, ns2), F32)],
        compiler_params=_params("parallel", "arbitrary"),
        name="s5",
    )(pb, h0, *params)


def _rglru_kernel(xg_ref, cbuf_ref, h0_ref, cw_ref, cb_ref, wri_ref, bri_ref, lam_ref,
                  o_ref, hl_ref, xp_scr, carry_scr, h_scr, *, t, dg):
    i = pl.program_id(1)
    nb = xg_ref.shape[0]

    @pl.when(i == 0)
    def _():
        xp_scr[:, 0:SUBLANES, :] = cbuf_ref[...]
        carry_scr[...] = h0_ref[...]

    base = SUBLANES - (CONV_W - 1)
    ys = []
    for n in range(nb):
        x = xg_ref[n, :, :dg]
        xp_scr[n, SUBLANES:SUBLANES + t, :] = x
        y = xp_scr[n, base:base + t, :] * cw_ref[0:1, :]
        for j in range(1, CONV_W):
            y = y + xp_scr[n, base + j:base + j + t, :] * cw_ref[j:j + 1, :]
        xp_scr[n, 0:SUBLANES, :] = x[t - SUBLANES:t, :]
        ys.append(y + cb_ref[...])
    y = jnp.concatenate(ys, axis=0)
    gb = xg_ref[...][:, :, dg:].reshape(nb * t, dg)

    ri = _mm(y, wri_ref[...]) + bri_ref[...]
    rg = _sigmoid(ri[:, :dg])
    ig = _sigmoid(ri[:, dg:])
    log_a = (-RG_C * rg) * _softplus(-lam_ref[...])
    a = jnp.exp(log_a)
    th = jnp.tanh(log_a)
    xin = jnp.sqrt(-2.0 * th / (1.0 - th)) * (ig * y)

    ng = nb * t // SUBLANES
    a = a.reshape(ng, SUBLANES, dg)
    xin = xin.reshape(ng, SUBLANES, dg)
    row = lax.broadcasted_iota(jnp.int32, (SUBLANES, dg), 0)
    for lvl in range(3):
        s = 1 << lvl
        keep = row >= s
        a_s = jnp.where(keep, pltpu.roll(a, s, 1), 1.0)
        x_s = jnp.where(keep, pltpu.roll(xin, s, 1), 0.0)
        xin = a * x_s + xin
        a = a * a_s
    for n in range(nb):
        cr = carry_scr[n]
        for j in range(n * t // SUBLANES, (n + 1) * t // SUBLANES):
            hb = xin[j] + a[j] * cr
            h_scr[j * SUBLANES:(j + 1) * SUBLANES, :] = hb
            cr = hb[SUBLANES - 1:SUBLANES]
        carry_scr[n] = cr
    hl_ref[...] = carry_scr[...]
    o_ref[...] = (h_scr[...] * _gelu_tanh(gb)).astype(o_ref.dtype).reshape(nb, t, dg)


def _rglru(pc, cbuf8, h0, params, layer, *, t):
    b, l, dg2 = pc.shape
    dg = dg2 // 2
    nb = _seqs_per_step(b, l, t)
    return pl.pallas_call(
        functools.partial(_rglru_kernel, t=t, dg=dg),
        grid=(b // nb, l // t),
        in_specs=[
            pl.BlockSpec((nb, t, dg2), lambda bi, i: (bi, i, 0)),
            pl.BlockSpec((nb, SUBLANES, dg), lambda bi, i: (bi, 0, 0)),
            pl.BlockSpec((nb, 1, dg), lambda bi, i: (bi, 0, 0)),
        ] + [_layer_spec(w, layer) for w in params],
        out_specs=[
            pl.BlockSpec((nb, t, dg), lambda bi, i: (bi, i, 0)),
            pl.BlockSpec((nb, 1, dg), lambda bi, i: (bi, 0, 0)),
        ],
        out_shape=[jax.ShapeDtypeStruct((b, l, dg), BF16), jax.ShapeDtypeStruct((b, 1, dg), F32)],
        scratch_shapes=[pltpu.VMEM((nb, SUBLANES + t, dg), F32), pltpu.VMEM((nb, 1, dg), F32),
                        pltpu.VMEM((nb * t, dg), F32)],
        compiler_params=_params("parallel", "arbitrary"),
        name="rglru",
    )(pc, cbuf8, h0, *params)


def _band_prompt_kernel(q_ref, kc_ref, vc_ref, kp_ref, vp_ref, bias_ref, o_ref, *, qb, qp, hd):
    span = N_PREV * CHUNK
    q = q_ref[0] * (hd ** -0.5)
    k = jnp.concatenate([kp_ref[0], kc_ref[0]], axis=0)
    v = jnp.concatenate([vp_ref[0], vc_ref[0]], axis=0)
    units = [(h, p) for h in range(H_D) for p in range(qb // qp)]
    col = lambda h: slice(h * hd, (h + 1) * hd)
    win = lambda p: slice(p * qp, span + (p + 1) * qp)
    s, e, den, o = {}, {}, {}, []
    for n in range(len(units) + 2):
        if n < len(units):
            h, p = units[n]
            s[n] = _mm_nt(q[p * qp:(p + 1) * qp, col(h)], k[win(p), col(h)]) + bias_ref[0, p, h]
        if 0 <= n - 1 < len(units):
            z = s.pop(n - 1)
            e[n - 1] = jnp.exp(z - jnp.max(z, axis=-1, keepdims=True))
            den[n - 1] = jnp.sum(e[n - 1], axis=-1, keepdims=True)
        if 0 <= n - 2 < len(units):
            h, p = units[n - 2]
            o.append(_mm(e.pop(n - 2), v[win(p), col(h)]) / den.pop(n - 2))
    npc = qb // qp
    o_ref[0] = jnp.concatenate([jnp.concatenate(o[h * npc:(h + 1) * npc], axis=0) for h in range(H_D)],
                               axis=-1).astype(o_ref.dtype)


def _band_prompt(pd, bias, layer, *, qb, qp):
    b, l, w3 = pd.shape
    w = w3 // 3
    hd = w // H_D
    span = N_PREV * CHUNK
    prev = lambda bi, i: jnp.maximum(i * (qb // span) - 1, 0)
    return pl.pallas_call(
        functools.partial(_band_prompt_kernel, qb=qb, qp=qp, hd=hd),
        grid=(b, l // qb),
        in_specs=[
            pl.BlockSpec((1, qb, w), lambda bi, i: (bi, i, 0)),
            pl.BlockSpec((1, qb, w), lambda bi, i: (bi, i, 1)),
            pl.BlockSpec((1, qb, w), lambda bi, i: (bi, i, 2)),
            pl.BlockSpec((1, span, w), lambda bi, i: (bi, prev(bi, i), 1)),
            pl.BlockSpec((1, span, w), lambda bi, i: (bi, prev(bi, i), 2)),
            pl.BlockSpec((None, 1) + bias.shape[2:], lambda bi, i: (layer, jnp.minimum(i, 1), 0, 0, 0, 0)),
        ],
        out_specs=pl.BlockSpec((1, qb, w), lambda bi, i: (bi, i, 0)),
        out_shape=jax.ShapeDtypeStruct((b, l, w), BF16),
        compiler_params=_params("parallel", "arbitrary"),
        name="band_prompt",
    )(pd, pd, pd, pd, pd, bias)


def _band_sample_kernel(qkv_ref, kt_ref, vt_ref, bc_ref, bn_ref, o_ref, *, hd):
    nb = qkv_ref.shape[0]
    w = H_D * hd
    units = [(n, h, slice(h * hd, (h + 1) * hd)) for n in range(nb) for h in range(H_D)]
    q = [qkv_ref[n, :, :w] * (hd ** -0.5) for n in range(nb)]
    kn = [qkv_ref[n, :, w:2 * w] for n in range(nb)]
    vn = [qkv_ref[n, :, 2 * w:] for n in range(nb)]
    sc = [_mm(q[n][:, sl], kt_ref[n, h]) + bc_ref[h] for n, h, sl in units]
    sn = [_mm_nt(q[n][:, sl], kn[n][:, sl]) + bn_ref[h] for n, h, sl in units]
    m = [jnp.maximum(jnp.max(c, axis=-1, keepdims=True), jnp.max(z, axis=-1, keepdims=True)) for c, z in zip(sc, sn)]
    pc = [jnp.exp(c - z) for c, z in zip(sc, m)]
    pn = [jnp.exp(c - z) for c, z in zip(sn, m)]
    den = [jnp.sum(c, axis=-1, keepdims=True) + jnp.sum(z, axis=-1, keepdims=True) for c, z in zip(pc, pn)]
    outs = [(_mm_nt(c, vt_ref[n, h]) + _mm(z, vn[n][:, sl])) / d
            for (n, h, sl), c, z, d in zip(units, pc, pn, den)]
    for n in range(nb):
        o_ref[n] = jnp.concatenate(outs[n * H_D:(n + 1) * H_D], axis=-1).astype(o_ref.dtype)


def _band_sample(pd, kt_cache, vt_cache, layer, bias_c, bias_n):
    b, l, w3 = pd.shape
    w = w3 // 3
    hd = w // H_D
    rows = kt_cache.shape[-1]
    nb = XATTN_SEQS_PER_STEP
    while b % nb:
        nb -= 1
    cache_spec = pl.BlockSpec((None, nb, H_D, hd, rows), lambda bi: (layer, bi, 0, 0, 0))
    return pl.pallas_call(
        functools.partial(_band_sample_kernel, hd=hd),
        grid=(b // nb,),
        in_specs=[
            pl.BlockSpec((nb, l, w3), lambda bi: (bi, 0, 0)),
            cache_spec, cache_spec, _layer_spec(bias_c, layer), _layer_spec(bias_n, layer),
        ],
        out_specs=pl.BlockSpec((nb, l, w), lambda bi: (bi, 0, 0)),
        out_shape=jax.ShapeDtypeStruct((b, l, w), BF16),
        compiler_params=_params("parallel"),
        name="band_sample",
    )(pd, kt_cache, vt_cache, bias_c, bias_n)


def _row_parts(rows, parts):
    if rows % (parts * 2 * SUBLANES):
        parts = 1
    step = rows // parts
    return [slice(n * step, (n + 1) * step) for n in range(parts)]


def _mix_xattn_kernel(x_ref, oa_ref, ob_ref, oc_ref, od_ref, wm_ref, mk_ref, mv_ref, wq_ref, wo_ref, g_ref, b_ref,
                      o_ref, *, alpha, hd, head_axis):
    nb, tm, d = x_ref.shape
    dg = oa_ref.shape[2]
    heads = [slice(h * hd, (h + 1) * hd) for h in range(H_X)]
    if head_axis:
        n_mem = mk_ref.shape[1]
        mk = [mk_ref[n].reshape(n_mem * H_X, hd).astype(BF16) for n in range(nb)]
        mv = [mv_ref[n].reshape(n_mem * H_X, hd).astype(BF16) for n in range(nb)]
    else:
        mk = [[mk_ref[n, :, hs].astype(BF16) for hs in heads] for n in range(nb)]
        mv = [[mv_ref[n, :, hs].astype(BF16) for hs in heads] for n in range(nb)]
    if nb == 1:
        parts = _row_parts(tm, 2)
        rows_of = lambda ref, sl: ref[0, sl, :]
        units = [(n, slice(0, sl.stop - sl.start), 0) for n, sl in enumerate(parts)]
    else:
        parts = [slice(0, nb * tm)]
        rows_of = lambda ref, sl: ref[...].reshape(nb * tm, ref.shape[2])
        units = [(0, slice(n * tm, (n + 1) * tm), n) for n in range(nb)]
    mix = [sum(jnp.dot(rows_of(r, sl), wm_ref[n * dg:(n + 1) * dg, :], preferred_element_type=F32)
               for n, r in enumerate((oa_ref, ob_ref, oc_ref, od_ref))) for sl in parts]
    xs = [_layer_norm(alpha * rows_of(x_ref, sl) + z, g_ref[0:1, :], b_ref[0:1, :]) for sl, z in zip(parts, mix)]
    q = [(_mm(x, wq_ref[...]) * (hd ** -0.5)).astype(BF16) for x in xs]
    if head_axis:
        rows_u = units[0][1].stop - units[0][1].start
        row = lax.broadcasted_iota(jnp.int32, (H_X * rows_u, n_mem * H_X), 0)
        r_head = sum((row >= h * rows_u).astype(jnp.int32) for h in range(1, H_X))
        c_head = lax.broadcasted_iota(jnp.int32, (H_X * rows_u, n_mem * H_X), 1) & (H_X - 1)
        own_head = jnp.where(r_head == c_head, 0.0, -jnp.inf)
        qs = [jnp.concatenate([q[p][rows, hs] for hs in heads], axis=0) for p, rows, _ in units]
        s = [_mm_nt(z, mk[n]) + own_head for z, (_, _, n) in zip(qs, units)]
        e = [jnp.exp(z - jnp.max(z, axis=-1, keepdims=True)) for z in s]
        pr = [z * (1.0 / jnp.sum(z, axis=-1, keepdims=True)) for z in e]
        pv = [_mm(z, mv[n]) for z, (_, _, n) in zip(pr, units)]
        pv = [jnp.concatenate([z[h * rows_u:(h + 1) * rows_u] for h in range(H_X)], axis=-1) for z in pv]
    else:
        s = [[_mm_nt(q[p][rows, hs], kh) for hs, kh in zip(heads, mk[n])] for p, rows, n in units]
        e = [[jnp.exp(z - jnp.max(z, axis=-1, keepdims=True)) for z in su] for su in s]
        pr = [[z * (1.0 / jnp.sum(z, axis=-1, keepdims=True)) for z in eu] for eu in e]
        pv = [jnp.concatenate([_mm(z, vh) for z, vh in zip(pu, mv[n])], axis=-1) for pu, (_, _, n) in zip(pr, units)]
    pv = [jnp.concatenate([z for z, (p, _, _) in zip(pv, units) if p == n], axis=0) for n in range(len(parts))]
    att = [_mm(z, wo_ref[...]) for z in pv]
    out = [_layer_norm(alpha * x + z, g_ref[1:2, :], b_ref[1:2, :]) for x, z in zip(xs, att)]
    if nb == 1:
        for sl, z in zip(parts, out):
            o_ref[0, sl, :] = z
    else:
        o_ref[...] = out[0].reshape(nb, tm, d)


def _mix_xattn(x, mixed, w_out, mk, mv, wq, wo, layer, g, b, *, alpha, rows=1024):
    bsz, l, d = x.shape
    dg = mixed[0].shape[2]
    hd = d // H_X
    tm = _tile(l, rows)
    nb = max(1, min(bsz, XATTN_SEQS_PER_STEP, rows // l)) if tm == l else 1
    while bsz % nb:
        nb -= 1
    row = lambda width: pl.BlockSpec((nb, tm, width), lambda bi, i: (bi, i, 0))
    if mk.ndim == 3:
        mem_spec = pl.BlockSpec((nb,) + mk.shape[1:], lambda bi, i: (bi, 0, 0))
    else:
        mem_spec = pl.BlockSpec((None, nb) + mk.shape[2:], lambda bi, i: (layer, bi, 0, 0, 0))
    return pl.pallas_call(
        functools.partial(_mix_xattn_kernel, alpha=alpha, hd=hd, head_axis=mk.ndim != 3),
        grid=(bsz // nb, l // tm),
        in_specs=[
            row(d), row(dg), row(dg), row(dg), row(dg), _layer_spec(w_out, layer),
            mem_spec, mem_spec,
            _layer_spec(wq, layer), _layer_spec(wo, layer), _layer_spec(g, layer), _layer_spec(b, layer),
        ],
        out_specs=row(d),
        out_shape=jax.ShapeDtypeStruct((bsz, l, d), F32),
        compiler_params=_params("parallel", "parallel"),
        name="mix_xattn_ln",
    )(x, *mixed, w_out, mk, mv, wq, wo, g, b)


def _mlp_kernel(x_ref, w1_ref, w2_ref, g_ref, b_ref, o_ref, xb_scr, *, alpha, parts):
    f = pl.program_id(1)
    sl = _row_parts(x_ref.shape[0], parts)
    last = pl.num_programs(1) - 1

    def step(first, finish):
        if first:
            for s in sl:
                xb_scr[s, :] = x_ref[s, :].astype(BF16)
        hid = [jnp.maximum(jnp.dot(xb_scr[s, :], w1_ref[...], preferred_element_type=F32), 0.0) for s in sl]
        act = [(z * z).astype(BF16) for z in hid]
        for s, z in zip(sl, act):
            acc = jnp.dot(z, w2_ref[...], preferred_element_type=F32)
            if not first:
                acc = o_ref[s, :] + acc
            o_ref[s, :] = _layer_norm(alpha * x_ref[s, :] + acc, g_ref[2:3, :], b_ref[2:3, :]) if finish else acc

    pl.when(f == 0)(lambda: step(True, False))
    pl.when(jnp.logical_and(f > 0, f < last))(lambda: step(False, False))
    pl.when(f == last)(lambda: step(False, True))


def _mlp(x, w1, w2, layer, g, b, *, alpha, tm=1024, tf=2048):
    m, d = x.shape
    dff = w1.shape[2]
    tm = _tile(m, tm)
    tf = _tile(dff, tf)
    assert dff // tf >= 2, "the kernel distinguishes the first and the last hidden tile"
    return pl.pallas_call(
        functools.partial(_mlp_kernel, alpha=alpha, parts=4),
        grid=(m // tm, dff // tf),
        in_specs=[
            pl.BlockSpec((tm, d), lambda i, f: (i, 0)),
            _layer_spec(w1, layer, (d, tf), lambda i, f: (0, f)),
            _layer_spec(w2, layer, (tf, d), lambda i, f: (f, 0)),
            _layer_spec(g, layer), _layer_spec(b, layer),
        ],
        out_specs=pl.BlockSpec((tm, d), lambda i, f: (i, 0)),
        out_shape=jax.ShapeDtypeStruct((m, d), F32),
        scratch_shapes=[pltpu.VMEM((tm, d), BF16)],
        compiler_params=_params("parallel", "arbitrary"),
        name="mlp_ln",
    )(x, w1, w2, g, b)


def _block_diag(blocks):
    g, r, c = blocks.shape
    eye = jnp.eye(g, dtype=blocks.dtype)
    return (eye[:, None, :, None] * blocks[:, :, None, :]).reshape(g * r, g * c)


def _pad_lanes(v):
    return jnp.pad(v, (0, LANES - v.shape[0]))[None, :]


def _s5_params(lam_re, lam_im, log_dt, b_re, b_im, c_re, c_im):
    dt = jnp.exp(log_dt)[:, None]
    mag = jnp.exp(lam_re * dt)
    ar, ai = mag * jnp.cos(lam_im * dt), mag * jnp.sin(lam_im * dt)
    den = lam_re * lam_re + lam_im * lam_im
    fr = ((ar - 1.0) * lam_re + ai * lam_im) / den
    fi = (ai * lam_re - (ar - 1.0) * lam_im) / den
    bbr = fr[..., None] * b_re - fi[..., None] * b_im
    bbi = fr[..., None] * b_im + fi[..., None] * b_re
    wb = jnp.concatenate([_block_diag(jnp.swapaxes(bbr, 1, 2)), _block_diag(jnp.swapaxes(bbi, 1, 2))], axis=1)
    wc = jnp.concatenate([_block_diag(jnp.swapaxes(c_re, 1, 2)), -_block_diag(jnp.swapaxes(c_im, 1, 2))], axis=0)
    ar, ai = ar.reshape(-1), ai.reshape(-1)
    pows = [(ar, ai)]
    for _ in range(SUBLANES - 1):
        pr, pi = pows[-1]
        pows.append((pr * ar - pi * ai, pr * ai + pi * ar))
    cat = lambda idx: jnp.stack([jnp.concatenate(pows[n]) for n in idx], axis=0)
    row = jnp.arange(SUBLANES)[:, None]
    shift_pows = jnp.stack([jnp.where(row >= s, cat((s - 1,)), 0.0) for s in (2, 4)])
    wbr, wbi = wb[:, :ar.shape[0]], wb[:, ar.shape[0]:]
    wb_lag = jnp.concatenate([ar * wbr - ai * wbi, ar * wbi + ai * wbr], axis=1)
    wb2 = jnp.concatenate([wb, wb_lag], axis=0)
    return wb2.astype(BF16), shift_pows, cat(range(SUBLANES)), wc.astype(BF16)


def _rel_bias_table(table, n_rows, n_cols, offset):
    tab = table.astype(F32).T
    rel_min, rel_max = offset - (n_cols - 1), offset + n_rows - 1
    lo, hi = max(rel_min, -REL_CLIP), min(rel_max, REL_CLIP)
    parts = [jnp.repeat(tab[:, :1], lo - rel_min, axis=1), tab[:, lo + REL_CLIP:hi + REL_CLIP + 1],
             jnp.repeat(tab[:, -1:], rel_max - hi, axis=1)]
    ext = jnp.concatenate(parts, axis=1)
    length = n_rows + n_cols - 1
    flipped = jnp.pad(ext[:, ::-1], ((0, 0), (0, 1)))
    shifted = jnp.tile(flipped, (1, n_rows))[:, :n_rows * length].reshape(-1, n_rows, length)
    return shifted[:, :, n_rows - 1:n_rows - 1 + n_cols]


def _clipped_bias_run(tab, rel_first, count):
    n_hi = min(max(rel_first - REL_CLIP, 0), count)
    n_lo = min(max(-REL_CLIP - (rel_first - count + 1), 0), count)
    mid = count - n_hi - n_lo
    top = min(rel_first, REL_CLIP) + REL_CLIP
    return jnp.concatenate([jnp.repeat(tab[:, -1:], n_hi, axis=1), tab[:, top - mid + 1:top + 1][:, ::-1],
                            jnp.repeat(tab[:, :1], n_lo, axis=1)], axis=1)


def _band_bias_kernel(f_ref, o_ref, *, qp, span):
    pieces, width = o_ref.shape[1], o_ref.shape[3]
    bias = pltpu.roll(jnp.broadcast_to(f_ref[...], (qp, f_ref.shape[1])), 0, 1, stride=1, stride_axis=0)[:, :width]
    shift = CHUNK.bit_length() - 1
    r_chunk = lax.broadcasted_iota(jnp.int32, (qp, width), 0) >> shift
    col = lax.broadcasted_iota(jnp.int32, (qp, width), 1)
    in_band = ((col >> shift) >= r_chunk) & ((col >> shift) <= r_chunk + N_PREV)
    for p in range(pieces):
        o_ref[0, p] = jnp.where(in_band & (col >= span - p * qp), bias, -jnp.inf)
        o_ref[1, p] = jnp.where(in_band, bias, -jnp.inf)


def _band_prompt_bias(tables, qb, qp):
    depth, _, nh = tables.shape
    span = N_PREV * CHUNK
    width = span + qp
    period = -(-(width + qp - 1) // LANES) * LANES
    tab = jnp.swapaxes(tables.astype(F32), 1, 2).reshape(depth * nh, -1)
    f = jnp.concatenate([_clipped_bias_run(tab, span, width), jnp.zeros((depth * nh, period - width - (qp - 1)), F32),
                         _clipped_bias_run(tab, span + qp - 1, qp - 1)], axis=1).reshape(depth, nh, 1, period)
    pieces = qb // qp
    return pl.pallas_call(
        functools.partial(_band_bias_kernel, qp=qp, span=span),
        grid=(depth, nh),
        in_specs=[pl.BlockSpec((None, None, 1, period), lambda l, h: (l, h, 0, 0))],
        out_specs=pl.BlockSpec((None, 2, pieces, None, qp, width), lambda l, h: (l, 0, 0, h, 0, 0)),
        out_shape=jax.ShapeDtypeStruct((depth, 2, pieces, nh, qp, width), F32),
        compiler_params=_params("parallel", "parallel"),
        name="band_bias",
    )(f)


def _trunk_layer(x, mem_k, mem_v, gdn_conv, gdn_s, s5_h, rg_conv, rg_h, band_k, band_v, p, *, alpha):
    b, l, d = x.shape
    dg = d // N_MIX
    m = b * l
    layer = p["layer"]
    pa, pb, pc, pd, pdb = _inproj(x.reshape(m, d), p["w_in"], layer, (4 * dg, dg, 2 * dg, 3 * dg, LANES))
    pa, pb, pc, pd, pdb = [t.reshape(b, l, -1) for t in (pa, pb, pc, pd, pdb)]

    pad8 = lambda buf: jnp.pad(buf, ((0, 0), (SUBLANES - (CONV_W - 1), 0), (0, 0)))
    chunk = CHUNK if l % CHUNK == 0 else l
    cps = max(1, min(GDN_CHUNKS_PER_STEP, l // chunk))
    o_a, gdn_s_new = _gdn(pa, pdb, pad8(gdn_conv), gdn_s, p["gdn"], layer, t=chunk, cps=cps)
    gdn_conv_new = pa[:, l - (CONV_W - 1):, :3 * dg]

    ns = p["s5"][0].shape[2] // 2
    h0 = jnp.concatenate([s5_h[..., 0].reshape(b, 1, ns), s5_h[..., 1].reshape(b, 1, ns)], axis=-1)
    o_b, h_last = _s5(pb, h0, p["s5"], layer, t=_tile(l, SCAN_ROWS_PER_STEP))
    s5_h_new = jnp.stack([h_last[:, 0, :ns].reshape(s5_h.shape[:-1]), h_last[:, 0, ns:].reshape(s5_h.shape[:-1])],
                         axis=-1)

    o_c, rg_last = _rglru(pc, pad8(rg_conv), rg_h[:, None, :], p["rglru"], layer, t=_tile(l, SCAN_ROWS_PER_STEP))
    rg_conv_new = pc[:, l - (CONV_W - 1):, :dg]
    rg_h_new = rg_last[:, 0, :]

    hd = dg // H_D
    if band_k is None:
        assert l % BAND_BLOCK == 0, "prompt length must be a multiple of the band-attention block"
        o_d = _band_prompt(pd, p["band_bias_prompt"], layer, qb=BAND_BLOCK, qp=BAND_PIECE)
        keep = min(N_PREV * CHUNK, l)
    else:
        o_d = _band_sample(pd, band_k, band_v, layer, p["band_bias_cache"], p["band_bias_new"])
        keep = l
    band_k_new = pd[:, l - keep:, dg:2 * dg].reshape(b, keep, H_D, hd)
    band_v_new = pd[:, l - keep:, 2 * dg:].reshape(b, keep, H_D, hd)

    x3 = _mix_xattn(x, (o_a, o_b, o_c, o_d), p["w_out"], mem_k, mem_v, p["xa_w_q"], p["xa_w_o"], layer,
                    p["ln_g"], p["ln_b"], alpha=alpha)
    x4 = _mlp(x3.reshape(m, d), p["mlp_w1"], p["mlp_w2"], layer, p["ln_g"], p["ln_b"], alpha=alpha)
    return x4.reshape(b, l, d), (gdn_conv_new, gdn_s_new, s5_h_new, rg_conv_new, rg_h_new, band_k_new, band_v_new)


def kernel(x_prompt, x_sample, state_gdn_conv, state_gdn, state_s5, state_rglru_conv, state_rglru, cache_band_k, cache_band_v, cache_mem_k, cache_mem_v, mem_prompt, w_in, w_out, ln_g, ln_b, gdn_conv_w, gdn_conv_b, gdn_a_log, gdn_dt_bias, gdn_norm_g, s5_lam_re, s5_lam_im, s5_log_dt, s5_b_re, s5_b_im, s5_c_re, s5_c_im, s5_d, s5_w_glu, s5_b_glu, rg_conv_w, rg_conv_b, rg_w_r, rg_b_r, rg_w_i, rg_b_i, rg_lam, band_rel_bias, xa_w_q, xa_w_k, xa_w_v, xa_w_o, mlp_w1, mlp_w2):
    depth = w_in.shape[0]
    bp, lp, d = x_prompt.shape
    bs, ls, _ = x_sample.shape
    n_mem = mem_prompt.shape[1]
    dg = d // N_MIX
    hd_x = d // H_X
    alpha = (2.0 * depth) ** 0.25
    band_rows = cache_band_k.shape[2]

    sizes = (3 * dg, dg, H_A, H_A, dg, dg, dg, 3 * dg)
    offs = [0]
    for s in sizes:
        offs.append(offs[-1] + s)

    w_db = jnp.pad(w_in[:, :, offs[2]:offs[4]], ((0, 0), (0, 0), (0, LANES - 2 * H_A)))
    w_in_bf = jnp.concatenate([w_in[:, :, offs[0]:offs[2]], w_in[:, :, offs[4]:offs[5]], w_in[:, :, offs[5]:offs[7]],
                               w_in[:, :, offs[7]:offs[8]], w_db], axis=2).astype(BF16)
    w_out_bf, xa_w_q_bf, xa_w_k_bf, xa_w_v_bf, xa_w_o_bf, mlp_w1_bf, mlp_w2_bf = [
        w.astype(BF16) for w in (w_out, xa_w_q, xa_w_k, xa_w_v, xa_w_o, mlp_w1, mlp_w2)]

    band_kt = jnp.transpose(cache_band_k, (0, 1, 3, 4, 2))
    band_vt = jnp.transpose(cache_band_v, (0, 1, 3, 4, 2))

    over_layers = jax.vmap
    s5_wb, s5_ap, s5_p8, s5_wc = over_layers(_s5_params)(s5_lam_re, s5_lam_im, s5_log_dt, s5_b_re, s5_b_im,
                                                         s5_c_re, s5_c_im)
    rg_wri = jnp.concatenate([over_layers(_block_diag)(rg_w_r), over_layers(_block_diag)(rg_w_i)], axis=2)
    shared = {
        "w_in": w_in_bf, "w_out": w_out_bf, "ln_g": ln_g, "ln_b": ln_b,
        "gdn": (gdn_conv_w, gdn_conv_b[:, None, :], over_layers(_pad_lanes)(gdn_a_log),
                over_layers(_pad_lanes)(gdn_dt_bias), gdn_norm_g[:, None, :]),
        "s5": (s5_wb, s5_ap, s5_p8, s5_wc, s5_d.reshape(depth, 1, dg), s5_w_glu.astype(BF16), s5_b_glu[:, None, :]),
        "rglru": (rg_conv_w, rg_conv_b[:, None, :], rg_wri.astype(BF16),
                  jnp.concatenate([rg_b_r, rg_b_i], axis=1)[:, None, :], rg_lam[:, None, :]),
        "band_bias_prompt": _band_prompt_bias(band_rel_bias, BAND_BLOCK, BAND_PIECE),
        "band_bias_cache": over_layers(lambda t: _rel_bias_table(t, ls, band_rows, band_rows))(band_rel_bias),
        "band_bias_new": over_layers(lambda t: _rel_bias_table(t, ls, ls, 0))(band_rel_bias),
        "xa_w_q": xa_w_q_bf, "xa_w_o": xa_w_o_bf, "mlp_w1": mlp_w1_bf, "mlp_w2": mlp_w2_bf,
    }

    xp, xs = x_prompt, x_sample
    p_states, s_states = [], []
    for l in range(depth):
        p = dict(shared, layer=l)
        mem2 = mem_prompt.reshape(bp * n_mem, d)
        mk = _matmul(mem2, xa_w_k_bf, l).reshape(bp, n_mem, d)
        mv = _matmul(mem2, xa_w_v_bf, l).reshape(bp, n_mem, d)
        xp, st_p = _trunk_layer(
            xp, mk, mv,
            jnp.zeros((bp, CONV_W - 1, 3 * dg), F32), jnp.zeros((bp, H_A, dg // H_A, dg // H_A), F32),
            jnp.zeros((bp, dg // S5_CH, P_B, 2), F32), jnp.zeros((bp, CONV_W - 1, dg), F32),
            jnp.zeros((bp, dg), F32), None, None, p, alpha=alpha)
        p_states.append(st_p + (mk.reshape(bp, n_mem, H_X, hd_x), mv.reshape(bp, n_mem, H_X, hd_x)))
        xs, st_s = _trunk_layer(
            xs, cache_mem_k, cache_mem_v,
            state_gdn_conv[l], state_gdn[l], state_s5[l], state_rglru_conv[l], state_rglru[l],
            band_kt, band_vt, p, alpha=alpha)
        s_states.append(st_s)

    def stk(states, i):
        return jnp.stack([st[i] for st in states], axis=0)

    return (xp, xs,
            stk(p_states, 0), stk(p_states, 1), stk(p_states, 2), stk(p_states, 3), stk(p_states, 4),
            stk(p_states, 5), stk(p_states, 6), stk(p_states, 7), stk(p_states, 8),
            stk(s_states, 0), stk(s_states, 1), stk(s_states, 2), stk(s_states, 3), stk(s_states, 4),
            stk(s_states, 5), stk(s_states, 6))
```

```python
import functools
import math

import jax
import jax.numpy as jnp
from jax import lax
from jax.experimental import pallas as pl
from jax.experimental.pallas import tpu as pltpu

F32 = jnp.float32
BF16 = jnp.bfloat16

N_MIX = 4
CONV_W = 4
CHUNK = 64
H_A = 4
S5_CH = 16
P_B = 64
RG_C = 8.0
H_D = 4
N_PREV = 8
REL_CLIP = 128
H_X = 4
LN_EPS = 1e-5
NORM_EPS = 1e-6

LANES = 128
SUBLANES = 8
VMEM_LIMIT_BYTES = 56 * 1024 * 1024
GDN_BLOCK = 128
GDN_CHUNKS_PER_STEP = 8
BAND_BLOCK = 1024
BAND_PIECE = 128
SCAN_ROWS_PER_STEP = 1024
XATTN_SEQS_PER_STEP = 4


def _params(*semantics):
    return pltpu.CompilerParams(dimension_semantics=semantics, vmem_limit_bytes=VMEM_LIMIT_BYTES)


def _tile(n, pref):
    t = min(n, pref)
    while n % t:
        t -= SUBLANES
    return t


def _seqs_per_step(b, l, t):
    nb = max(1, min(b, SCAN_ROWS_PER_STEP // t)) if l == t else 1
    while b % nb:
        nb -= 1
    return nb


def _mm(a, b):
    return jnp.dot(a.astype(BF16), b.astype(BF16), preferred_element_type=F32)


def _mm_nt(a, b):
    return lax.dot_general(a.astype(BF16), b.astype(BF16), (((1,), (1,)), ((), ())),
                           preferred_element_type=F32)


def _mm_tn(a, b):
    return lax.dot_general(a.astype(BF16), b.astype(BF16), (((0,), (0,)), ((), ())),
                           preferred_element_type=F32)


def _sigmoid(x):
    return 1.0 / (1.0 + jnp.exp(-x))


def _softplus(x):
    return jnp.maximum(x, 0.0) + jnp.log1p(jnp.exp(-jnp.abs(x)))


def _gelu_tanh(x):
    c = math.sqrt(2.0 / math.pi)
    return 0.5 * x * (1.0 + jnp.tanh(c * (x + 0.044715 * (x * x * x))))


def _layer_norm(z, g, b):
    mu = jnp.mean(z, axis=-1, keepdims=True)
    zc = z - mu
    var = jnp.mean(zc * zc, axis=-1, keepdims=True)
    return zc * lax.rsqrt(var + LN_EPS) * g + b


def _matmul_kernel(x_ref, w_ref, o_ref):
    o_ref[...] = _mm(x_ref[...], w_ref[...])


def _layer_spec(w, layer, block=None, index=None):
    block = tuple(w.shape[1:]) if block is None else block
    index = (lambda *_: (0,) * len(block)) if index is None else index
    return pl.BlockSpec((None,) + block, lambda *g: (layer,) + tuple(index(*g)))


def _matmul(x, w_bf16, layer, tm=512):
    m, k = x.shape
    n = w_bf16.shape[2]
    tm = _tile(m, tm)
    return pl.pallas_call(
        _matmul_kernel,
        grid=(m // tm,),
        in_specs=[pl.BlockSpec((tm, k), lambda i: (i, 0)), _layer_spec(w_bf16, layer)],
        out_specs=pl.BlockSpec((tm, n), lambda i: (i, 0)),
        out_shape=jax.ShapeDtypeStruct((m, n), F32),
        compiler_params=_params("parallel"),
        name="matmul",
    )(x, w_bf16)


def _inproj_kernel(x_ref, w_ref, *o_refs, bounds):
    xb = x_ref[...].astype(BF16)
    for o_ref, (s, e) in zip(o_refs, bounds):
        o_ref[...] = jnp.dot(xb, w_ref[:, s:e], preferred_element_type=F32)


def _inproj(x, w_bf16, layer, widths, tm=1024):
    m, k = x.shape
    tm = _tile(m, tm)
    bounds, s = [], 0
    for w in widths:
        bounds.append((s, s + w))
        s += w
    return pl.pallas_call(
        functools.partial(_inproj_kernel, bounds=tuple(bounds)),
        grid=(m // tm,),
        in_specs=[pl.BlockSpec((tm, k), lambda i: (i, 0)), _layer_spec(w_bf16, layer)],
        out_specs=[pl.BlockSpec((tm, w), lambda i: (i, 0)) for w in widths],
        out_shape=[jax.ShapeDtypeStruct((m, w), F32) for w in widths],
        compiler_params=_params("parallel"),
        name="inproj",
    )(x, w_bf16)


def _inverse_masks(r, c, t):
    neg_diag8 = jnp.where((r >> 3) == (c >> 3), -1.0, 0.0)
    offs, lb = [], 3
    while (1 << lb) < t:
        off = ((r >> (lb + 1)) == (c >> (lb + 1))) & (((r >> lb) & 1) == 1) & (((c >> lb) & 1) == 0)
        offs.append(jnp.where(off, 1.0, 0.0))
        lb += 1
    return neg_diag8, offs


def _unit_lower_inverse_offdiag(a_list, masks):
    neg_diag8, offs = masks
    n1 = [a * neg_diag8 for a in a_list]
    n2 = [_mm(x, x) for x in n1]
    n3 = [_mm(x, x2) for x, x2 in zip(n1, n2)]
    n4 = [_mm(x2, x2) for x2 in n2]
    p = [x + x2 + x3 for x, x2, x3 in zip(n1, n2, n3)]
    pn4 = [_mm(pp, x4) for pp, x4 in zip(p, n4)]
    y = [pp + x4 + px for pp, x4, px in zip(p, n4, pn4)]
    for off in offs:
        m = [a * off for a in a_list]
        z = [mm + _mm(yy, mm) for yy, mm in zip(y, m)]
        zy = [_mm(zz, yy) for zz, yy in zip(z, y)]
        y = [yy - (zz + zzy) for yy, zz, zzy in zip(y, z, zy)]
    return y


def _split3(x):
    h1 = x.astype(BF16)
    r1 = x - h1.astype(F32)
    h2 = r1.astype(BF16)
    h3 = (r1 - h2.astype(F32)).astype(BF16)
    return h1, h2, h3


def _gdn_kernel(qkv_ref, gate_ref, db_ref, cbuf_ref, s0_ref, cw_ref, cb_ref, alog_ref, dtb_ref, ng_ref,
                o_ref, sfin_ref, xp_scr, s_scr, o_scr, *, t, dk):
    i = pl.program_id(1)
    nb, rows, _ = qkv_ref.shape
    tb = nb * rows
    nh = H_A
    dq = nh * dk

    @pl.when(i == 0)
    def _():
        xp_scr[:, 0:SUBLANES, :] = cbuf_ref[...]
        s_scr[...] = s0_ref[...]

    base = SUBLANES - (CONV_W - 1)
    ys = []
    for n in range(nb):
        x = qkv_ref[n]
        xp_scr[n, SUBLANES:SUBLANES + rows, :] = x
        y = xp_scr[n, base:base + rows, :] * cw_ref[0:1, :]
        for j in range(1, CONV_W):
            y = y + xp_scr[n, base + j:base + j + rows, :] * cw_ref[j:j + 1, :]
        xp_scr[n, 0:SUBLANES, :] = x[rows - SUBLANES:rows, :]
        ys.append(y + cb_ref[...])
    y = jnp.concatenate(ys, axis=0)
    y = y * _sigmoid(y)

    db = db_ref[...].reshape(tb, LANES)
    log_a = -jnp.exp(alog_ref[...]) * _softplus(db + dtb_ref[...])
    beta_all = _sigmoid(db)

    bs = min(tb, GDN_BLOCK)
    lt = t.bit_length() - 1
    r = lax.broadcasted_iota(jnp.int32, (bs, bs), 0)
    c = lax.broadcasted_iota(jnp.int32, (bs, bs), 1)
    same = (r >> lt) == (c >> lt)
    causal_neg = jnp.where(same & (r >= c), 0.0, -jnp.inf)
    strict_f = jnp.where(same & (r > c), 1.0, 0.0)
    tril = jnp.where(same & (r >= c), 1.0, 0.0).astype(BF16)
    striu = jnp.where(same & (r < c), 1.0, 0.0).astype(BF16)
    inv_masks = _inverse_masks(r, c, t)

    blocks = list(range(0, tb, bs))
    pairs = [(bi, h) for bi in range(len(blocks)) for h in range(nh)]
    la3 = [_split3(log_a[b0:b0 + bs, :]) for b0 in blocks]
    gc = [sum(jnp.dot(tril, part, preferred_element_type=F32) for part in parts) for parts in la3]
    rv = [sum(jnp.dot(striu, part, preferred_element_type=F32) for part in parts) for parts in la3]
    eg = [jnp.exp(g) for g in gc]
    erv = [jnp.exp(g) for g in rv]
    e_tot = [jnp.exp(g + g2) for g, g2 in zip(gc, rv)]
    gc_rows = [g.T for g in gc]

    def head_cols(z, bi, h, off):
        b0 = blocks[bi]
        return z[b0:b0 + bs, off + h * dk:off + (h + 1) * dk]

    q = [head_cols(y, bi, h, 0) for bi, h in pairs]
    k = [head_cols(y, bi, h, dq) for bi, h in pairs]
    v = [head_cols(y, bi, h, 2 * dq) for bi, h in pairs]
    q = [z * lax.rsqrt(jnp.sum(z * z, axis=-1, keepdims=True) + NORM_EPS) * (dk ** -0.5) for z in q]
    k = [z * lax.rsqrt(jnp.sum(z * z, axis=-1, keepdims=True) + NORM_EPS) for z in k]
    decay = [jnp.exp(gc[bi][:, h:h + 1] - gc_rows[bi][h:h + 1, :] + causal_neg) for bi, h in pairs]
    beta = [beta_all[blocks[bi]:blocks[bi] + bs, nh + h:nh + h + 1] for bi, h in pairs]
    eg_col = [eg[bi][:, h:h + 1] for bi, h in pairs]
    kk = [_mm_nt(z, z) for z in k]
    qk = [_mm_nt(zq, zk) for zq, zk in zip(q, k)]
    a = [(b * z * d) * strict_f for b, z, d in zip(beta, kk, decay)]
    qk = [z * d for z, d in zip(qk, decay)]
    y_inv = _unit_lower_inverse_offdiag(a, inv_masks)
    rhs = [jnp.concatenate([zv * b, zk * (b * e)], axis=-1) for zv, zk, b, e in zip(v, k, beta, eg_col)]
    sol = [z + _mm(yi, z) for yi, z in zip(y_inv, rhs)]
    q_dec = [z * e for z, e in zip(q, eg_col)]
    k_dec = [z * erv[bi][:, h:h + 1] for z, (bi, h) in zip(k, pairs)]

    chunks = [(bi, r0) for bi in range(len(blocks)) for r0 in range(0, bs, t)]
    n_kw = [[_mm_tn(k_dec[bi * nh + h][r0:r0 + t], sol[bi * nh + h][r0:r0 + t]) for h in range(nh)]
            for bi, r0 in chunks]
    states = [s_scr[0, h] for h in range(nh)]
    starts = []
    for ci, ((bi, r0), nk) in enumerate(zip(chunks, n_kw)):
        if nb > 1:
            states = [s_scr[ci, h] for h in range(nh)]
        starts.append(states)
        drop = [_mm(z[:, dk:], s) for z, s in zip(nk, states)]
        states = [s * e_tot[bi][r0:r0 + 1, h:h + 1] + (z[:, :dk] - d)
                  for h, (s, z, d) in enumerate(zip(states, nk, drop))]
        if nb > 1:
            for h in range(nh):
                s_scr[ci, h] = states[h]
    ws = [[_mm(jnp.concatenate([sol[bi * nh + h][r0:r0 + t, dk:], q_dec[bi * nh + h][r0:r0 + t]], axis=0), st[h])
           for h in range(nh)] for (bi, r0), st in zip(chunks, starts)]
    v_news = [[] for _ in pairs]
    o_inter = [[] for _ in pairs]
    for (bi, r0), wc in zip(chunks, ws):
        for h, z in enumerate(wc):
            v_news[bi * nh + h].append(sol[bi * nh + h][r0:r0 + t, :dk] - z[:t])
            o_inter[bi * nh + h].append(z[t:])
    o = [jnp.concatenate(oi, axis=0) + _mm(z, jnp.concatenate(vn, axis=0))
         for oi, z, vn in zip(o_inter, qk, v_news)]
    o = [z * lax.rsqrt(jnp.mean(z * z, axis=-1, keepdims=True) + NORM_EPS) * ng_ref[...] for z in o]
    for (bi, h), z in zip(pairs, o):
        o_scr[blocks[bi]:blocks[bi] + bs, h * dk:(h + 1) * dk] = z
    if nb == 1:
        for h in range(nh):
            s_scr[0, h] = states[h]

    g = gate_ref[...].reshape(tb, dq)
    o_ref[...] = (o_scr[...] * (g * _sigmoid(g))).astype(o_ref.dtype).reshape(nb, rows, dq)

    @pl.when(i == pl.num_programs(1) - 1)
    def _():
        sfin_ref[...] = s_scr[...]


def _gdn(pa, pdb, cbuf8, s0, params, layer, *, t, cps):
    b, l, _ = pa.shape
    nh, dk = s0.shape[1], s0.shape[2]
    dq = nh * dk
    nb = _seqs_per_step(b, l, t)
    rows = t * cps
    return pl.pallas_call(
        functools.partial(_gdn_kernel, t=t, dk=dk),
        grid=(b // nb, l // rows),
        in_specs=[
            pl.BlockSpec((nb, rows, 3 * dq), lambda bi, i: (bi, i, 0)),
            pl.BlockSpec((nb, rows, dq), lambda bi, i: (bi, i, 3)),
            pl.BlockSpec((nb, rows, LANES), lambda bi, i: (bi, i, 0)),
            pl.BlockSpec((nb, SUBLANES, 3 * dq), lambda bi, i: (bi, 0, 0)),
            pl.BlockSpec((nb, nh, dk, dk), lambda bi, i: (bi, 0, 0, 0)),
        ] + [_layer_spec(w, layer) for w in params],
        out_specs=[
            pl.BlockSpec((nb, rows, dq), lambda bi, i: (bi, i, 0)),
            pl.BlockSpec((nb, nh, dk, dk), lambda bi, i: (bi, 0, 0, 0)),
        ],
        out_shape=[jax.ShapeDtypeStruct((b, l, dq), BF16), jax.ShapeDtypeStruct((b, nh, dk, dk), F32)],
        scratch_shapes=[pltpu.VMEM((nb, SUBLANES + rows, 3 * dq), F32), pltpu.VMEM((nb, nh, dk, dk), F32),
                        pltpu.VMEM((nb * rows, dq), F32)],
        compiler_params=_params("parallel", "arbitrary"),
        name="gdn",
    )(pa, pa, pdb, cbuf8, s0, *params)


def _s5_kernel(u_ref, h0_ref, wb_ref, ap_ref, p8_ref, wc_ref, d_ref, wg_ref, bg_ref,
               o_ref, hl_ref, carry_scr, h_scr, *, t, ns):
    i = pl.program_id(1)
    nb, _, dg = u_ref.shape

    @pl.when(i == 0)
    def _():
        carry_scr[...] = h0_ref[...]

    u = u_ref[...].reshape(nb * t, dg)
    ng = nb * t // SUBLANES
    first = lax.broadcasted_iota(jnp.int32, (SUBLANES, dg), 0) == 0
    u_prev = jnp.where(first, 0.0, pltpu.roll(u.reshape(ng, SUBLANES, dg), 1, 1)).reshape(nb * t, dg)
    x = _mm(jnp.concatenate([u, u_prev], axis=-1), wb_ref[...])
    xr = x[:, :ns].reshape(ng, SUBLANES, ns)
    xi = x[:, ns:].reshape(ng, SUBLANES, ns)
    for lvl in range(2):
        s = 2 << lvl
        pr, pi = ap_ref[lvl, :, :ns], ap_ref[lvl, :, ns:]
        sr = pltpu.roll(xr, s, 1)
        si = pltpu.roll(xi, s, 1)
        xr, xi = xr + (pr * sr - pi * si), xi + (pr * si + pi * sr)
    p8r, p8i = p8_ref[:, :ns], p8_ref[:, ns:]
    for n in range(nb):
        cr, ci = carry_scr[n, :, :ns], carry_scr[n, :, ns:]
        for j in range(n * t // SUBLANES, (n + 1) * t // SUBLANES):
            sl = slice(j * SUBLANES, (j + 1) * SUBLANES)
            br = xr[j] + (p8r * cr - p8i * ci)
            bi = xi[j] + (p8r * ci + p8i * cr)
            h_scr[sl, :ns] = br
            h_scr[sl, ns:] = bi
            cr, ci = br[SUBLANES - 1:SUBLANES], bi[SUBLANES - 1:SUBLANES]
        carry_scr[n, :, :ns] = cr
        carry_scr[n, :, ns:] = ci
    hl_ref[...] = carry_scr[...]

    y = _mm(h_scr[...], wc_ref[...]) + d_ref[...] * u
    y = _gelu_tanh(y)
    z = _mm(y, wg_ref[...]) + bg_ref[...]
    o_ref[...] = (y * _sigmoid(z)).astype(o_ref.dtype).reshape(nb, t, dg)


def _s5(pb, h0, params, layer, *, t):
    b, l, dg = pb.shape
    ns2 = params[0].shape[2]
    ns = ns2 // 2
    nb = _seqs_per_step(b, l, t)
    return pl.pallas_call(
        functools.partial(_s5_kernel, t=t, ns=ns),
        grid=(b // nb, l // t),
        in_specs=[
            pl.BlockSpec((nb, t, dg), lambda bi, i: (bi, i, 0)),
            pl.BlockSpec((nb, 1, ns2), lambda bi, i: (bi, 0, 0)),
        ] + [_layer_spec(w, layer) for w in params],
        out_specs=[
            pl.BlockSpec((nb, t, dg), lambda bi, i: (bi, i, 0)),
            pl.BlockSpec((nb, 1, ns2), lambda bi, i: (bi, 0, 0)),
        ],
        out_shape=[jax.ShapeDtypeStruct((b, l, dg), BF16), jax.ShapeDtypeStruct((b, 1, ns2), F32)],
        scratch_shapes=[pltpu.VMEM((nb, 1, ns2), F32), pltpu.VMEM((nb * t, ns2), F32)],
        compiler_params=_params("parallel", "arbitrary"),
        name="s5",
    )(pb, h0, *params)


def _rglru_kernel(xg_ref, cbuf_ref, h0_ref, cw_ref, cb_ref, wri_ref, bri_ref, lam_ref,
                  o_ref, hl_ref, xp_scr, carry_scr, h_scr, *, t, dg):
    i = pl.program_id(1)
    nb = xg_ref.shape[0]

    @pl.when(i == 0)
    def _():
        xp_scr[:, 0:SUBLANES, :] = cbuf_ref[...]
        carry_scr[...] = h0_ref[...]

    base = SUBLANES - (CONV_W - 1)
    ys = []
    for n in range(nb):
        x = xg_ref[n, :, :dg]
        xp_scr[n, SUBLANES:SUBLANES + t, :] = x
        y = xp_scr[n, base:base + t, :] * cw_ref[0:1, :]
        for j in range(1, CONV_W):
            y = y + xp_scr[n, base + j:base + j + t, :] * cw_ref[j:j + 1, :]
        xp_scr[n, 0:SUBLANES, :] = x[t - SUBLANES:t, :]
        ys.append(y + cb_ref[...])
    y = jnp.concatenate(ys, axis=0)
    gb = xg_ref[...][:, :, dg:].reshape(nb * t, dg)

    ri = _mm(y, wri_ref[...]) + bri_ref[...]
    rg = _sigmoid(ri[:, :dg])
    ig = _sigmoid(ri[:, dg:])
    log_a = (-RG_C * rg) * _softplus(-lam_ref[...])
    a = jnp.exp(log_a)
    th = jnp.tanh(log_a)
    xin = jnp.sqrt(-2.0 * th / (1.0 - th)) * (ig * y)

    ng = nb * t // SUBLANES
    a = a.reshape(ng, SUBLANES, dg)
    xin = xin.reshape(ng, SUBLANES, dg)
    row = lax.broadcasted_iota(jnp.int32, (SUBLANES, dg), 0)
    for lvl in range(3):
        s = 1 << lvl
        keep = row >= s
        a_s = jnp.where(keep, pltpu.roll(a, s, 1), 1.0)
        x_s = jnp.where(keep, pltpu.roll(xin, s, 1), 0.0)
        xin = a * x_s + xin
        a = a * a_s
    for n in range(nb):
        cr = carry_scr[n]
        for j in range(n * t // SUBLANES, (n + 1) * t // SUBLANES):
            hb = xin[j] + a[j] * cr
            h_scr[j * SUBLANES:(j + 1) * SUBLANES, :] = hb
            cr = hb[SUBLANES - 1:SUBLANES]
        carry_scr[n] = cr
    hl_ref[...] = carry_scr[...]
    o_ref[...] = (h_scr[...] * _gelu_tanh(gb)).astype(o_ref.dtype).reshape(nb, t, dg)


def _rglru(pc, cbuf8, h0, params, layer, *, t):
    b, l, dg2 = pc.shape
    dg = dg2 // 2
    nb = _seqs_per_step(b, l, t)
    return pl.pallas_call(
        functools.partial(_rglru_kernel, t=t, dg=dg),
        grid=(b // nb, l // t),
        in_specs=[
            pl.BlockSpec((nb, t, dg2), lambda bi, i: (bi, i, 0)),
            pl.BlockSpec((nb, SUBLANES, dg), lambda bi, i: (bi, 0, 0)),
            pl.BlockSpec((nb, 1, dg), lambda bi, i: (bi, 0, 0)),
        ] + [_layer_spec(w, layer) for w in params],
        out_specs=[
            pl.BlockSpec((nb, t, dg), lambda bi, i: (bi, i, 0)),
            pl.BlockSpec((nb, 1, dg), lambda bi, i: (bi, 0, 0)),
        ],
        out_shape=[jax.ShapeDtypeStruct((b, l, dg), BF16), jax.ShapeDtypeStruct((b, 1, dg), F32)],
        scratch_shapes=[pltpu.VMEM((nb, SUBLANES + t, dg), F32), pltpu.VMEM((nb, 1, dg), F32),
                        pltpu.VMEM((nb * t, dg), F32)],
        compiler_params=_params("parallel", "arbitrary"),
        name="rglru",
    )(pc, cbuf8, h0, *params)


def _band_prompt_kernel(q_ref, kc_ref, vc_ref, kp_ref, vp_ref, bias_ref, o_ref, *, qb, qp, hd):
    span = N_PREV * CHUNK
    q = q_ref[0] * (hd ** -0.5)
    k = jnp.concatenate([kp_ref[0], kc_ref[0]], axis=0)
    v = jnp.concatenate([vp_ref[0], vc_ref[0]], axis=0)
    units = [(h, p) for h in range(H_D) for p in range(qb // qp)]
    col = lambda h: slice(h * hd, (h + 1) * hd)
    win = lambda p: slice(p * qp, span + (p + 1) * qp)
    s, e, den, o = {}, {}, {}, []
    for n in range(len(units) + 2):
        if n < len(units):
            h, p = units[n]
            s[n] = _mm_nt(q[p * qp:(p + 1) * qp, col(h)], k[win(p), col(h)]) + bias_ref[0, p, h]
        if 0 <= n - 1 < len(units):
            z = s.pop(n - 1)
            e[n - 1] = jnp.exp(z - jnp.max(z, axis=-1, keepdims=True))
            den[n - 1] = jnp.sum(e[n - 1], axis=-1, keepdims=True)
        if 0 <= n - 2 < len(units):
            h, p = units[n - 2]
            o.append(_mm(e.pop(n - 2), v[win(p), col(h)]) / den.pop(n - 2))
    npc = qb // qp
    o_ref[0] = jnp.concatenate([jnp.concatenate(o[h * npc:(h + 1) * npc], axis=0) for h in range(H_D)],
                               axis=-1).astype(o_ref.dtype)


def _band_prompt(pd, bias, layer, *, qb, qp):
    b, l, w3 = pd.shape
    w = w3 // 3
    hd = w // H_D
    span = N_PREV * CHUNK
    prev = lambda bi, i: jnp.maximum(i * (qb // span) - 1, 0)
    return pl.pallas_call(
        functools.partial(_band_prompt_kernel, qb=qb, qp=qp, hd=hd),
        grid=(b, l // qb),
        in_specs=[
            pl.BlockSpec((1, qb, w), lambda bi, i: (bi, i, 0)),
            pl.BlockSpec((1, qb, w), lambda bi, i: (bi, i, 1)),
            pl.BlockSpec((1, qb, w), lambda bi, i: (bi, i, 2)),
            pl.BlockSpec((1, span, w), lambda bi, i: (bi, prev(bi, i), 1)),
            pl.BlockSpec((1, span, w), lambda bi, i: (bi, prev(bi, i), 2)),
            pl.BlockSpec((None, 1) + bias.shape[2:], lambda bi, i: (layer, jnp.minimum(i, 1), 0, 0, 0, 0)),
        ],
        out_specs=pl.BlockSpec((1, qb, w), lambda bi, i: (bi, i, 0)),
        out_shape=jax.ShapeDtypeStruct((b, l, w), BF16),
        compiler_params=_params("parallel", "arbitrary"),
        name="band_prompt",
    )(pd, pd, pd, pd, pd, bias)


def _band_sample_kernel(qkv_ref, kt_ref, vt_ref, bc_ref, bn_ref, o_ref, *, hd):
    nb = qkv_ref.shape[0]
    w = H_D * hd
    units = [(n, h, slice(h * hd, (h + 1) * hd)) for n in range(nb) for h in range(H_D)]
    q = [qkv_ref[n, :, :w] * (hd ** -0.5) for n in range(nb)]
    kn = [qkv_ref[n, :, w:2 * w] for n in range(nb)]
    vn = [qkv_ref[n, :, 2 * w:] for n in range(nb)]
    sc = [_mm(q[n][:, sl], kt_ref[n, h]) + bc_ref[h] for n, h, sl in units]
    sn = [_mm_nt(q[n][:, sl], kn[n][:, sl]) + bn_ref[h] for n, h, sl in units]
    m = [jnp.maximum(jnp.max(c, axis=-1, keepdims=True), jnp.max(z, axis=-1, keepdims=True)) for c, z in zip(sc, sn)]
    pc = [jnp.exp(c - z) for c, z in zip(sc, m)]
    pn = [jnp.exp(c - z) for c, z in zip(sn, m)]
    den = [jnp.sum(c, axis=-1, keepdims=True) + jnp.sum(z, axis=-1, keepdims=True) for c, z in zip(pc, pn)]
    outs = [(_mm_nt(c, vt_ref[n, h]) + _mm(z, vn[n][:, sl])) / d
            for (n, h, sl), c, z, d in zip(units, pc, pn, den)]
    for n in range(nb):
        o_ref[n] = jnp.concatenate(outs[n * H_D:(n + 1) * H_D], axis=-1).astype(o_ref.dtype)


def _band_sample(pd, kt_cache, vt_cache, layer, bias_c, bias_n):
    b, l, w3 = pd.shape
    w = w3 // 3
    hd = w // H_D
    rows = kt_cache.shape[-1]
    nb = XATTN_SEQS_PER_STEP
    while b % nb:
        nb -= 1
    cache_spec = pl.BlockSpec((None, nb, H_D, hd, rows), lambda bi: (layer, bi, 0, 0, 0))
    return pl.pallas_call(
        functools.partial(_band_sample_kernel, hd=hd),
        grid=(b // nb,),
        in_specs=[
            pl.BlockSpec((nb, l, w3), lambda bi: (bi, 0, 0)),
            cache_spec, cache_spec, _layer_spec(bias_c, layer), _layer_spec(bias_n, layer),
        ],
        out_specs=pl.BlockSpec((nb, l, w), lambda bi: (bi, 0, 0)),
        out_shape=jax.ShapeDtypeStruct((b, l, w), BF16),
        compiler_params=_params("parallel"),
        name="band_sample",
    )(pd, kt_cache, vt_cache, bias_c, bias_n)


def _row_parts(rows, parts):
    if rows % (parts * 2 * SUBLANES):
        parts = 1
    step = rows // parts
    return [slice(n * step, (n + 1) * step) for n in range(parts)]


def _mix_xattn_kernel(x_ref, oa_ref, ob_ref, oc_ref, od_ref, wm_ref, mk_ref, mv_ref, wq_ref, wo_ref, g_ref, b_ref,
                      o_ref, *, alpha, hd, head_axis):
    nb, tm, d = x_ref.shape
    dg = oa_ref.shape[2]
    heads = [slice(h * hd, (h + 1) * hd) for h in range(H_X)]
    if head_axis:
        n_mem = mk_ref.shape[1]
        mk = [mk_ref[n].reshape(n_mem * H_X, hd).astype(BF16) for n in range(nb)]
        mv = [mv_ref[n].reshape(n_mem * H_X, hd).astype(BF16) for n in range(nb)]
    else:
        mk = [[mk_ref[n, :, hs].astype(BF16) for hs in heads] for n in range(nb)]
        mv = [[mv_ref[n, :, hs].astype(BF16) for hs in heads] for n in range(nb)]
    if nb == 1:
        parts = _row_parts(tm, 2)
        rows_of = lambda ref, sl: ref[0, sl, :]
        units = [(n, slice(0, sl.stop - sl.start), 0) for n, sl in enumerate(parts)]
    else:
        parts = [slice(0, nb * tm)]
        rows_of = lambda ref, sl: ref[...].reshape(nb * tm, ref.shape[2])
        units = [(0, slice(n * tm, (n + 1) * tm), n) for n in range(nb)]
    mix = [sum(jnp.dot(rows_of(r, sl), wm_ref[n * dg:(n + 1) * dg, :], preferred_element_type=F32)
               for n, r in enumerate((oa_ref, ob_ref, oc_ref, od_ref))) for sl in parts]
    xs = [_layer_norm(alpha * rows_of(x_ref, sl) + z, g_ref[0:1, :], b_ref[0:1, :]) for sl, z in zip(parts, mix)]
    q = [(_mm(x, wq_ref[...]) * (hd ** -0.5)).astype(BF16) for x in xs]
    if head_axis:
        rows_u = units[0][1].stop - units[0][1].start
        row = lax.broadcasted_iota(jnp.int32, (H_X * rows_u, n_mem * H_X), 0)
        r_head = sum((row >= h * rows_u).astype(jnp.int32) for h in range(1, H_X))
        c_head = lax.broadcasted_iota(jnp.int32, (H_X * rows_u, n_mem * H_X), 1) & (H_X - 1)
        own_head = jnp.where(r_head == c_head, 0.0, -jnp.inf)
        qs = [jnp.concatenate([q[p][rows, hs] for hs in heads], axis=0) for p, rows, _ in units]
        s = [_mm_nt(z, mk[n]) + own_head for z, (_, _, n) in zip(qs, units)]
        e = [jnp.exp(z - jnp.max(z, axis=-1, keepdims=True)) for z in s]
        pr = [z * (1.0 / jnp.sum(z, axis=-1, keepdims=True)) for z in e]
        pv = [_mm(z, mv[n]) for z, (_, _, n) in zip(pr, units)]
        pv = [jnp.concatenate([z[h * rows_u:(h + 1) * rows_u] for h in range(H_X)], axis=-1) for z in pv]
    else:
        s = [[_mm_nt(q[p][rows, hs], kh) for hs, kh in zip(heads, mk[n])] for p, rows, n in units]
        e = [[jnp.exp(z - jnp.max(z, axis=-1, keepdims=True)) for z in su] for su in s]
        pr = [[z * (1.0 / jnp.sum(z, axis=-1, keepdims=True)) for z in eu] for eu in e]
        pv = [jnp.concatenate([_mm(z, vh) for z, vh in zip(pu, mv[n])], axis=-1) for pu, (_, _, n) in zip(pr, units)]
    pv = [jnp.concatenate([z for z, (p, _, _) in zip(pv, units) if p == n], axis=0) for n in range(len(parts))]
    att = [_mm(z, wo_ref[...]) for z in pv]
    out = [_layer_norm(alpha * x + z, g_ref[1:2, :], b_ref[1:2, :]) for x, z in zip(xs, att)]
    if nb == 1:
        for sl, z in zip(parts, out):
            o_ref[0, sl, :] = z
    else:
        o_ref[...] = out[0].reshape(nb, tm, d)


def _mix_xattn(x, mixed, w_out, mk, mv, wq, wo, layer, g, b, *, alpha, rows=1024):
    bsz, l, d = x.shape
    dg = mixed[0].shape[2]
    hd = d // H_X
    tm = _tile(l, rows)
    nb = max(1, min(bsz, XATTN_SEQS_PER_STEP, rows // l)) if tm == l else 1
    while bsz % nb:
        nb -= 1
    row = lambda width: pl.BlockSpec((nb, tm, width), lambda bi, i: (bi, i, 0))
    if mk.ndim == 3:
        mem_spec = pl.BlockSpec((nb,) + mk.shape[1:], lambda bi, i: (bi, 0, 0))
    else:
        mem_spec = pl.BlockSpec((None, nb) + mk.shape[2:], lambda bi, i: (layer, bi, 0, 0, 0))
    return pl.pallas_call(
        functools.partial(_mix_xattn_kernel, alpha=alpha, hd=hd, head_axis=mk.ndim != 3),
        grid=(bsz // nb, l // tm),
        in_specs=[
            row(d), row(dg), row(dg), row(dg), row(dg), _layer_spec(w_out, layer),
            mem_spec, mem_spec,
            _layer_spec(wq, layer), _layer_spec(wo, layer), _layer_spec(g, layer), _layer_spec(b, layer),
        ],
        out_specs=row(d),
        out_shape=jax.ShapeDtypeStruct((bsz, l, d), F32),
        compiler_params=_params("parallel", "parallel"),
        name="mix_xattn_ln",
    )(x, *mixed, w_out, mk, mv, wq, wo, g, b)


def _mlp_kernel(x_ref, w1_ref, w2_ref, g_ref, b_ref, o_ref, xb_scr, *, alpha, parts):
    f = pl.program_id(1)
    sl = _row_parts(x_ref.shape[0], parts)
    last = pl.num_programs(1) - 1

    def step(first, finish):
        if first:
            for s in sl:
                xb_scr[s, :] = x_ref[s, :].astype(BF16)
        hid = [jnp.maximum(jnp.dot(xb_scr[s, :], w1_ref[...], preferred_element_type=F32), 0.0) for s in sl]
        act = [(z * z).astype(BF16) for z in hid]
        for s, z in zip(sl, act):
            acc = jnp.dot(z, w2_ref[...], preferred_element_type=F32)
            if not first:
                acc = o_ref[s, :] + acc
            o_ref[s, :] = _layer_norm(alpha * x_ref[s, :] + acc, g_ref[2:3, :], b_ref[2:3, :]) if finish else acc

    pl.when(f == 0)(lambda: step(True, False))
    pl.when(jnp.logical_and(f > 0, f < last))(lambda: step(False, False))
    pl.when(f == last)(lambda: step(False, True))


def _mlp(x, w1, w2, layer, g, b, *, alpha, tm=1024, tf=2048):
    m, d = x.shape
    dff = w1.shape[2]
    tf = _tile(dff, tf if m >= tm else tf // 2)
    tm = _tile(m, tm)
    assert dff // tf >= 2, "the kernel distinguishes the first and the last hidden tile"
    return pl.pallas_call(
        functools.partial(_mlp_kernel, alpha=alpha, parts=4),
        grid=(m // tm, dff // tf),
        in_specs=[
            pl.BlockSpec((tm, d), lambda i, f: (i, 0)),
            _layer_spec(w1, layer, (d, tf), lambda i, f: (0, f)),
            _layer_spec(w2, layer, (tf, d), lambda i, f: (f, 0)),
            _layer_spec(g, layer), _layer_spec(b, layer),
        ],
        out_specs=pl.BlockSpec((tm, d), lambda i, f: (i, 0)),
        out_shape=jax.ShapeDtypeStruct((m, d), F32),
        scratch_shapes=[pltpu.VMEM((tm, d), BF16)],
        compiler_params=_params("parallel", "arbitrary"),
        name="mlp_ln",
    )(x, w1, w2, g, b)


def _block_diag(blocks):
    g, r, c = blocks.shape
    eye = jnp.eye(g, dtype=blocks.dtype)
    return (eye[:, None, :, None] * blocks[:, :, None, :]).reshape(g * r, g * c)


def _pad_lanes(v):
    return jnp.pad(v, (0, LANES - v.shape[0]))[None, :]


def _s5_params(lam_re, lam_im, log_dt, b_re, b_im, c_re, c_im):
    dt = jnp.exp(log_dt)[:, None]
    mag = jnp.exp(lam_re * dt)
    ar, ai = mag * jnp.cos(lam_im * dt), mag * jnp.sin(lam_im * dt)
    den = lam_re * lam_re + lam_im * lam_im
    fr = ((ar - 1.0) * lam_re + ai * lam_im) / den
    fi = (ai * lam_re - (ar - 1.0) * lam_im) / den
    bbr = fr[..., None] * b_re - fi[..., None] * b_im
    bbi = fr[..., None] * b_im + fi[..., None] * b_re
    wb = jnp.concatenate([_block_diag(jnp.swapaxes(bbr, 1, 2)), _block_diag(jnp.swapaxes(bbi, 1, 2))], axis=1)
    wc = jnp.concatenate([_block_diag(jnp.swapaxes(c_re, 1, 2)), -_block_diag(jnp.swapaxes(c_im, 1, 2))], axis=0)
    ar, ai = ar.reshape(-1), ai.reshape(-1)
    pows = [(ar, ai)]
    for _ in range(SUBLANES - 1):
        pr, pi = pows[-1]
        pows.append((pr * ar - pi * ai, pr * ai + pi * ar))
    cat = lambda idx: jnp.stack([jnp.concatenate(pows[n]) for n in idx], axis=0)
    row = jnp.arange(SUBLANES)[:, None]
    shift_pows = jnp.stack([jnp.where(row >= s, cat((s - 1,)), 0.0) for s in (2, 4)])
    wbr, wbi = wb[:, :ar.shape[0]], wb[:, ar.shape[0]:]
    wb_lag = jnp.concatenate([ar * wbr - ai * wbi, ar * wbi + ai * wbr], axis=1)
    wb2 = jnp.concatenate([wb, wb_lag], axis=0)
    return wb2.astype(BF16), shift_pows, cat(range(SUBLANES)), wc.astype(BF16)


def _rel_bias_table(table, n_rows, n_cols, offset):
    tab = table.astype(F32).T
    rel_min, rel_max = offset - (n_cols - 1), offset + n_rows - 1
    lo, hi = max(rel_min, -REL_CLIP), min(rel_max, REL_CLIP)
    parts = [jnp.repeat(tab[:, :1], lo - rel_min, axis=1), tab[:, lo + REL_CLIP:hi + REL_CLIP + 1],
             jnp.repeat(tab[:, -1:], rel_max - hi, axis=1)]
    ext = jnp.concatenate(parts, axis=1)
    length = n_rows + n_cols - 1
    flipped = jnp.pad(ext[:, ::-1], ((0, 0), (0, 1)))
    shifted = jnp.tile(flipped, (1, n_rows))[:, :n_rows * length].reshape(-1, n_rows, length)
    return shifted[:, :, n_rows - 1:n_rows - 1 + n_cols]


def _clipped_bias_run(tab, rel_first, count):
    n_hi = min(max(rel_first - REL_CLIP, 0), count)
    n_lo = min(max(-REL_CLIP - (rel_first - count + 1), 0), count)
    mid = count - n_hi - n_lo
    top = min(rel_first, REL_CLIP) + REL_CLIP
    return jnp.concatenate([jnp.repeat(tab[:, -1:], n_hi, axis=1), tab[:, top - mid + 1:top + 1][:, ::-1],
                            jnp.repeat(tab[:, :1], n_lo, axis=1)], axis=1)


def _band_bias_kernel(f_ref, o_ref, *, qp, span):
    pieces, width = o_ref.shape[1], o_ref.shape[3]
    bias = pltpu.roll(jnp.broadcast_to(f_ref[...], (qp, f_ref.shape[1])), 0, 1, stride=1, stride_axis=0)[:, :width]
    shift = CHUNK.bit_length() - 1
    r_chunk = lax.broadcasted_iota(jnp.int32, (qp, width), 0) >> shift
    col = lax.broadcasted_iota(jnp.int32, (qp, width), 1)
    in_band = ((col >> shift) >= r_chunk) & ((col >> shift) <= r_chunk + N_PREV)
    for p in range(pieces):
        o_ref[0, p] = jnp.where(in_band & (col >= span - p * qp), bias, -jnp.inf)
        o_ref[1, p] = jnp.where(in_band, bias, -jnp.inf)


def _band_prompt_bias(tables, qb, qp):
    depth, _, nh = tables.shape
    span = N_PREV * CHUNK
    width = span + qp
    period = -(-(width + qp - 1) // LANES) * LANES
    tab = jnp.swapaxes(tables.astype(F32), 1, 2).reshape(depth * nh, -1)
    f = jnp.concatenate([_clipped_bias_run(tab, span, width), jnp.zeros((depth * nh, period - width - (qp - 1)), F32),
                         _clipped_bias_run(tab, span + qp - 1, qp - 1)], axis=1).reshape(depth, nh, 1, period)
    pieces = qb // qp
    return pl.pallas_call(
        functools.partial(_band_bias_kernel, qp=qp, span=span),
        grid=(depth, nh),
        in_specs=[pl.BlockSpec((None, None, 1, period), lambda l, h: (l, h, 0, 0))],
        out_specs=pl.BlockSpec((None, 2, pieces, None, qp, width), lambda l, h: (l, 0, 0, h, 0, 0)),
        out_shape=jax.ShapeDtypeStruct((depth, 2, pieces, nh, qp, width), F32),
        compiler_params=_params("parallel", "parallel"),
        name="band_bias",
    )(f)


def _trunk_layer(x, mem_k, mem_v, gdn_conv, gdn_s, s5_h, rg_conv, rg_h, band_k, band_v, p, *, alpha):
    b, l, d = x.shape
    dg = d // N_MIX
    m = b * l
    layer = p["layer"]
    pa, pb, pc, pd, pdb = _inproj(x.reshape(m, d), p["w_in"], layer, (4 * dg, dg, 2 * dg, 3 * dg, LANES))
    pa, pb, pc, pd, pdb = [t.reshape(b, l, -1) for t in (pa, pb, pc, pd, pdb)]

    pad8 = lambda buf: jnp.pad(buf, ((0, 0), (SUBLANES - (CONV_W - 1), 0), (0, 0)))
    chunk = CHUNK if l % CHUNK == 0 else l
    cps = max(1, min(GDN_CHUNKS_PER_STEP, l // chunk))
    o_a, gdn_s_new = _gdn(pa, pdb, pad8(gdn_conv), gdn_s, p["gdn"], layer, t=chunk, cps=cps)
    gdn_conv_new = pa[:, l - (CONV_W - 1):, :3 * dg]

    ns = p["s5"][0].shape[2] // 2
    h0 = jnp.concatenate([s5_h[..., 0].reshape(b, 1, ns), s5_h[..., 1].reshape(b, 1, ns)], axis=-1)
    o_b, h_last = _s5(pb, h0, p["s5"], layer, t=_tile(l, SCAN_ROWS_PER_STEP))
    s5_h_new = jnp.stack([h_last[:, 0, :ns].reshape(s5_h.shape[:-1]), h_last[:, 0, ns:].reshape(s5_h.shape[:-1])],
                         axis=-1)

    o_c, rg_last = _rglru(pc, pad8(rg_conv), rg_h[:, None, :], p["rglru"], layer, t=_tile(l, SCAN_ROWS_PER_STEP))
    rg_conv_new = pc[:, l - (CONV_W - 1):, :dg]
    rg_h_new = rg_last[:, 0, :]

    hd = dg // H_D
    if band_k is None:
        assert l % BAND_BLOCK == 0, "prompt length must be a multiple of the band-attention block"
        o_d = _band_prompt(pd, p["band_bias_prompt"], layer, qb=BAND_BLOCK, qp=BAND_PIECE)
        keep = min(N_PREV * CHUNK, l)
    else:
        o_d = _band_sample(pd, band_k, band_v, layer, p["band_bias_cache"], p["band_bias_new"])
        keep = l
    band_k_new = pd[:, l - keep:, dg:2 * dg].reshape(b, keep, H_D, hd)
    band_v_new = pd[:, l - keep:, 2 * dg:].reshape(b, keep, H_D, hd)

    x3 = _mix_xattn(x, (o_a, o_b, o_c, o_d), p["w_out"], mem_k, mem_v, p["xa_w_q"], p["xa_w_o"], layer,
                    p["ln_g"], p["ln_b"], alpha=alpha)
    x4 = _mlp(x3.reshape(m, d), p["mlp_w1"], p["mlp_w2"], layer, p["ln_g"], p["ln_b"], alpha=alpha)
    return x4.reshape(b, l, d), (gdn_conv_new, gdn_s_new, s5_h_new, rg_conv_new, rg_h_new, band_k_new, band_v_new)


def kernel(x_prompt, x_sample, state_gdn_conv, state_gdn, state_s5, state_rglru_conv, state_rglru, cache_band_k, cache_band_v, cache_mem_k, cache_mem_v, mem_prompt, w_in, w_out, ln_g, ln_b, gdn_conv_w, gdn_conv_b, gdn_a_log, gdn_dt_bias, gdn_norm_g, s5_lam_re, s5_lam_im, s5_log_dt, s5_b_re, s5_b_im, s5_c_re, s5_c_im, s5_d, s5_w_glu, s5_b_glu, rg_conv_w, rg_conv_b, rg_w_r, rg_b_r, rg_w_i, rg_b_i, rg_lam, band_rel_bias, xa_w_q, xa_w_k, xa_w_v, xa_w_o, mlp_w1, mlp_w2):
    depth = w_in.shape[0]
    bp, lp, d = x_prompt.shape
    bs, ls, _ = x_sample.shape
    n_mem = mem_prompt.shape[1]
    dg = d // N_MIX
    hd_x = d // H_X
    alpha = (2.0 * depth) ** 0.25
    band_rows = cache_band_k.shape[2]

    sizes = (3 * dg, dg, H_A, H_A, dg, dg, dg, 3 * dg)
    offs = [0]
    for s in sizes:
        offs.append(offs[-1] + s)

    w_db = jnp.pad(w_in[:, :, offs[2]:offs[4]], ((0, 0), (0, 0), (0, LANES - 2 * H_A)))
    w_in_bf = jnp.concatenate([w_in[:, :, offs[0]:offs[2]], w_in[:, :, offs[4]:offs[5]], w_in[:, :, offs[5]:offs[7]],
                               w_in[:, :, offs[7]:offs[8]], w_db], axis=2).astype(BF16)
    w_out_bf, xa_w_q_bf, xa_w_k_bf, xa_w_v_bf, xa_w_o_bf, mlp_w1_bf, mlp_w2_bf = [
        w.astype(BF16) for w in (w_out, xa_w_q, xa_w_k, xa_w_v, xa_w_o, mlp_w1, mlp_w2)]

    band_kt = jnp.transpose(cache_band_k, (0, 1, 3, 4, 2))
    band_vt = jnp.transpose(cache_band_v, (0, 1, 3, 4, 2))

    over_layers = jax.vmap
    s5_wb, s5_ap, s5_p8, s5_wc = over_layers(_s5_params)(s5_lam_re, s5_lam_im, s5_log_dt, s5_b_re, s5_b_im,
                                                         s5_c_re, s5_c_im)
    rg_wri = jnp.concatenate([over_layers(_block_diag)(rg_w_r), over_layers(_block_diag)(rg_w_i)], axis=2)
    shared = {
        "w_in": w_in_bf, "w_out": w_out_bf, "ln_g": ln_g, "ln_b": ln_b,
        "gdn": (gdn_conv_w, gdn_conv_b[:, None, :], over_layers(_pad_lanes)(gdn_a_log),
                over_layers(_pad_lanes)(gdn_dt_bias), gdn_norm_g[:, None, :]),
        "s5": (s5_wb, s5_ap, s5_p8, s5_wc, s5_d.reshape(depth, 1, dg), s5_w_glu.astype(BF16), s5_b_glu[:, None, :]),
        "rglru": (rg_conv_w, rg_conv_b[:, None, :], rg_wri.astype(BF16),
                  jnp.concatenate([rg_b_r, rg_b_i], axis=1)[:, None, :], rg_lam[:, None, :]),
        "band_bias_prompt": _band_prompt_bias(band_rel_bias, BAND_BLOCK, BAND_PIECE),
        "band_bias_cache": over_layers(lambda t: _rel_bias_table(t, ls, band_rows, band_rows))(band_rel_bias),
        "band_bias_new": over_layers(lambda t: _rel_bias_table(t, ls, ls, 0))(band_rel_bias),
        "xa_w_q": xa_w_q_bf, "xa_w_o": xa_w_o_bf, "mlp_w1": mlp_w1_bf, "mlp_w2": mlp_w2_bf,
    }

    xp, xs = x_prompt, x_sample
    p_states, s_states = [], []
    for l in range(depth):
        p = dict(shared, layer=l)
        mem2 = mem_prompt.reshape(bp * n_mem, d)
        mk = _matmul(mem2, xa_w_k_bf, l).reshape(bp, n_mem, d)
        mv = _matmul(mem2, xa_w_v_bf, l).reshape(bp, n_mem, d)
        xp, st_p = _trunk_layer(
            xp, mk, mv,
            jnp.zeros((bp, CONV_W - 1, 3 * dg), F32), jnp.zeros((bp, H_A, dg // H_A, dg // H_A), F32),
            jnp.zeros((bp, dg // S5_CH, P_B, 2), F32), jnp.zeros((bp, CONV_W - 1, dg), F32),
            jnp.zeros((bp, dg), F32), None, None, p, alpha=alpha)
        p_states.append(st_p + (mk.reshape(bp, n_mem, H_X, hd_x), mv.reshape(bp, n_mem, H_X, hd_x)))
        xs, st_s = _trunk_layer(
            xs, cache_mem_k, cache_mem_v,
            state_gdn_conv[l], state_gdn[l], state_s5[l], state_rglru_conv[l], state_rglru[l],
            band_kt, band_vt, p, alpha=alpha)
        s_states.append(st_s)

    def stk(states, i):
        return jnp.stack([st[i] for st in states], axis=0)

    return (xp, xs,
            stk(p_states, 0), stk(p_states, 1), stk(p_states, 2), stk(p_states, 3), stk(p_states, 4),
            stk(p_states, 5), stk(p_states, 6), stk(p_states, 7), stk(p_states, 8),
            stk(s_states, 0), stk(s_states, 1), stk(s_states, 2), stk(s_states, 3), stk(s_states, 4),
            stk(s_states, 5), stk(s_states, 6))
```

```python
import functools
import math

import jax
import jax.numpy as jnp
from jax import lax
from jax.experimental import pallas as pl
from jax.experimental.pallas import tpu as pltpu

F32 = jnp.float32
BF16 = jnp.bfloat16

N_MIX = 4
CONV_W = 4
CHUNK = 64
H_A = 4
S5_CH = 16
P_B = 64
RG_C = 8.0
H_D = 4
N_PREV = 8
REL_CLIP = 128
H_X = 4
LN_EPS = 1e-5
NORM_EPS = 1e-6

LANES = 128
SUBLANES = 8
VMEM_LIMIT_BYTES = 56 * 1024 * 1024
GDN_BLOCK = 128
GDN_CHUNKS_PER_STEP = 8
BAND_BLOCK = 1024
BAND_PIECE = 128
SCAN_ROWS_PER_STEP = 1024
XATTN_SEQS_PER_STEP = 4


def _params(*semantics):
    return pltpu.CompilerParams(dimension_semantics=semantics, vmem_limit_bytes=VMEM_LIMIT_BYTES)


def _tile(n, pref):
    t = min(n, pref)
    while n % t:
        t -= SUBLANES
    return t


def _seqs_per_step(b, l, t):
    nb = max(1, min(b, SCAN_ROWS_PER_STEP // t)) if l == t else 1
    while b % nb:
        nb -= 1
    return nb


def _mm(a, b):
    return jnp.dot(a.astype(BF16), b.astype(BF16), preferred_element_type=F32)


def _mm_nt(a, b):
    return lax.dot_general(a.astype(BF16), b.astype(BF16), (((1,), (1,)), ((), ())),
                           preferred_element_type=F32)


def _mm_tn(a, b):
    return lax.dot_general(a.astype(BF16), b.astype(BF16), (((0,), (0,)), ((), ())),
                           preferred_element_type=F32)


def _sigmoid(x):
    return 1.0 / (1.0 + jnp.exp(-x))


def _softplus(x):
    return jnp.maximum(x, 0.0) + jnp.log1p(jnp.exp(-jnp.abs(x)))


def _gelu_tanh(x):
    c = math.sqrt(2.0 / math.pi)
    return 0.5 * x * (1.0 + jnp.tanh(c * (x + 0.044715 * (x * x * x))))


def _layer_norm(z, g, b):
    mu = jnp.mean(z, axis=-1, keepdims=True)
    zc = z - mu
    var = jnp.mean(zc * zc, axis=-1, keepdims=True)
    return zc * lax.rsqrt(var + LN_EPS) * g + b


def _matmul_kernel(x_ref, w_ref, o_ref):
    o_ref[...] = _mm(x_ref[...], w_ref[...])


def _layer_spec(w, layer, block=None, index=None):
    block = tuple(w.shape[1:]) if block is None else block
    index = (lambda *_: (0,) * len(block)) if index is None else index
    return pl.BlockSpec((None,) + block, lambda *g: (layer,) + tuple(index(*g)))


def _matmul(x, w_bf16, layer, tm=512):
    m, k = x.shape
    n = w_bf16.shape[2]
    tm = _tile(m, tm)
    return pl.pallas_call(
        _matmul_kernel,
        grid=(m // tm,),
        in_specs=[pl.BlockSpec((tm, k), lambda i: (i, 0)), _layer_spec(w_bf16, layer)],
        out_specs=pl.BlockSpec((tm, n), lambda i: (i, 0)),
        out_shape=jax.ShapeDtypeStruct((m, n), F32),
        compiler_params=_params("parallel"),
        name="matmul",
    )(x, w_bf16)


def _inproj_kernel(x_ref, w_ref, *o_refs, bounds):
    xb = x_ref[...].astype(BF16)
    for o_ref, (s, e) in zip(o_refs, bounds):
        o_ref[...] = jnp.dot(xb, w_ref[:, s:e], preferred_element_type=F32)


def _inproj(x, w_bf16, layer, widths, tm=1024):
    m, k = x.shape
    tm = _tile(m, tm)
    bounds, s = [], 0
    for w in widths:
        bounds.append((s, s + w))
        s += w
    return pl.pallas_call(
        functools.partial(_inproj_kernel, bounds=tuple(bounds)),
        grid=(m // tm,),
        in_specs=[pl.BlockSpec((tm, k), lambda i: (i, 0)), _layer_spec(w_bf16, layer)],
        out_specs=[pl.BlockSpec((tm, w), lambda i: (i, 0)) for w in widths],
        out_shape=[jax.ShapeDtypeStruct((m, w), F32) for w in widths],
        compiler_params=_params("parallel"),
        name="inproj",
    )(x, w_bf16)


def _inverse_masks(r, c, t):
    neg_diag8 = jnp.where((r >> 3) == (c >> 3), -1.0, 0.0)
    offs, lb = [], 3
    while (1 << lb) < t:
        off = ((r >> (lb + 1)) == (c >> (lb + 1))) & (((r >> lb) & 1) == 1) & (((c >> lb) & 1) == 0)
        offs.append(jnp.where(off, 1.0, 0.0))
        lb += 1
    return neg_diag8, offs


def _unit_lower_inverse_offdiag(a_list, masks):
    neg_diag8, offs = masks
    n1 = [a * neg_diag8 for a in a_list]
    n2 = [_mm(x, x) for x in n1]
    n3 = [_mm(x, x2) for x, x2 in zip(n1, n2)]
    n4 = [_mm(x2, x2) for x2 in n2]
    p = [x + x2 + x3 for x, x2, x3 in zip(n1, n2, n3)]
    pn4 = [_mm(pp, x4) for pp, x4 in zip(p, n4)]
    y = [pp + x4 + px for pp, x4, px in zip(p, n4, pn4)]
    for off in offs:
        m = [a * off for a in a_list]
        z = [mm + _mm(yy, mm) for yy, mm in zip(y, m)]
        zy = [_mm(zz, yy) for zz, yy in zip(z, y)]
        y = [yy - (zz + zzy) for yy, zz, zzy in zip(y, z, zy)]
    return y


def _split3(x):
    h1 = x.astype(BF16)
    r1 = x - h1.astype(F32)
    h2 = r1.astype(BF16)
    h3 = (r1 - h2.astype(F32)).astype(BF16)
    return h1, h2, h3


def _gdn_kernel(qkv_ref, gate_ref, db_ref, cbuf_ref, s0_ref, cw_ref, cb_ref, alog_ref, dtb_ref, ng_ref,
                o_ref, sfin_ref, xp_scr, s_scr, o_scr, *, t, dk):
    i = pl.program_id(1)
    nb, rows, _ = qkv_ref.shape
    tb = nb * rows
    nh = H_A
    dq = nh * dk

    @pl.when(i == 0)
    def _():
        xp_scr[:, 0:SUBLANES, :] = cbuf_ref[...]
        s_scr[...] = s0_ref[...]

    base = SUBLANES - (CONV_W - 1)
    ys = []
    for n in range(nb):
        x = qkv_ref[n]
        xp_scr[n, SUBLANES:SUBLANES + rows, :] = x
        y = xp_scr[n, base:base + rows, :] * cw_ref[0:1, :]
        for j in range(1, CONV_W):
            y = y + xp_scr[n, base + j:base + j + rows, :] * cw_ref[j:j + 1, :]
        xp_scr[n, 0:SUBLANES, :] = x[rows - SUBLANES:rows, :]
        ys.append(y + cb_ref[...])
    y = jnp.concatenate(ys, axis=0)
    y = y * _sigmoid(y)

    db = db_ref[...].reshape(tb, LANES)
    log_a = -jnp.exp(alog_ref[...]) * _softplus(db + dtb_ref[...])
    beta_all = _sigmoid(db)

    bs = min(tb, GDN_BLOCK)
    lt = t.bit_length() - 1
    r = lax.broadcasted_iota(jnp.int32, (bs, bs), 0)
    c = lax.broadcasted_iota(jnp.int32, (bs, bs), 1)
    same = (r >> lt) == (c >> lt)
    causal_neg = jnp.where(same & (r >= c), 0.0, -jnp.inf)
    strict_f = jnp.where(same & (r > c), 1.0, 0.0)
    tril = jnp.where(same & (r >= c), 1.0, 0.0).astype(BF16)
    striu = jnp.where(same & (r < c), 1.0, 0.0).astype(BF16)
    inv_masks = _inverse_masks(r, c, t)

    blocks = list(range(0, tb, bs))
    pairs = [(bi, h) for bi in range(len(blocks)) for h in range(nh)]
    la3 = [_split3(log_a[b0:b0 + bs, :]) for b0 in blocks]
    gc = [sum(jnp.dot(tril, part, preferred_element_type=F32) for part in parts) for parts in la3]
    rv = [sum(jnp.dot(striu, part, preferred_element_type=F32) for part in parts) for parts in la3]
    eg = [jnp.exp(g) for g in gc]
    erv = [jnp.exp(g) for g in rv]
    e_tot = [jnp.exp(g + g2) for g, g2 in zip(gc, rv)]
    gc_rows = [g.T for g in gc]

    def head_cols(z, bi, h, off):
        b0 = blocks[bi]
        return z[b0:b0 + bs, off + h * dk:off + (h + 1) * dk]

    q = [head_cols(y, bi, h, 0) for bi, h in pairs]
    k = [head_cols(y, bi, h, dq) for bi, h in pairs]
    v = [head_cols(y, bi, h, 2 * dq) for bi, h in pairs]
    q = [z * lax.rsqrt(jnp.sum(z * z, axis=-1, keepdims=True) + NORM_EPS) * (dk ** -0.5) for z in q]
    k = [z * lax.rsqrt(jnp.sum(z * z, axis=-1, keepdims=True) + NORM_EPS) for z in k]
    decay = [jnp.exp(gc[bi][:, h:h + 1] - gc_rows[bi][h:h + 1, :] + causal_neg) for bi, h in pairs]
    beta = [beta_all[blocks[bi]:blocks[bi] + bs, nh + h:nh + h + 1] for bi, h in pairs]
    eg_col = [eg[bi][:, h:h + 1] for bi, h in pairs]
    kk = [_mm_nt(z, z) for z in k]
    qk = [_mm_nt(zq, zk) for zq, zk in zip(q, k)]
    a = [(b * z * d) * strict_f for b, z, d in zip(beta, kk, decay)]
    qk = [z * d for z, d in zip(qk, decay)]
    y_inv = _unit_lower_inverse_offdiag(a, inv_masks)
    rhs = [jnp.concatenate([zv * b, zk * (b * e)], axis=-1) for zv, zk, b, e in zip(v, k, beta, eg_col)]
    sol = [z + _mm(yi, z) for yi, z in zip(y_inv, rhs)]
    q_dec = [z * e for z, e in zip(q, eg_col)]
    k_dec = [z * erv[bi][:, h:h + 1] for z, (bi, h) in zip(k, pairs)]

    chunks = [(bi, r0) for bi in range(len(blocks)) for r0 in range(0, bs, t)]
    n_kw = [[_mm_tn(k_dec[bi * nh + h][r0:r0 + t], sol[bi * nh + h][r0:r0 + t]) for h in range(nh)]
            for bi, r0 in chunks]
    states = [s_scr[0, h] for h in range(nh)]
    starts = []
    for ci, ((bi, r0), nk) in enumerate(zip(chunks, n_kw)):
        if nb > 1:
            states = [s_scr[ci, h] for h in range(nh)]
        starts.append(states)
        drop = [_mm(z[:, dk:], s) for z, s in zip(nk, states)]
        states = [s * e_tot[bi][r0:r0 + 1, h:h + 1] + (z[:, :dk] - d)
                  for h, (s, z, d) in enumerate(zip(states, nk, drop))]
        if nb > 1:
            for h in range(nh):
                s_scr[ci, h] = states[h]
    ws = [[_mm(jnp.concatenate([sol[bi * nh + h][r0:r0 + t, dk:], q_dec[bi * nh + h][r0:r0 + t]], axis=0), st[h])
           for h in range(nh)] for (bi, r0), st in zip(chunks, starts)]
    v_news = [[] for _ in pairs]
    o_inter = [[] for _ in pairs]
    for (bi, r0), wc in zip(chunks, ws):
        for h, z in enumerate(wc):
            v_news[bi * nh + h].append(sol[bi * nh + h][r0:r0 + t, :dk] - z[:t])
            o_inter[bi * nh + h].append(z[t:])
    o = [jnp.concatenate(oi, axis=0) + _mm(z, jnp.concatenate(vn, axis=0))
         for oi, z, vn in zip(o_inter, qk, v_news)]
    o = [z * lax.rsqrt(jnp.mean(z * z, axis=-1, keepdims=True) + NORM_EPS) * ng_ref[...] for z in o]
    for (bi, h), z in zip(pairs, o):
        o_scr[blocks[bi]:blocks[bi] + bs, h * dk:(h + 1) * dk] = z
    if nb == 1:
        for h in range(nh):
            s_scr[0, h] = states[h]

    g = gate_ref[...].reshape(tb, dq)
    o_ref[...] = (o_scr[...] * (g * _sigmoid(g))).astype(o_ref.dtype).reshape(nb, rows, dq)

    @pl.when(i == pl.num_programs(1) - 1)
    def _():
        sfin_ref[...] = s_scr[...]


def _gdn(pa, pdb, cbuf8, s0, params, layer, *, t, cps):
    b, l, _ = pa.shape
    nh, dk = s0.shape[1], s0.shape[2]
    dq = nh * dk
    nb = _seqs_per_step(b, l, t)
    rows = t * cps
    return pl.pallas_call(
        functools.partial(_gdn_kernel, t=t, dk=dk),
        grid=(b // nb, l // rows),
        in_specs=[
            pl.BlockSpec((nb, rows, 3 * dq), lambda bi, i: (bi, i, 0)),
            pl.BlockSpec((nb, rows, dq), lambda bi, i: (bi, i, 3)),
            pl.BlockSpec((nb, rows, LANES), lambda bi, i: (bi, i, 0)),
            pl.BlockSpec((nb, SUBLANES, 3 * dq), lambda bi, i: (bi, 0, 0)),
            pl.BlockSpec((nb, nh, dk, dk), lambda bi, i: (bi, 0, 0, 0)),
        ] + [_layer_spec(w, layer) for w in params],
        out_specs=[
            pl.BlockSpec((nb, rows, dq), lambda bi, i: (bi, i, 0)),
            pl.BlockSpec((nb, nh, dk, dk), lambda bi, i: (bi, 0, 0, 0)),
        ],
        out_shape=[jax.ShapeDtypeStruct((b, l, dq), BF16), jax.ShapeDtypeStruct((b, nh, dk, dk), F32)],
        scratch_shapes=[pltpu.VMEM((nb, SUBLANES + rows, 3 * dq), F32), pltpu.VMEM((nb, nh, dk, dk), F32),
                        pltpu.VMEM((nb * rows, dq), F32)],
        compiler_params=_params("parallel", "arbitrary"),
        name="gdn",
    )(pa, pa, pdb, cbuf8, s0, *params)


def _s5_kernel(u_ref, h0_ref, wb_ref, ap_ref, p8_ref, wc_ref, d_ref, wg_ref, bg_ref,
               o_ref, hl_ref, carry_scr, h_scr, *, t, ns):
    i = pl.program_id(1)
    nb, _, dg = u_ref.shape

    @pl.when(i == 0)
    def _():
        carry_scr[...] = h0_ref[...]

    u = u_ref[...].reshape(nb * t, dg)
    ng = nb * t // SUBLANES
    first = lax.broadcasted_iota(jnp.int32, (SUBLANES, dg), 0) == 0
    u_prev = jnp.where(first, 0.0, pltpu.roll(u.reshape(ng, SUBLANES, dg), 1, 1)).reshape(nb * t, dg)
    x = _mm(jnp.concatenate([u, u_prev], axis=-1), wb_ref[...])
    xr = x[:, :ns].reshape(ng, SUBLANES, ns)
    xi = x[:, ns:].reshape(ng, SUBLANES, ns)
    for lvl in range(2):
        s = 2 << lvl
        pr, pi = ap_ref[lvl, :, :ns], ap_ref[lvl, :, ns:]
        sr = pltpu.roll(xr, s, 1)
        si = pltpu.roll(xi, s, 1)
        xr, xi = xr + (pr * sr - pi * si), xi + (pr * si + pi * sr)
    p8r, p8i = p8_ref[:, :ns], p8_ref[:, ns:]
    for n in range(nb):
        cr, ci = carry_scr[n, :, :ns], carry_scr[n, :, ns:]
        for j in range(n * t // SUBLANES, (n + 1) * t // SUBLANES):
            sl = slice(j * SUBLANES, (j + 1) * SUBLANES)
            br = xr[j] + (p8r * cr - p8i * ci)
            bi = xi[j] + (p8r * ci + p8i * cr)
            h_scr[sl, :ns] = br
            h_scr[sl, ns:] = bi
            cr, ci = br[SUBLANES - 1:SUBLANES], bi[SUBLANES - 1:SUBLANES]
        carry_scr[n, :, :ns] = cr
        carry_scr[n, :, ns:] = ci
    hl_ref[...] = carry_scr[...]

    y = _mm(h_scr[...], wc_ref[...]) + d_ref[...] * u
    y = _gelu_tanh(y)
    z = _mm(y, wg_ref[...]) + bg_ref[...]
    o_ref[...] = (y * _sigmoid(z)).astype(o_ref.dtype).reshape(nb, t, dg)


def _s5(pb, h0, params, layer, *, t):
    b, l, dg = pb.shape
    ns2 = params[0].shape[2]
    ns = ns2 // 2
    nb = _seqs_per_step(b, l, t)
    return pl.pallas_call(
        functools.partial(_s5_kernel, t=t, ns=ns),
        grid=(b // nb, l // t),
        in_specs=[
            pl.BlockSpec((nb, t, dg), lambda bi, i: (bi, i, 0)),
            pl.BlockSpec((nb, 1, ns2), lambda bi, i: (bi, 0, 0)),
        ] + [_layer_spec(w, layer) for w in params],
        out_specs=[
            pl.BlockSpec((nb, t, dg), lambda bi, i: (bi, i, 0)),
            pl.BlockSpec((nb, 1, ns2), lambda bi, i: (bi, 0, 0)),
        ],
        out_shape=[jax.ShapeDtypeStruct((b, l, dg), BF16), jax.ShapeDtypeStruct((b, 1, ns2), F32)],
        scratch_shapes=[pltpu.VMEM((nb, 1, ns2), F32), pltpu.VMEM((nb * t, ns2), F32)],
        compiler_params=_params("parallel", "arbitrary"),
        name="s5",
    )(pb, h0, *params)


def _rglru_kernel(xg_ref, cbuf_ref, h0_ref, cw_ref, cb_ref, wri_ref, bri_ref, lam_ref,
                  o_ref, hl_ref, xp_scr, carry_scr, h_scr, *, t, dg):
    i = pl.program_id(1)
    nb = xg_ref.shape[0]

    @pl.when(i == 0)
    def _():
        xp_scr[:, 0:SUBLANES, :] = cbuf_ref[...]
        carry_scr[...] = h0_ref[...]

    base = SUBLANES - (CONV_W - 1)
    ys = []
    for n in range(nb):
        x = xg_ref[n, :, :dg]
        xp_scr[n, SUBLANES:SUBLANES + t, :] = x
        y = xp_scr[n, base:base + t, :] * cw_ref[0:1, :]
        for j in range(1, CONV_W):
            y = y + xp_scr[n, base + j:base + j + t, :] * cw_ref[j:j + 1, :]
        xp_scr[n, 0:SUBLANES, :] = x[t - SUBLANES:t, :]
        ys.append(y + cb_ref[...])
    y = jnp.concatenate(ys, axis=0)
    gb = xg_ref[...][:, :, dg:].reshape(nb * t, dg)

    ri = _mm(y, wri_ref[...]) + bri_ref[...]
    rg = _sigmoid(ri[:, :dg])
    ig = _sigmoid(ri[:, dg:])
    log_a = (-RG_C * rg) * _softplus(-lam_ref[...])
    a = jnp.exp(log_a)
    th = jnp.tanh(log_a)
    xin = jnp.sqrt(-2.0 * th / (1.0 - th)) * (ig * y)

    ng = nb * t // SUBLANES
    a = a.reshape(ng, SUBLANES, dg)
    xin = xin.reshape(ng, SUBLANES, dg)
    row = lax.broadcasted_iota(jnp.int32, (SUBLANES, dg), 0)
    for lvl in range(3):
        s = 1 << lvl
        keep = row >= s
        a_s = jnp.where(keep, pltpu.roll(a, s, 1), 1.0)
        x_s = jnp.where(keep, pltpu.roll(xin, s, 1), 0.0)
        xin = a * x_s + xin
        a = a * a_s
    for n in range(nb):
        cr = carry_scr[n]
        for j in range(n * t // SUBLANES, (n + 1) * t // SUBLANES):
            hb = xin[j] + a[j] * cr
            h_scr[j * SUBLANES:(j + 1) * SUBLANES, :] = hb
            cr = hb[SUBLANES - 1:SUBLANES]
        carry_scr[n] = cr
    hl_ref[...] = carry_scr[...]
    o_ref[...] = (h_scr[...] * _gelu_tanh(gb)).astype(o_ref.dtype).reshape(nb, t, dg)


def _rglru(pc, cbuf8, h0, params, layer, *, t):
    b, l, dg2 = pc.shape
    dg = dg2 // 2
    nb = _seqs_per_step(b, l, t)
    return pl.pallas_call(
        functools.partial(_rglru_kernel, t=t, dg=dg),
        grid=(b // nb, l // t),
        in_specs=[
            pl.BlockSpec((nb, t, dg2), lambda bi, i: (bi, i, 0)),
            pl.BlockSpec((nb, SUBLANES, dg), lambda bi, i: (bi, 0, 0)),
            pl.BlockSpec((nb, 1, dg), lambda bi, i: (bi, 0, 0)),
        ] + [_layer_spec(w, layer) for w in params],
        out_specs=[
            pl.BlockSpec((nb, t, dg), lambda bi, i: (bi, i, 0)),
            pl.BlockSpec((nb, 1, dg), lambda bi, i: (bi, 0, 0)),
        ],
        out_shape=[jax.ShapeDtypeStruct((b, l, dg), BF16), jax.ShapeDtypeStruct((b, 1, dg), F32)],
        scratch_shapes=[pltpu.VMEM((nb, SUBLANES + t, dg), F32), pltpu.VMEM((nb, 1, dg), F32),
                        pltpu.VMEM((nb * t, dg), F32)],
        compiler_params=_params("parallel", "arbitrary"),
        name="rglru",
    )(pc, cbuf8, h0, *params)


def _band_prompt_kernel(q_ref, kc_ref, vc_ref, kp_ref, vp_ref, bias_ref, o_ref, *, qb, qp, hd):
    span = N_PREV * CHUNK
    q = q_ref[0] * (hd ** -0.5)
    k = jnp.concatenate([kp_ref[0], kc_ref[0]], axis=0)
    v = jnp.concatenate([vp_ref[0], vc_ref[0]], axis=0)
    units = [(h, p) for h in range(H_D) for p in range(qb // qp)]
    col = lambda h: slice(h * hd, (h + 1) * hd)
    win = lambda p: slice(p * qp, span + (p + 1) * qp)
    s, e, den, o = {}, {}, {}, []
    for n in range(len(units) + 2):
        if n < len(units):
            h, p = units[n]
            s[n] = _mm_nt(q[p * qp:(p + 1) * qp, col(h)], k[win(p), col(h)]) + bias_ref[0, p, h]
        if 0 <= n - 1 < len(units):
            z = s.pop(n - 1)
            e[n - 1] = jnp.exp(z - jnp.max(z, axis=-1, keepdims=True))
            den[n - 1] = jnp.sum(e[n - 1], axis=-1, keepdims=True)
        if 0 <= n - 2 < len(units):
            h, p = units[n - 2]
            o.append(_mm(e.pop(n - 2), v[win(p), col(h)]) / den.pop(n - 2))
    npc = qb // qp
    o_ref[0] = jnp.concatenate([jnp.concatenate(o[h * npc:(h + 1) * npc], axis=0) for h in range(H_D)],
                               axis=-1).astype(o_ref.dtype)


def _band_prompt(pd, bias, layer, *, qb, qp):
    b, l, w3 = pd.shape
    w = w3 // 3
    hd = w // H_D
    span = N_PREV * CHUNK
    prev = lambda bi, i: jnp.maximum(i * (qb // span) - 1, 0)
    return pl.pallas_call(
        functools.partial(_band_prompt_kernel, qb=qb, qp=qp, hd=hd),
        grid=(b, l // qb),
        in_specs=[
            pl.BlockSpec((1, qb, w), lambda bi, i: (bi, i, 0)),
            pl.BlockSpec((1, qb, w), lambda bi, i: (bi, i, 1)),
            pl.BlockSpec((1, qb, w), lambda bi, i: (bi, i, 2)),
            pl.BlockSpec((1, span, w), lambda bi, i: (bi, prev(bi, i), 1)),
            pl.BlockSpec((1, span, w), lambda bi, i: (bi, prev(bi, i), 2)),
            pl.BlockSpec((None, 1) + bias.shape[2:], lambda bi, i: (layer, jnp.minimum(i, 1), 0, 0, 0, 0)),
        ],
        out_specs=pl.BlockSpec((1, qb, w), lambda bi, i: (bi, i, 0)),
        out_shape=jax.ShapeDtypeStruct((b, l, w), BF16),
        compiler_params=_params("parallel", "arbitrary"),
        name="band_prompt",
    )(pd, pd, pd, pd, pd, bias)


def _band_sample_kernel(qkv_ref, kt_ref, vt_ref, bc_ref, bn_ref, o_ref, *, hd):
    nb = qkv_ref.shape[0]
    w = H_D * hd
    units = [(n, h, slice(h * hd, (h + 1) * hd)) for n in range(nb) for h in range(H_D)]
    q = [qkv_ref[n, :, :w] * (hd ** -0.5) for n in range(nb)]
    kn = [qkv_ref[n, :, w:2 * w] for n in range(nb)]
    vn = [qkv_ref[n, :, 2 * w:] for n in range(nb)]
    sc = [_mm(q[n][:, sl], kt_ref[n, h]) + bc_ref[h] for n, h, sl in units]
    sn = [_mm_nt(q[n][:, sl], kn[n][:, sl]) + bn_ref[h] for n, h, sl in units]
    m = [jnp.maximum(jnp.max(c, axis=-1, keepdims=True), jnp.max(z, axis=-1, keepdims=True)) for c, z in zip(sc, sn)]
    pc = [jnp.exp(c - z) for c, z in zip(sc, m)]
    pn = [jnp.exp(c - z) for c, z in zip(sn, m)]
    den = [jnp.sum(c, axis=-1, keepdims=True) + jnp.sum(z, axis=-1, keepdims=True) for c, z in zip(pc, pn)]
    outs = [(_mm_nt(c, vt_ref[n, h]) + _mm(z, vn[n][:, sl])) / d
            for (n, h, sl), c, z, d in zip(units, pc, pn, den)]
    for n in range(nb):
        o_ref[n] = jnp.concatenate(outs[n * H_D:(n + 1) * H_D], axis=-1).astype(o_ref.dtype)


def _band_sample(pd, kt_cache, vt_cache, layer, bias_c, bias_n):
    b, l, w3 = pd.shape
    w = w3 // 3
    hd = w // H_D
    rows = kt_cache.shape[-1]
    nb = XATTN_SEQS_PER_STEP
    while b % nb:
        nb -= 1
    cache_spec = pl.BlockSpec((None, nb, H_D, hd, rows), lambda bi: (layer, bi, 0, 0, 0))
    return pl.pallas_call(
        functools.partial(_band_sample_kernel, hd=hd),
        grid=(b // nb,),
        in_specs=[
            pl.BlockSpec((nb, l, w3), lambda bi: (bi, 0, 0)),
            cache_spec, cache_spec, _layer_spec(bias_c, layer), _layer_spec(bias_n, layer),
        ],
        out_specs=pl.BlockSpec((nb, l, w), lambda bi: (bi, 0, 0)),
        out_shape=jax.ShapeDtypeStruct((b, l, w), BF16),
        compiler_params=_params("parallel"),
        name="band_sample",
    )(pd, kt_cache, vt_cache, bias_c, bias_n)


def _row_parts(rows, parts):
    if rows % (parts * 2 * SUBLANES):
        parts = 1
    step = rows // parts
    return [slice(n * step, (n + 1) * step) for n in range(parts)]


def _mix_xattn_kernel(x_ref, oa_ref, ob_ref, oc_ref, od_ref, wm_ref, mk_ref, mv_ref, wq_ref, wo_ref, g_ref, b_ref,
                      o_ref, *, alpha, hd, head_axis):
    nb, tm, d = x_ref.shape
    dg = oa_ref.shape[2]
    heads = [slice(h * hd, (h + 1) * hd) for h in range(H_X)]
    if head_axis:
        n_mem = mk_ref.shape[1]
        mk = [mk_ref[n].reshape(n_mem * H_X, hd).astype(BF16) for n in range(nb)]
        mv = [mv_ref[n].reshape(n_mem * H_X, hd).astype(BF16) for n in range(nb)]
    else:
        mk = [[mk_ref[n, :, hs].astype(BF16) for hs in heads] for n in range(nb)]
        mv = [[mv_ref[n, :, hs].astype(BF16) for hs in heads] for n in range(nb)]
    if nb == 1:
        parts = _row_parts(tm, 4)
        rows_of = lambda ref, sl: ref[0, sl, :]
        units = [(n, slice(0, sl.stop - sl.start), 0) for n, sl in enumerate(parts)]
    else:
        parts = [slice(0, nb * tm)]
        rows_of = lambda ref, sl: ref[...].reshape(nb * tm, ref.shape[2])
        units = [(0, slice(n * tm, (n + 1) * tm), n) for n in range(nb)]
    mix = [sum(jnp.dot(rows_of(r, sl), wm_ref[n * dg:(n + 1) * dg, :], preferred_element_type=F32)
               for n, r in enumerate((oa_ref, ob_ref, oc_ref, od_ref))) for sl in parts]
    xs = [_layer_norm(alpha * rows_of(x_ref, sl) + z, g_ref[0:1, :], b_ref[0:1, :]) for sl, z in zip(parts, mix)]
    q = [(_mm(x, wq_ref[...]) * (hd ** -0.5)).astype(BF16) for x in xs]
    if head_axis:
        rows_u = units[0][1].stop - units[0][1].start
        row = lax.broadcasted_iota(jnp.int32, (H_X * rows_u, n_mem * H_X), 0)
        r_head = sum((row >= h * rows_u).astype(jnp.int32) for h in range(1, H_X))
        c_head = lax.broadcasted_iota(jnp.int32, (H_X * rows_u, n_mem * H_X), 1) & (H_X - 1)
        own_head = jnp.where(r_head == c_head, 0.0, -jnp.inf)
        qs = [jnp.concatenate([q[p][rows, hs] for hs in heads], axis=0) for p, rows, _ in units]
        s = [_mm_nt(z, mk[n]) + own_head for z, (_, _, n) in zip(qs, units)]
        e = [jnp.exp(z - jnp.max(z, axis=-1, keepdims=True)) for z in s]
        pr = [z * (1.0 / jnp.sum(z, axis=-1, keepdims=True)) for z in e]
        pv = [_mm(z, mv[n]) for z, (_, _, n) in zip(pr, units)]
        pv = [jnp.concatenate([z[h * rows_u:(h + 1) * rows_u] for h in range(H_X)], axis=-1) for z in pv]
    else:
        s = [[_mm_nt(q[p][rows, hs], kh) for hs, kh in zip(heads, mk[n])] for p, rows, n in units]
        e = [[jnp.exp(z - jnp.max(z, axis=-1, keepdims=True)) for z in su] for su in s]
        pr = [[z * (1.0 / jnp.sum(z, axis=-1, keepdims=True)) for z in eu] for eu in e]
        pv = [jnp.concatenate([_mm(z, vh) for z, vh in zip(pu, mv[n])], axis=-1) for pu, (_, _, n) in zip(pr, units)]
    pv = [jnp.concatenate([z for z, (p, _, _) in zip(pv, units) if p == n], axis=0) for n in range(len(parts))]
    att = [_mm(z, wo_ref[...]) for z in pv]
    out = [_layer_norm(alpha * x + z, g_ref[1:2, :], b_ref[1:2, :]) for x, z in zip(xs, att)]
    if nb == 1:
        for sl, z in zip(parts, out):
            o_ref[0, sl, :] = z
    else:
        o_ref[...] = out[0].reshape(nb, tm, d)


def _mix_xattn(x, mixed, w_out, mk, mv, wq, wo, layer, g, b, *, alpha, rows=1024):
    bsz, l, d = x.shape
    dg = mixed[0].shape[2]
    hd = d // H_X
    tm = _tile(l, rows)
    nb = max(1, min(bsz, XATTN_SEQS_PER_STEP, rows // l)) if tm == l else 1
    while bsz % nb:
        nb -= 1
    row = lambda width: pl.BlockSpec((nb, tm, width), lambda bi, i: (bi, i, 0))
    if mk.ndim == 3:
        mem_spec = pl.BlockSpec((nb,) + mk.shape[1:], lambda bi, i: (bi, 0, 0))
    else:
        mem_spec = pl.BlockSpec((None, nb) + mk.shape[2:], lambda bi, i: (layer, bi, 0, 0, 0))
    return pl.pallas_call(
        functools.partial(_mix_xattn_kernel, alpha=alpha, hd=hd, head_axis=mk.ndim != 3),
        grid=(bsz // nb, l // tm),
        in_specs=[
            row(d), row(dg), row(dg), row(dg), row(dg), _layer_spec(w_out, layer),
            mem_spec, mem_spec,
            _layer_spec(wq, layer), _layer_spec(wo, layer), _layer_spec(g, layer), _layer_spec(b, layer),
        ],
        out_specs=row(d),
        out_shape=jax.ShapeDtypeStruct((bsz, l, d), F32),
        compiler_params=_params("parallel", "parallel"),
        name="mix_xattn_ln",
    )(x, *mixed, w_out, mk, mv, wq, wo, g, b)


def _mlp_kernel(x_ref, w1_ref, w2_ref, g_ref, b_ref, o_ref, xb_scr, *, alpha, parts):
    f = pl.program_id(1)
    sl = _row_parts(x_ref.shape[0], parts)
    last = pl.num_programs(1) - 1

    def step(first, finish):
        if first:
            for s in sl:
                xb_scr[s, :] = x_ref[s, :].astype(BF16)
        hid = [jnp.maximum(jnp.dot(xb_scr[s, :], w1_ref[...], preferred_element_type=F32), 0.0) for s in sl]
        act = [(z * z).astype(BF16) for z in hid]
        for s, z in zip(sl, act):
            acc = jnp.dot(z, w2_ref[...], preferred_element_type=F32)
            if not first:
                acc = o_ref[s, :] + acc
            o_ref[s, :] = _layer_norm(alpha * x_ref[s, :] + acc, g_ref[2:3, :], b_ref[2:3, :]) if finish else acc

    pl.when(f == 0)(lambda: step(True, False))
    pl.when(jnp.logical_and(f > 0, f < last))(lambda: step(False, False))
    pl.when(f == last)(lambda: step(False, True))


def _mlp(x, w1, w2, layer, g, b, *, alpha, tm=1024, tf=2048):
    m, d = x.shape
    dff = w1.shape[2]
    tf = _tile(dff, tf if m >= tm else tf // 2)
    tm = _tile(m, tm)
    assert dff // tf >= 2, "the kernel distinguishes the first and the last hidden tile"
    return pl.pallas_call(
        functools.partial(_mlp_kernel, alpha=alpha, parts=4),
        grid=(m // tm, dff // tf),
        in_specs=[
            pl.BlockSpec((tm, d), lambda i, f: (i, 0)),
            _layer_spec(w1, layer, (d, tf), lambda i, f: (0, f)),
            _layer_spec(w2, layer, (tf, d), lambda i, f: (f, 0)),
            _layer_spec(g, layer), _layer_spec(b, layer),
        ],
        out_specs=pl.BlockSpec((tm, d), lambda i, f: (i, 0)),
        out_shape=jax.ShapeDtypeStruct((m, d), F32),
        scratch_shapes=[pltpu.VMEM((tm, d), BF16)],
        compiler_params=_params("parallel", "arbitrary"),
        name="mlp_ln",
    )(x, w1, w2, g, b)


def _block_diag(blocks):
    g, r, c = blocks.shape
    eye = jnp.eye(g, dtype=blocks.dtype)
    return (eye[:, None, :, None] * blocks[:, :, None, :]).reshape(g * r, g * c)


def _pad_lanes(v):
    return jnp.pad(v, (0, LANES - v.shape[0]))[None, :]


def _s5_params(lam_re, lam_im, log_dt, b_re, b_im, c_re, c_im):
    dt = jnp.exp(log_dt)[:, None]
    mag = jnp.exp(lam_re * dt)
    ar, ai = mag * jnp.cos(lam_im * dt), mag * jnp.sin(lam_im * dt)
    den = lam_re * lam_re + lam_im * lam_im
    fr = ((ar - 1.0) * lam_re + ai * lam_im) / den
    fi = (ai * lam_re - (ar - 1.0) * lam_im) / den
    bbr = fr[..., None] * b_re - fi[..., None] * b_im
    bbi = fr[..., None] * b_im + fi[..., None] * b_re
    wb = jnp.concatenate([_block_diag(jnp.swapaxes(bbr, 1, 2)), _block_diag(jnp.swapaxes(bbi, 1, 2))], axis=1)
    wc = jnp.concatenate([_block_diag(jnp.swapaxes(c_re, 1, 2)), -_block_diag(jnp.swapaxes(c_im, 1, 2))], axis=0)
    ar, ai = ar.reshape(-1), ai.reshape(-1)
    pows = [(ar, ai)]
    for _ in range(SUBLANES - 1):
        pr, pi = pows[-1]
        pows.append((pr * ar - pi * ai, pr * ai + pi * ar))
    cat = lambda idx: jnp.stack([jnp.concatenate(pows[n]) for n in idx], axis=0)
    row = jnp.arange(SUBLANES)[:, None]
    shift_pows = jnp.stack([jnp.where(row >= s, cat((s - 1,)), 0.0) for s in (2, 4)])
    wbr, wbi = wb[:, :ar.shape[0]], wb[:, ar.shape[0]:]
    wb_lag = jnp.concatenate([ar * wbr - ai * wbi, ar * wbi + ai * wbr], axis=1)
    wb2 = jnp.concatenate([wb, wb_lag], axis=0)
    return wb2.astype(BF16), shift_pows, cat(range(SUBLANES)), wc.astype(BF16)


def _rel_bias_table(table, n_rows, n_cols, offset):
    tab = table.astype(F32).T
    rel_min, rel_max = offset - (n_cols - 1), offset + n_rows - 1
    lo, hi = max(rel_min, -REL_CLIP), min(rel_max, REL_CLIP)
    parts = [jnp.repeat(tab[:, :1], lo - rel_min, axis=1), tab[:, lo + REL_CLIP:hi + REL_CLIP + 1],
             jnp.repeat(tab[:, -1:], rel_max - hi, axis=1)]
    ext = jnp.concatenate(parts, axis=1)
    length = n_rows + n_cols - 1
    flipped = jnp.pad(ext[:, ::-1], ((0, 0), (0, 1)))
    shifted = jnp.tile(flipped, (1, n_rows))[:, :n_rows * length].reshape(-1, n_rows, length)
    return shifted[:, :, n_rows - 1:n_rows - 1 + n_cols]


def _clipped_bias_run(tab, rel_first, count):
    n_hi = min(max(rel_first - REL_CLIP, 0), count)
    n_lo = min(max(-REL_CLIP - (rel_first - count + 1), 0), count)
    mid = count - n_hi - n_lo
    top = min(rel_first, REL_CLIP) + REL_CLIP
    return jnp.concatenate([jnp.repeat(tab[:, -1:], n_hi, axis=1), tab[:, top - mid + 1:top + 1][:, ::-1],
                            jnp.repeat(tab[:, :1], n_lo, axis=1)], axis=1)


def _band_bias_kernel(f_ref, o_ref, *, qp, span):
    pieces, width = o_ref.shape[1], o_ref.shape[3]
    bias = pltpu.roll(jnp.broadcast_to(f_ref[...], (qp, f_ref.shape[1])), 0, 1, stride=1, stride_axis=0)[:, :width]
    shift = CHUNK.bit_length() - 1
    r_chunk = lax.broadcasted_iota(jnp.int32, (qp, width), 0) >> shift
    col = lax.broadcasted_iota(jnp.int32, (qp, width), 1)
    in_band = ((col >> shift) >= r_chunk) & ((col >> shift) <= r_chunk + N_PREV)
    for p in range(pieces):
        o_ref[0, p] = jnp.where(in_band & (col >= span - p * qp), bias, -jnp.inf)
        o_ref[1, p] = jnp.where(in_band, bias, -jnp.inf)


def _band_prompt_bias(tables, qb, qp):
    depth, _, nh = tables.shape
    span = N_PREV * CHUNK
    width = span + qp
    period = -(-(width + qp - 1) // LANES) * LANES
    tab = jnp.swapaxes(tables.astype(F32), 1, 2).reshape(depth * nh, -1)
    f = jnp.concatenate([_clipped_bias_run(tab, span, width), jnp.zeros((depth * nh, period - width - (qp - 1)), F32),
                         _clipped_bias_run(tab, span + qp - 1, qp - 1)], axis=1).reshape(depth, nh, 1, period)
    pieces = qb // qp
    return pl.pallas_call(
        functools.partial(_band_bias_kernel, qp=qp, span=span),
        grid=(depth, nh),
        in_specs=[pl.BlockSpec((None, None, 1, period), lambda l, h: (l, h, 0, 0))],
        out_specs=pl.BlockSpec((None, 2, pieces, None, qp, width), lambda l, h: (l, 0, 0, h, 0, 0)),
        out_shape=jax.ShapeDtypeStruct((depth, 2, pieces, nh, qp, width), F32),
        compiler_params=_params("parallel", "parallel"),
        name="band_bias",
    )(f)


def _trunk_layer(x, mem_k, mem_v, gdn_conv, gdn_s, s5_h, rg_conv, rg_h, band_k, band_v, p, *, alpha):
    b, l, d = x.shape
    dg = d // N_MIX
    m = b * l
    layer = p["layer"]
    pa, pb, pc, pd, pdb = _inproj(x.reshape(m, d), p["w_in"], layer, (4 * dg, dg, 2 * dg, 3 * dg, LANES))
    pa, pb, pc, pd, pdb = [t.reshape(b, l, -1) for t in (pa, pb, pc, pd, pdb)]

    pad8 = lambda buf: jnp.pad(buf, ((0, 0), (SUBLANES - (CONV_W - 1), 0), (0, 0)))
    chunk = CHUNK if l % CHUNK == 0 else l
    cps = max(1, min(GDN_CHUNKS_PER_STEP, l // chunk))
    o_a, gdn_s_new = _gdn(pa, pdb, pad8(gdn_conv), gdn_s, p["gdn"], layer, t=chunk, cps=cps)
    gdn_conv_new = pa[:, l - (CONV_W - 1):, :3 * dg]

    ns = p["s5"][0].shape[2] // 2
    h0 = jnp.concatenate([s5_h[..., 0].reshape(b, 1, ns), s5_h[..., 1].reshape(b, 1, ns)], axis=-1)
    o_b, h_last = _s5(pb, h0, p["s5"], layer, t=_tile(l, SCAN_ROWS_PER_STEP))
    s5_h_new = jnp.stack([h_last[:, 0, :ns].reshape(s5_h.shape[:-1]), h_last[:, 0, ns:].reshape(s5_h.shape[:-1])],
                         axis=-1)

    o_c, rg_last = _rglru(pc, pad8(rg_conv), rg_h[:, None, :], p["rglru"], layer, t=_tile(l, SCAN_ROWS_PER_STEP))
    rg_conv_new = pc[:, l - (CONV_W - 1):, :dg]
    rg_h_new = rg_last[:, 0, :]

    hd = dg // H_D
    if band_k is None:
        assert l % BAND_BLOCK == 0, "prompt length must be a multiple of the band-attention block"
        o_d = _band_prompt(pd, p["band_bias_prompt"], layer, qb=BAND_BLOCK, qp=BAND_PIECE)
        keep = min(N_PREV * CHUNK, l)
    else:
        o_d = _band_sample(pd, band_k, band_v, layer, p["band_bias_cache"], p["band_bias_new"])
        keep = l
    band_k_new = pd[:, l - keep:, dg:2 * dg].reshape(b, keep, H_D, hd)
    band_v_new = pd[:, l - keep:, 2 * dg:].reshape(b, keep, H_D, hd)

    x3 = _mix_xattn(x, (o_a, o_b, o_c, o_d), p["w_out"], mem_k, mem_v, p["xa_w_q"], p["xa_w_o"], layer,
                    p["ln_g"], p["ln_b"], alpha=alpha)
    x4 = _mlp(x3.reshape(m, d), p["mlp_w1"], p["mlp_w2"], layer, p["ln_g"], p["ln_b"], alpha=alpha)
    return x4.reshape(b, l, d), (gdn_conv_new, gdn_s_new, s5_h_new, rg_conv_new, rg_h_new, band_k_new, band_v_new)


def kernel(x_prompt, x_sample, state_gdn_conv, state_gdn, state_s5, state_rglru_conv, state_rglru, cache_band_k, cache_band_v, cache_mem_k, cache_mem_v, mem_prompt, w_in, w_out, ln_g, ln_b, gdn_conv_w, gdn_conv_b, gdn_a_log, gdn_dt_bias, gdn_norm_g, s5_lam_re, s5_lam_im, s5_log_dt, s5_b_re, s5_b_im, s5_c_re, s5_c_im, s5_d, s5_w_glu, s5_b_glu, rg_conv_w, rg_conv_b, rg_w_r, rg_b_r, rg_w_i, rg_b_i, rg_lam, band_rel_bias, xa_w_q, xa_w_k, xa_w_v, xa_w_o, mlp_w1, mlp_w2):
    depth = w_in.shape[0]
    bp, lp, d = x_prompt.shape
    bs, ls, _ = x_sample.shape
    n_mem = mem_prompt.shape[1]
    dg = d // N_MIX
    hd_x = d // H_X
    alpha = (2.0 * depth) ** 0.25
    band_rows = cache_band_k.shape[2]

    sizes = (3 * dg, dg, H_A, H_A, dg, dg, dg, 3 * dg)
    offs = [0]
    for s in sizes:
        offs.append(offs[-1] + s)

    w_db = jnp.pad(w_in[:, :, offs[2]:offs[4]], ((0, 0), (0, 0), (0, LANES - 2 * H_A)))
    w_in_bf = jnp.concatenate([w_in[:, :, offs[0]:offs[2]], w_in[:, :, offs[4]:offs[5]], w_in[:, :, offs[5]:offs[7]],
                               w_in[:, :, offs[7]:offs[8]], w_db], axis=2).astype(BF16)
    w_out_bf, xa_w_q_bf, xa_w_k_bf, xa_w_v_bf, xa_w_o_bf, mlp_w1_bf, mlp_w2_bf = [
        w.astype(BF16) for w in (w_out, xa_w_q, xa_w_k, xa_w_v, xa_w_o, mlp_w1, mlp_w2)]

    band_kt = jnp.transpose(cache_band_k, (0, 1, 3, 4, 2))
    band_vt = jnp.transpose(cache_band_v, (0, 1, 3, 4, 2))

    over_layers = jax.vmap
    s5_wb, s5_ap, s5_p8, s5_wc = over_layers(_s5_params)(s5_lam_re, s5_lam_im, s5_log_dt, s5_b_re, s5_b_im,
                                                         s5_c_re, s5_c_im)
    rg_wri = jnp.concatenate([over_layers(_block_diag)(rg_w_r), over_layers(_block_diag)(rg_w_i)], axis=2)
    shared = {
        "w_in": w_in_bf, "w_out": w_out_bf, "ln_g": ln_g, "ln_b": ln_b,
        "gdn": (gdn_conv_w, gdn_conv_b[:, None, :], over_layers(_pad_lanes)(gdn_a_log),
                over_layers(_pad_lanes)(gdn_dt_bias), gdn_norm_g[:, None, :]),
        "s5": (s5_wb, s5_ap, s5_p8, s5_wc, s5_d.reshape(depth, 1, dg), s5_w_glu.astype(BF16), s5_b_glu[:, None, :]),
        "rglru": (rg_conv_w, rg_conv_b[:, None, :], rg_wri.astype(BF16),
                  jnp.concatenate([rg_b_r, rg_b_i], axis=1)[:, None, :], rg_lam[:, None, :]),
        "band_bias_prompt": _band_prompt_bias(band_rel_bias, BAND_BLOCK, BAND_PIECE),
        "band_bias_cache": over_layers(lambda t: _rel_bias_table(t, ls, band_rows, band_rows))(band_rel_bias),
        "band_bias_new": over_layers(lambda t: _rel_bias_table(t, ls, ls, 0))(band_rel_bias),
        "xa_w_q": xa_w_q_bf, "xa_w_o": xa_w_o_bf, "mlp_w1": mlp_w1_bf, "mlp_w2": mlp_w2_bf,
    }

    xp, xs = x_prompt, x_sample
    p_states, s_states = [], []
    for l in range(depth):
        p = dict(shared, layer=l)
        mem2 = mem_prompt.reshape(bp * n_mem, d)
        mk = _matmul(mem2, xa_w_k_bf, l).reshape(bp, n_mem, d)
        mv = _matmul(mem2, xa_w_v_bf, l).reshape(bp, n_mem, d)
        xp, st_p = _trunk_layer(
            xp, mk, mv,
            jnp.zeros((bp, CONV_W - 1, 3 * dg), F32), jnp.zeros((bp, H_A, dg // H_A, dg // H_A), F32),
            jnp.zeros((bp, dg // S5_CH, P_B, 2), F32), jnp.zeros((bp, CONV_W - 1, dg), F32),
            jnp.zeros((bp, dg), F32), None, None, p, alpha=alpha)
        p_states.append(st_p + (mk.reshape(bp, n_mem, H_X, hd_x), mv.reshape(bp, n_mem, H_X, hd_x)))
        xs, st_s = _trunk_layer(
            xs, cache_mem_k, cache_mem_v,
            state_gdn_conv[l], state_gdn[l], state_s5[l], state_rglru_conv[l], state_rglru[l],
            band_kt, band_vt, p, alpha=alpha)
        s_states.append(st_s)

    def stk(states, i):
        return jnp.stack([st[i] for st in states], axis=0)

    return (xp, xs,
            stk(p_states, 0), stk(p_states, 1), stk(p_states, 2), stk(p_states, 3), stk(p_states, 4),
            stk(p_states, 5), stk(p_states, 6), stk(p_states, 7), stk(p_states, 8),
            stk(s_states, 0), stk(s_states, 1), stk(s_states, 2), stk(s_states, 3), stk(s_states, 4),
            stk(s_states, 5), stk(s_states, 6))
```
